```python
import jax
import jax.numpy as jnp
from jax import lax
import numpy as np

D_MODEL = 1024
BATCH = 4
SEQ = 4096
DEPTH = 2
DEC_BATCH = 128
DEC_SEQ = 4
PAST_LEN = 2048
PAGE_SIZE = 128

N_BRANCH = 3
W_BRANCH = D_MODEL // 2
HEAD_DIM = 64
A_GROUPS = 4
A_GROUP_W = W_BRANCH // A_GROUPS
A_CHUNK = 128
B_HEADS = W_BRANCH // HEAD_DIM
B_KV_HEADS = 2
IDX_HEADS = 4
IDX_DIM = 64
TOPK_MAX = 256
C_HEADS = W_BRANCH // HEAD_DIM
ROT_FRAC = 4
ROPE_THETA = 500000.0
D_FF = 2816
Q_BLOCK = 128
EPS = 1e-6
IN_WIDTHS = (N_BRANCH * D_MODEL, 2 * W_BRANCH,
             B_HEADS * HEAD_DIM, B_KV_HEADS * HEAD_DIM, B_KV_HEADS * HEAD_DIM,
             IDX_HEADS * IDX_DIM, IDX_DIM, IDX_HEADS,
             C_HEADS * HEAD_DIM, C_HEADS * HEAD_DIM, C_HEADS * HEAD_DIM, C_HEADS)
N_IN = sum(IN_WIDTHS)

kernel_name = "hybrid_gated_gmlp_dsa_fox_decode_step"

F32 = jnp.float32


def rms_norm(x, g):
    xf = x.astype(F32)
    y = xf * lax.rsqrt(jnp.mean(xf * xf, axis=-1, keepdims=True) + EPS)
    return (y * g.astype(F32)).astype(x.dtype)


def layer_norm(x, g, b):
    xf = x.astype(F32)
    mu = jnp.mean(xf, axis=-1, keepdims=True)
    var = jnp.mean(jnp.square(xf - mu), axis=-1, keepdims=True)
    return ((xf - mu) * lax.rsqrt(var + EPS) * g.astype(F32) + b.astype(F32)).astype(x.dtype)


def swiglu(x, wi, wo):
    g, u = jnp.split(x @ wi, 2, axis=-1)
    return (jax.nn.silu(g) * u) @ wo


def partial_rope(x, pos):
    d = x.shape[-1]
    rot = d // ROT_FRAC
    half = rot // 2
    inv = ROPE_THETA ** (-jnp.arange(half, dtype=F32) / half)
    ang = pos.astype(F32)[:, None] * inv[None, :]
    cos = jnp.cos(ang)[:, None, :]
    sin = jnp.sin(ang)[:, None, :]
    xf = x.astype(F32)
    x1, x2, rest = xf[..., :half], xf[..., half:rot], xf[..., rot:]
    out = jnp.concatenate([x1 * cos - x2 * sin, x2 * cos + x1 * sin, rest], axis=-1)
    return out.astype(x.dtype)


def split_columns(h):
    cuts = [int(c) for c in np.cumsum(IN_WIDTHS)[:-1]]
    return jnp.split(h, cuts, axis=-1)


def gather_pages(pool, page_table):
    g = pool[page_table]
    return g.reshape(g.shape[0], g.shape[1] * g.shape[2], *g.shape[3:])


def sweep_query_blocks(fn, q_args, qpos):
    tq = qpos.shape[0]
    if tq <= Q_BLOCK or tq % Q_BLOCK:
        return fn(*q_args, qpos)
    nb = tq // Q_BLOCK
    blocks = tuple(jnp.moveaxis(a.reshape(a.shape[0], nb, Q_BLOCK, *a.shape[2:]), 1, 0) for a in q_args)
    out = lax.map(lambda args: fn(*args), blocks + (qpos.reshape(nb, Q_BLOCK),))
    out = jnp.moveaxis(out, 0, 1)
    return out.reshape(out.shape[0], tq, *out.shape[3:])


def chunk_spatial_gating(h_a, ln_g, ln_b, w_s, b_s):
    bn, t, _ = h_a.shape
    u, v = jnp.split(jax.nn.gelu(h_a), 2, axis=-1)
    v = layer_norm(v, ln_g, ln_b)
    c = min(t, A_CHUNK)
    vc = v.reshape(bn, t // c, c, A_GROUPS, A_GROUP_W).astype(F32)
    ws = jnp.where(jnp.tril(jnp.ones((c, c), dtype=bool)), w_s[:, :c, :c].astype(F32), 0.0)
    mixed = jnp.einsum("gts,bnsgw->bntgw", ws, vc) + b_s[:, :c].T.astype(F32)[:, :, None]
    mixed = mixed.reshape(bn, t, W_BRANCH).astype(h_a.dtype)
    return u * mixed, v


def dsa_block(q, qi, wi, qpos, k, v, ki, kpos, topk):
    s = jnp.einsum("bthd,bsd->bths", qi.astype(F32), ki.astype(F32)) * IDX_DIM ** -0.5
    score = jnp.einsum("bth,bths->bts", wi.astype(F32), jax.nn.relu(s))
    causal = kpos[None, :] <= qpos[:, None]
    score = jnp.where(causal[None], score, -jnp.inf)
    _, idx = lax.top_k(score, topk)
    valid = kpos[idx] <= qpos[None, :, None]
    ksel = jax.vmap(lambda kb, ib: kb[ib])(k, idx)
    vsel = jax.vmap(lambda vb, ib: vb[ib])(v, idx)
    bn, tq, h, dh = q.shape
    qg = q.reshape(bn, tq, B_KV_HEADS, h // B_KV_HEADS, dh).astype(F32)
    logits = jnp.einsum("btkgd,btnkd->btkgn", qg, ksel.astype(F32)) * dh ** -0.5
    logits = jnp.where(valid[:, :, None, None, :], logits, -jnp.inf)
    p = jax.nn.softmax(logits, axis=-1)
    o = jnp.einsum("btkgn,btnkd->btkgd", p, vsel.astype(F32))
    return o.reshape(bn, tq, h * dh).astype(q.dtype)


def fox_block(q, dq, qpos, k, v, dk_t, kpos):
    bn, tq, h, dh = q.shape
    logits = jnp.einsum("bthd,bshd->bhts", q.astype(F32), k.astype(F32)) * dh ** -0.5
    logits = logits + jnp.transpose(dq, (0, 2, 1))[..., None] - dk_t[:, :, None, :]
    causal = kpos[None, :] <= qpos[:, None]
    logits = jnp.where(causal[None, None], logits, -jnp.inf)
    p = jax.nn.softmax(logits, axis=-1)
    o = jnp.einsum("bhts,bshd->bthd", p, v.astype(F32))
    return o.reshape(bn, tq, h * dh).astype(q.dtype)


def token_mixing(xn, pos, past, lp):
    bn, t, _ = xn.shape
    (h_gate, h_a, h_bq, h_bk, h_bv, h_iq, h_ik, h_iw,
     h_cq, h_ck, h_cv, h_cf) = split_columns(xn @ lp["w_in"])
    o_a, v_a = chunk_spatial_gating(h_a, lp["a_ln_g"], lp["a_ln_b"], lp["a_ws"], lp["a_bs"])
    q_b = partial_rope(h_bq.reshape(bn, t, B_HEADS, HEAD_DIM), pos)
    k_b = partial_rope(h_bk.reshape(bn, t, B_KV_HEADS, HEAD_DIM), pos)
    v_b = h_bv.reshape(bn, t, B_KV_HEADS, HEAD_DIM)
    q_i = partial_rope(h_iq.reshape(bn, t, IDX_HEADS, IDX_DIM), pos)
    k_i = partial_rope(h_ik[:, :, None, :], pos)[:, :, 0, :]
    w_i = h_iw * IDX_HEADS ** -0.5
    q_c = h_cq.reshape(bn, t, C_HEADS, HEAD_DIM)
    k_c = h_ck.reshape(bn, t, C_HEADS, HEAD_DIM)
    v_c = h_cv.reshape(bn, t, C_HEADS, HEAD_DIM)
    logf = jax.nn.log_sigmoid(h_cf.astype(F32) + lp["b_forget"].astype(F32))
    new_rows = {"k_b": k_b, "v_b": v_b, "k_i": k_i, "k_c": k_c, "v_c": v_c, "logf": logf, "v_a": v_a}
    if past is None:
        kpos = pos
        kb, vb, ki, kc, vc, lf = k_b, v_b, k_i, k_c, v_c, logf
    else:
        past_len = past["k_b"].shape[1]
        kpos = jnp.concatenate([jnp.arange(past_len, dtype=pos.dtype), pos])
        cat = lambda a, b: jnp.concatenate([a.astype(b.dtype), b], axis=1)
        kb = cat(past["k_b"], k_b)
        vb = cat(past["v_b"], v_b)
        ki = cat(past["k_i"], k_i)
        kc = cat(past["k_c"], k_c)
        vc = cat(past["v_c"], v_c)
        lf = cat(past["logf"], logf)
    cum = jnp.cumsum(lf.astype(F32), axis=1)
    dq = cum[:, -t:]
    dk_t = jnp.transpose(cum, (0, 2, 1))
    topk = min(TOPK_MAX, kpos.shape[0] // 4)
    o_b = sweep_query_blocks(lambda q, qi, wi, qp: dsa_block(q, qi, wi, qp, kb, vb, ki, kpos, topk),
                             (q_b, q_i, w_i), pos)
    o_c = sweep_query_blocks(lambda q, d, qp: fox_block(q, d, qp, kc, vc, dk_t, kpos), (q_c, dq), pos)
    gates = jax.nn.sigmoid(h_gate.astype(F32)).reshape(bn, t, N_BRANCH, D_MODEL)
    branches = jnp.stack([o_a, o_b, o_c], axis=2)
    proj = jnp.einsum("btnw,nwd->btnd", branches, lp["w_branch"]).astype(F32)
    merged = jnp.sum(proj * gates, axis=2).astype(xn.dtype)
    return merged @ lp["w_out"], new_rows


def decoder_layer(x, pos, past, lp):
    x = x + 0.5 * swiglu(rms_norm(x, lp["norm_ffn1"]), lp["ffn1_wi"], lp["ffn1_wo"])
    mix, new_rows = token_mixing(rms_norm(x, lp["norm_mix"]), pos, past, lp)
    x = x + mix
    x = x + 0.5 * swiglu(rms_norm(x, lp["norm_ffn2"]), lp["ffn2_wi"], lp["ffn2_wo"])
    return x, new_rows


def setup_inputs(seed: int = 0) -> dict:
    key = jax.random.key(seed)
    ks = jax.random.split(key, 32)
    n_pages = PAST_LEN // PAGE_SIZE
    n_used = DEC_BATCH * n_pages
    n_pool = n_used + max(1, n_used // 4)

    def nrm(k, shape, s=1.0):
        return s * jax.random.normal(k, shape, F32)

    def gain(k, shape):
        return 1.0 + 0.01 * jax.random.normal(k, shape, F32)

    page_table = jax.random.permutation(ks[0], n_pool)[:n_used].reshape(DEC_BATCH, n_pages).astype(jnp.int32)
    return {
        "x_prompt": nrm(ks[1], (BATCH, SEQ, D_MODEL)),
        "x_sample": nrm(ks[2], (DEC_BATCH, DEC_SEQ, D_MODEL)),
        "cache_b_k": nrm(ks[3], (DEPTH, n_pool, PAGE_SIZE, B_KV_HEADS, HEAD_DIM)),
        "cache_b_v": nrm(ks[4], (DEPTH, n_pool, PAGE_SIZE, B_KV_HEADS, HEAD_DIM)),
        "cache_b_kidx": nrm(ks[5], (DEPTH, n_pool, PAGE_SIZE, IDX_DIM)),
        "cache_c_k": nrm(ks[6], (DEPTH, n_pool, PAGE_SIZE, C_HEADS, HEAD_DIM)),
        "cache_c_v": nrm(ks[7], (DEPTH, n_pool, PAGE_SIZE, C_HEADS, HEAD_DIM)),
        "cache_c_logf": jax.nn.log_sigmoid(nrm(ks[8], (DEPTH, n_pool, PAGE_SIZE, C_HEADS), 0.5)),
        "page_table": page_table,
        "norm_ffn1": gain(ks[9], (DEPTH, D_MODEL)),
        "ffn1_wi": nrm(ks[10], (DEPTH, D_MODEL, 2 * D_FF), D_MODEL ** -0.5),
        "ffn1_wo": nrm(ks[11], (DEPTH, D_FF, D_MODEL), D_FF ** -0.5),
        "norm_mix": gain(ks[12], (DEPTH, D_MODEL)),
        "w_in": nrm(ks[13], (DEPTH, D_MODEL, N_IN), D_MODEL ** -0.5),
        "b_forget": nrm(ks[14], (DEPTH, C_HEADS), 0.1),
        "a_ln_g": gain(ks[15], (DEPTH, W_BRANCH)),
        "a_ln_b": nrm(ks[16], (DEPTH, W_BRANCH), 0.01),
        "a_ws": nrm(ks[17], (DEPTH, A_GROUPS, A_CHUNK, A_CHUNK), A_CHUNK ** -0.5),
        "a_bs": 1.0 + nrm(ks[18], (DEPTH, A_GROUPS, A_CHUNK), 0.01),
        "w_branch": nrm(ks[19], (DEPTH, N_BRANCH, W_BRANCH, D_MODEL), W_BRANCH ** -0.5),
        "w_out": nrm(ks[20], (DEPTH, D_MODEL, D_MODEL), D_MODEL ** -0.5),
        "norm_ffn2": gain(ks[21], (DEPTH, D_MODEL)),
        "ffn2_wi": nrm(ks[22], (DEPTH, D_MODEL, 2 * D_FF), D_MODEL ** -0.5),
        "ffn2_wo": nrm(ks[23], (DEPTH, D_FF, D_MODEL), D_FF ** -0.5),
        "norm_final": gain(ks[24], (D_MODEL,)),
    }


def reference(x_prompt, x_sample, cache_b_k, cache_b_v, cache_b_kidx, cache_c_k, cache_c_v, cache_c_logf,
              page_table, norm_ffn1, ffn1_wi, ffn1_wo, norm_mix, w_in, b_forget, a_ln_g, a_ln_b, a_ws, a_bs,
              w_branch, w_out, norm_ffn2, ffn2_wi, ffn2_wo, norm_final):
    past_len = page_table.shape[1] * cache_b_k.shape[2]
    pos_p = jnp.arange(x_prompt.shape[1], dtype=jnp.int32)
    pos_s = past_len + jnp.arange(x_sample.shape[1], dtype=jnp.int32)
    hp, hs = x_prompt, x_sample
    rows_p, rows_s = [], []
    for l in range(DEPTH):
        lp = {"norm_ffn1": norm_ffn1[l], "ffn1_wi": ffn1_wi[l], "ffn1_wo": ffn1_wo[l],
              "norm_mix": norm_mix[l], "w_in": w_in[l], "b_forget": b_forget[l],
              "a_ln_g": a_ln_g[l], "a_ln_b": a_ln_b[l], "a_ws": a_ws[l], "a_bs": a_bs[l],
              "w_branch": w_branch[l], "w_out": w_out[l],
              "norm_ffn2": norm_ffn2[l], "ffn2_wi": ffn2_wi[l], "ffn2_wo": ffn2_wo[l]}
        past = {"k_b": gather_pages(cache_b_k[l], page_table),
                "v_b": gather_pages(cache_b_v[l], page_table),
                "k_i": gather_pages(cache_b_kidx[l], page_table),
                "k_c": gather_pages(cache_c_k[l], page_table),
                "v_c": gather_pages(cache_c_v[l], page_table),
                "logf": gather_pages(cache_c_logf[l], page_table)}
        hp, rp = decoder_layer(hp, pos_p, None, lp)
        hs, rs = decoder_layer(hs, pos_s, past, lp)
        rows_p.append(rp)
        rows_s.append(rs)
    y_prompt = rms_norm(hp, norm_final)
    y_sample = rms_norm(hs, norm_final)
    p_b_k = jnp.stack([r["k_b"] for r in rows_p])
    p_b_v = jnp.stack([r["v_b"] for r in rows_p])
    p_b_kidx = jnp.stack([r["k_i"] for r in rows_p])
    p_c_k = jnp.stack([r["k_c"] for r in rows_p])
    p_c_v = jnp.stack([r["v_c"] for r in rows_p])
    p_c_logf = jnp.stack([r["logf"] for r in rows_p])
    s_b_k = jnp.stack([r["k_b"] for r in rows_s])
    s_b_v = jnp.stack([r["v_b"] for r in rows_s])
    s_b_kidx = jnp.stack([r["k_i"] for r in rows_s])
    s_c_k = jnp.stack([r["k_c"] for r in rows_s])
    s_c_v = jnp.stack([r["v_c"] for r in rows_s])
    s_c_logf = jnp.stack([r["logf"] for r in rows_s])
    s_a_v = jnp.stack([r["v_a"] for r in rows_s])
    return (y_prompt, y_sample, p_b_k, p_b_v, p_b_kidx, p_c_k, p_c_v, p_c_logf,
            s_b_k, s_b_v, s_b_kidx, s_c_k, s_c_v, s_c_logf, s_a_v)
```

```python
import functools

import numpy as np
import jax
import jax.numpy as jnp
from jax import lax
from jax.experimental import pallas as pl
from jax.experimental.pallas import tpu as pltpu

F32 = jnp.float32
BF16 = jnp.bfloat16
I32 = jnp.int32

LANES = 128
HEAD_DIM = 64
N_BRANCH = 3
A_GROUPS = 4
B_HEADS = 8
B_KV_HEADS = 2
IDX_HEADS = 4
C_HEADS = 8
TOPK_MAX = 256
ROT_FRAC = 4
ROPE_THETA = 500000.0
EPS = 1e-6
NEG = -1e30
INT_MIN = -2 ** 31
VMEM_LIMIT = 56 * 1024 * 1024

_NT = (((1,), (1,)), ((), ()))


def _cparams(*sem):
    return pltpu.CompilerParams(dimension_semantics=sem, vmem_limit_bytes=VMEM_LIMIT)


def _dot(a, b, precision=None):
    return jnp.dot(a, b, preferred_element_type=F32, precision=precision)


def _dot_nt(a, b, precision=None):
    return lax.dot_general(a, b, _NT, preferred_element_type=F32, precision=precision)


def _rms(x, g):
    return x * lax.rsqrt(jnp.mean(x * x, axis=-1, keepdims=True) + EPS) * g


def _ffn_body(x_ref, gn_ref, wig_ref, wiu_ref, wo_ref, gf_ref, o_ref, xn_ref, acc_ref, *, nf, final_norm):
    f = pl.program_id(1)

    @pl.when(f == 0)
    def _():
        xn_ref[...] = _rms(x_ref[...], gn_ref[...]).astype(BF16)
        acc_ref[...] = jnp.zeros_like(acc_ref)

    xn = xn_ref[...]
    g = _dot(xn, wig_ref[...])
    u = _dot(xn, wiu_ref[...])
    h = (jax.nn.silu(g) * u).astype(BF16)
    acc_ref[...] += _dot(h, wo_ref[...])

    @pl.when(f == nf - 1)
    def _():
        y = x_ref[...] + 0.5 * acc_ref[...]
        if final_norm:
            y = _rms(y, gf_ref[...])
        o_ref[...] = y


def _ffn(x, gn, wi16, wo16, gf, *, final_norm, tm, tf):
    r, d = x.shape
    dff = wo16.shape[0]
    nf = dff // tf
    return pl.pallas_call(
        functools.partial(_ffn_body, nf=nf, final_norm=final_norm),
        grid=(r // tm, nf),
        in_specs=[
            pl.BlockSpec((tm, d), lambda i, f: (i, 0)),
            pl.BlockSpec((1, d), lambda i, f: (0, 0)),
            pl.BlockSpec((d, tf), lambda i, f: (0, f)),
            pl.BlockSpec((d, tf), lambda i, f: (0, nf + f)),
            pl.BlockSpec((tf, d), lambda i, f: (f, 0)),
            pl.BlockSpec((1, d), lambda i, f: (0, 0)),
        ],
        out_specs=pl.BlockSpec((tm, d), lambda i, f: (i, 0)),
        out_shape=jax.ShapeDtypeStruct((r, d), F32),
        scratch_shapes=[pltpu.VMEM((tm, d), BF16), pltpu.VMEM((tm, d), F32)],
        compiler_params=_cparams("parallel", "arbitrary"),
        name="ffn",
    )(x, gn, wi16, wi16, wo16, gf)


C_A = 0
C_BQ = 1024
C_BKV = 2048
C_IQ = 2304
C_KI = 2560
C_MISC = 2688
C_CQ = 2816
C_CK = 3328
C_CV = 3840
C_END = 4352
MISC_LOGF = 0
MISC_WI = 8


def _mixin_body(x_ref, gn_ref, w_ref, lng_ref, lnb_ref, ws_ref, ab_ref, bf_ref, rc_ref, rs_ref,
                oa_ref, va_ref, qb_ref, kvb32_ref, kvb16_ref, qi_ref, ki32_ref, ki16_ref, misc_ref,
                qc_ref, kc32_ref, vc32_ref, kc16_ref, vc16_ref, *, tm):
    xn = _rms(x_ref[...], gn_ref[...]).astype(BF16)
    rc = rc_ref[...]
    rs = rs_ref[...]
    lane = lax.broadcasted_iota(I32, (tm, LANES), 1)
    first_half = (lane % HEAD_DIM) < (HEAD_DIM // ROT_FRAC // 2)

    def rope(v):
        sw = jnp.where(first_half, pltpu.roll(v, LANES - 8, 1), pltpu.roll(v, 8, 1))
        return v * rc + sw * rs

    def proj(a, b):
        return _dot(xn, w_ref[:, a:b])

    ga = jax.nn.gelu(proj(C_A, C_A + 1024))
    u = ga[:, :512]
    v = ga[:, 512:]
    mu = jnp.mean(v, axis=-1, keepdims=True)
    vc = v - mu
    var = jnp.mean(vc * vc, axis=-1, keepdims=True)
    vn = vc * lax.rsqrt(var + EPS) * lng_ref[...] + lnb_ref[...]
    va_ref[...] = vn
    vn16 = vn.astype(BF16)
    for c in range(tm // LANES):
        rows = slice(c * LANES, (c + 1) * LANES)
        for g in range(A_GROUPS):
            cols = slice(g * LANES, (g + 1) * LANES)
            mixed = _dot(ws_ref[g], vn16[rows, cols]) + ab_ref[:, cols]
            oa_ref[rows, cols] = (u[rows, cols] * mixed).astype(BF16)

    for s in range(B_HEADS):
        cols = slice(s * LANES, (s + 1) * LANES)
        qb_ref[:, cols] = (rope(proj(C_BQ + s * LANES, C_BQ + (s + 1) * LANES)) * 0.125).astype(BF16)
    hk = rope(proj(C_BKV, C_BKV + 128))
    hv = proj(C_BKV + 128, C_BKV + 256)
    kvb32_ref[:, :128] = hk
    kvb32_ref[:, 128:] = hv
    kvb16_ref[:, :128] = hk.astype(BF16)
    kvb16_ref[:, 128:] = hv.astype(BF16)
    for s in range(2):
        cols = slice(s * LANES, (s + 1) * LANES)
        qi_ref[:, cols] = (rope(proj(C_IQ + s * LANES, C_IQ + (s + 1) * LANES)) * 0.125).astype(BF16)
    hki = rope(proj(C_KI, C_KI + 128))
    ki32_ref[...] = hki
    ki16_ref[...] = hki.astype(BF16)
    hm = proj(C_MISC, C_MISC + 128)
    lf = jax.nn.log_sigmoid(hm + bf_ref[...])
    misc_ref[...] = jnp.where(lane < MISC_WI, lf, jnp.where(lane < MISC_WI + IDX_HEADS, hm * 0.5, 0.0))

    qc_ref[...] = (proj(C_CQ, C_CQ + 512) * 0.125).astype(BF16)
    hck = proj(C_CK, C_CK + 512)
    kc32_ref[...] = hck
    kc16_ref[...] = hck.astype(BF16)
    hcv = proj(C_CV, C_CV + 512)
    vc32_ref[...] = hcv
    vc16_ref[...] = hcv.astype(BF16)


def _mixin(x, gn, w2, lng, lnb, ws16, abias, bfg, rc, rs, *, tm):
    r, d = x.shape
    npos = rc.shape[0] // tm
    row = lambda i: (i, 0)
    const2 = lambda i: (0, 0)
    widths = [(512, BF16), (512, F32), (1024, BF16), (256, F32), (256, BF16), (256, BF16), (128, F32), (128, BF16),
              (128, F32), (512, BF16), (512, F32), (512, F32), (512, BF16), (512, BF16)]
    return pl.pallas_call(
        functools.partial(_mixin_body, tm=tm),
        grid=(r // tm,),
        in_specs=[
            pl.BlockSpec((tm, d), row),
            pl.BlockSpec((1, d), const2),
            pl.BlockSpec((d, C_END), const2),
            pl.BlockSpec((1, 512), const2),
            pl.BlockSpec((1, 512), const2),
            pl.BlockSpec((A_GROUPS, LANES, LANES), lambda i: (0, 0, 0)),
            pl.BlockSpec((LANES, 512), const2),
            pl.BlockSpec((1, LANES), const2),
            pl.BlockSpec((tm, LANES), lambda i: (i % npos, 0)),
            pl.BlockSpec((tm, LANES), lambda i: (i % npos, 0)),
        ],
        out_specs=[pl.BlockSpec((tm, w), row) for w, _ in widths],
        out_shape=[jax.ShapeDtypeStruct((r, w), dt) for w, dt in widths],
        compiler_params=_cparams("parallel"),
        name="mix_in",
    )(x, gn, w2, lng, lnb, ws16, abias, bfg, rc, rs)


def _cum_body(misc_ref, o_ref, *, t):
    r = lax.broadcasted_iota(I32, (LANES, LANES), 0)
    c = lax.broadcasted_iota(I32, (LANES, LANES), 1)
    tri = jnp.where(c <= r, 1.0, 0.0).astype(F32)
    carry = jnp.zeros((1, LANES), F32)
    for ch in range(t // LANES):
        xs = misc_ref[ch * LANES:(ch + 1) * LANES, :]
        cum = _dot(tri, xs, precision=lax.Precision.HIGHEST) + carry
        carry = cum[LANES - 1:LANES, :]
        o_ref[0, :, ch * LANES:(ch + 1) * LANES] = cum.T[0:C_HEADS, :]


def _cum_t(misc, nb, t):
    return pl.pallas_call(
        functools.partial(_cum_body, t=t),
        grid=(nb,),
        in_specs=[pl.BlockSpec((t, LANES), lambda b: (b, 0))],
        out_specs=pl.BlockSpec((1, C_HEADS, t), lambda b: (b, 0, 0)),
        out_shape=jax.ShapeDtypeStruct((nb, C_HEADS, t), F32),
        compiler_params=_cparams("parallel"),
        name="cum_logf",
    )(misc)


def _fox_body(q_ref, k_ref, v_ref, cum_ref, o_ref, *, tq):
    i = pl.program_id(1)
    lane = lax.broadcasted_iota(I32, (tq, LANES), 1)
    lo = lane < HEAD_DIM
    tri = lax.broadcasted_iota(I32, (tq, tq), 1) <= lax.broadcasted_iota(I32, (tq, tq), 0)
    t0 = pl.multiple_of(i * tq, tq)
    for p in range(C_HEADS // 2):
        cols = slice(p * LANES, (p + 1) * LANES)
        q2 = q_ref[:, cols]
        zero = jnp.zeros_like(q2)
        qs = (jnp.where(lo, q2, zero), jnp.where(lo, zero, q2))
        outs = []
        for e in range(2):
            h = 2 * p + e
            c0 = cum_ref[0, h:h + 1, pl.ds(t0, LANES)][:, 0:1]

            def step(j, carry, masked, qe=qs[e], h=h, c0=c0, cols=cols):
                m, l, acc = carry
                s0 = pl.multiple_of(j * tq, tq)
                s = _dot_nt(qe, k_ref[pl.ds(s0, tq), cols])
                s = s + (c0 - cum_ref[0, h:h + 1, pl.ds(s0, tq)])
                if masked:
                    s = jnp.where(tri, s, NEG)
                m_new = jnp.maximum(m, jnp.max(s, axis=-1, keepdims=True))
                alpha = jnp.exp(m - m_new)
                pr = jnp.exp(s - m_new)
                l = alpha * l + jnp.sum(pr, axis=-1, keepdims=True)
                acc = alpha * acc + _dot(pr.astype(BF16), v_ref[pl.ds(s0, tq), cols])
                return m_new, l, acc

            init = (jnp.full((tq, 1), NEG, F32), jnp.zeros((tq, 1), F32), jnp.zeros((tq, LANES), F32))
            carry = lax.fori_loop(0, i, functools.partial(step, masked=False), init)
            _, l, acc = step(i, carry, True)
            outs.append(acc / l)
        o_ref[:, cols] = jnp.where(lo, outs[0], outs[1]).astype(BF16)


def _fox_prompt(qc16, kc16, vc16, cum_t, *, nb, t, tq):
    nq = t // tq
    w = qc16.shape[1]
    return pl.pallas_call(
        functools.partial(_fox_body, tq=tq),
        grid=(nb, nq),
        in_specs=[
            pl.BlockSpec((tq, w), lambda b, i: (b * nq + i, 0)),
            pl.BlockSpec((t, w), lambda b, i: (b, 0)),
            pl.BlockSpec((t, w), lambda b, i: (b, 0)),
            pl.BlockSpec((1, C_HEADS, t), lambda b, i: (b, 0, 0)),
        ],
        out_specs=pl.BlockSpec((tq, w), lambda b, i: (b * nq + i, 0)),
        out_shape=jax.ShapeDtypeStruct(qc16.shape, BF16),
        compiler_params=_cparams("parallel", "arbitrary"),
        name="fox_prompt",
    )(qc16, kc16, vc16, cum_t)


def _float_key(score):
    b = pltpu.bitcast(score, I32)
    return jnp.where(score == 0.0, 0, jnp.where(b >= 0, b, b ^ jnp.int32(0x7FFFFFFF)))


def _topk_select(keys_ref, bias_ref, *, rows, nch, ch, k, idx_of, n_total_ch):
    kf = float(k)

    def count(pred):
        def body(c, part):
            kc = keys_ref[:, pl.ds(pl.multiple_of(c * ch, ch), ch)]
            hit = jnp.where(pred(kc, c), 1.0, 0.0)
            for b in range(ch // LANES):
                part = part + hit[:, b * LANES:(b + 1) * LANES]
            return part
        part = lax.fori_loop(0, nch, body, jnp.zeros((rows, LANES), F32))
        return jnp.sum(part, axis=-1, keepdims=True)

    thr = jnp.where(count(lambda kc, c: kc >= 0) >= kf, jnp.int32(0), jnp.int32(INT_MIN)) + jnp.zeros((rows, 1), I32)

    def vbit(b, thr):
        cand = thr + (jnp.int32(1) << (jnp.int32(30) - b))
        return jnp.where(count(lambda kc, c: kc >= cand) >= kf, cand, thr)

    thr = lax.fori_loop(0, 31, vbit, thr)
    need = kf - count(lambda kc, c: kc > thr)

    def ibit(b, j):
        cand = j + (jnp.int32(1) << (jnp.int32(13) - b))
        return jnp.where(count(lambda kc, c: (kc == thr) & (idx_of(c) < cand)) <= need, cand, j)

    jthr = lax.fori_loop(0, 14, ibit, jnp.zeros((rows, 1), I32))

    def emit(c, _):
        off = pl.multiple_of(c * ch, ch)
        kc = keys_ref[:, pl.ds(off, ch)]
        sel = ((kc > thr) | ((kc == thr) & (idx_of(c) < jthr))) & (kc > INT_MIN)
        bias_ref[:, pl.ds(off, ch)] = jnp.where(sel, 0.0, NEG).astype(bias_ref.dtype)
        return 0

    lax.fori_loop(0, nch, emit, 0)

    def fill(c, _):
        bias_ref[:, pl.ds(pl.multiple_of(c * ch, ch), ch)] = jnp.full((rows, ch), NEG, bias_ref.dtype)
        return 0

    lax.fori_loop(nch, n_total_ch, fill, 0)


TOPK_ROWS = 128
TOPK_CH = 512


def _idx_topk_body(qi_ref, misc_ref, ki_ref, bias_ref, keys_ref, *, t, k):
    i = pl.program_id(1)
    rows, ch = TOPK_ROWS, TOPK_CH
    q0 = i * rows
    nch = (q0 + rows + ch - 1) // ch
    lane = lax.broadcasted_iota(I32, (rows, LANES), 1)
    lo = lane < HEAD_DIM
    qhs = []
    for p in range(IDX_HEADS // 2):
        q2 = qi_ref[:, p * LANES:(p + 1) * LANES]
        zero = jnp.zeros_like(q2)
        qhs += [jnp.where(lo, q2, zero), jnp.where(lo, zero, q2)]
    ws = [misc_ref[:, MISC_WI + h:MISC_WI + h + 1] for h in range(IDX_HEADS)]
    qpos = q0 + lax.broadcasted_iota(I32, (rows, ch), 0)
    col = lax.broadcasted_iota(I32, (rows, ch), 1)

    def score_chunk(c, _):
        off = pl.multiple_of(c * ch, ch)
        kk = ki_ref[pl.ds(off, ch), :]
        score = jnp.zeros((rows, ch), F32)
        for h in range(IDX_HEADS):
            score = score + ws[h] * jnp.maximum(_dot_nt(qhs[h], kk), 0.0)
        keys_ref[:, pl.ds(off, ch)] = jnp.where(off + col <= qpos, _float_key(score), INT_MIN)
        return 0

    lax.fori_loop(0, nch, score_chunk, 0)
    _topk_select(keys_ref, bias_ref, rows=rows, nch=nch, ch=ch, k=k,
                 idx_of=lambda c: c * ch + col, n_total_ch=t // ch)


def _idx_topk_prompt(qi16, misc, ki16, *, nb, t, k):
    nq = t // TOPK_ROWS
    return pl.pallas_call(
        functools.partial(_idx_topk_body, t=t, k=k),
        grid=(nb, nq),
        in_specs=[
            pl.BlockSpec((TOPK_ROWS, 256), lambda b, i: (b * nq + i, 0)),
            pl.BlockSpec((TOPK_ROWS, LANES), lambda b, i: (b * nq + i, 0)),
            pl.BlockSpec((t, LANES), lambda b, i: (b, 0)),
        ],
        out_specs=pl.BlockSpec((TOPK_ROWS, t), lambda b, i: (b * nq + i, 0)),
        out_shape=jax.ShapeDtypeStruct((nb * t, t), BF16),
        scratch_shapes=[pltpu.VMEM((TOPK_ROWS, t), I32)],
        compiler_params=_cparams("parallel", "arbitrary"),
        name="idx_topk_prompt",
    )(qi16, misc, ki16)


def _dsa_body(q_ref, kv_ref, bias_ref, o_ref, *, tq, ch):
    i = pl.program_id(1)
    nch = (i * tq + tq + ch - 1) // ch
    lane = lax.broadcasted_iota(I32, (tq, LANES), 1)
    lo = lane < HEAD_DIM
    res = []
    for h in range(B_HEADS):
        qh = q_ref[:, h * LANES:(h + 1) * LANES]

        def step(c, carry, qh=qh):
            m, l, acc = carry
            off = pl.multiple_of(c * ch, ch)
            s = _dot_nt(qh, kv_ref[pl.ds(off, ch), 0:LANES]) + bias_ref[:, pl.ds(off, ch)].astype(F32)
            m_new = jnp.maximum(m, jnp.max(s, axis=-1, keepdims=True))
            alpha = jnp.exp(m - m_new)
            pr = jnp.exp(s - m_new)
            l = alpha * l + jnp.sum(pr, axis=-1, keepdims=True)
            acc = alpha * acc + _dot(pr.astype(BF16), kv_ref[pl.ds(off, ch), LANES:2 * LANES])
            return m_new, l, acc

        init = (jnp.full((tq, 1), 2 * NEG, F32), jnp.zeros((tq, 1), F32), jnp.zeros((tq, LANES), F32))
        _, l, acc = lax.fori_loop(0, nch, step, init)
        res.append(acc / l)
    for p in range(B_HEADS // 2):
        g = (2 * p) // (B_HEADS // B_KV_HEADS)
        a, b = res[2 * p], res[2 * p + 1]
        if g == 0:
            b = pltpu.roll(b, HEAD_DIM, 1)
        else:
            a = pltpu.roll(a, HEAD_DIM, 1)
        o_ref[:, p * LANES:(p + 1) * LANES] = jnp.where(lo, a, b).astype(BF16)


def _dsa_prompt(qb16, kvb16, bias, *, nb, t, tq, ch):
    nq = t // tq
    return pl.pallas_call(
        functools.partial(_dsa_body, tq=tq, ch=ch),
        grid=(nb, nq),
        in_specs=[
            pl.BlockSpec((tq, B_HEADS * LANES), lambda b, i: (b * nq + i, 0)),
            pl.BlockSpec((t, 2 * LANES), lambda b, i: (b, 0)),
            pl.BlockSpec((tq, t), lambda b, i: (b * nq + i, 0)),
        ],
        out_specs=pl.BlockSpec((tq, 512), lambda b, i: (b * nq + i, 0)),
        out_shape=jax.ShapeDtypeStruct((nb * t, 512), BF16),
        compiler_params=_cparams("parallel", "arbitrary"),
        name="dsa_prompt",
    )(qb16, kvb16, bias)


def _page_specs(n_pages, layer, block):
    def spec(j):
        return pl.BlockSpec((None, None) + block, lambda b, pt: (layer, pt[b, j]) + (0,) * len(block))
    return [spec(j) for j in range(n_pages)]


def _s_score_body(pt_ref, q_ref, w_ref, *refs, n_pages):
    pages, new_ref, o_ref = refs[:n_pages], refs[n_pages], refs[n_pages + 1]
    q = q_ref[...]
    w = w_ref[...]
    for j in range(n_pages + 1):
        kp = pages[j][...].astype(BF16) if j < n_pages else new_ref[...]
        s = jnp.maximum(_dot_nt(q, kp), 0.0)
        score = jnp.zeros((8, LANES), F32)
        for h in range(IDX_HEADS):
            score = score + w[:, h:h + 1] * s[h * 8:(h + 1) * 8, :]
        o_ref[:, j * LANES:(j + 1) * LANES] = score[0:4, :]


def _s_score(pt, qi_s, wi_s, cache_kidx, ki_new, *, layer, ns, n_pages):
    lw = (n_pages + 1) * LANES
    return pl.pallas_call(
        functools.partial(_s_score_body, n_pages=n_pages),
        grid_spec=pltpu.PrefetchScalarGridSpec(
            num_scalar_prefetch=1, grid=(ns,),
            in_specs=[pl.BlockSpec((None, 32, HEAD_DIM), lambda b, pt: (b, 0, 0)),
                      pl.BlockSpec((None, 8, IDX_HEADS), lambda b, pt: (b, 0, 0))]
            + _page_specs(n_pages, layer, (LANES, HEAD_DIM))
            + [pl.BlockSpec((None, LANES, HEAD_DIM), lambda b, pt: (b, 0, 0))],
            out_specs=pl.BlockSpec((None, 4, lw), lambda b, pt: (b, 0, 0))),
        out_shape=jax.ShapeDtypeStruct((ns, 4, lw), F32),
        compiler_params=_cparams("arbitrary"),
        name="sample_idx_score",
    )(pt, qi_s, wi_s, *([cache_kidx] * n_pages), ki_new)


def _s_topk_body(s_ref, bias_ref, keys_ref, *, past, nt, k, lw):
    rows = TOPK_ROWS
    col = lax.broadcasted_iota(I32, (rows, LANES), 1)
    tq = lax.broadcasted_iota(I32, (rows, LANES), 0) % nt
    nch = lw // LANES
    for c in range(nch):
        cols = slice(c * LANES, (c + 1) * LANES)
        key = _float_key(s_ref[:, cols])
        if (c + 1) * LANES > past:
            key = jnp.where(c * LANES + col - past <= tq, key, INT_MIN)
        keys_ref[:, cols] = key
    _topk_select(keys_ref, bias_ref, rows=rows, nch=nch, ch=LANES, k=k,
                 idx_of=lambda c: c * LANES + col, n_total_ch=nch)


def _s_topk(scores, *, past, nt, k):
    r, lw = scores.shape
    return pl.pallas_call(
        functools.partial(_s_topk_body, past=past, nt=nt, k=k, lw=lw),
        grid=(r // TOPK_ROWS,),
        in_specs=[pl.BlockSpec((TOPK_ROWS, lw), lambda i: (i, 0))],
        out_specs=pl.BlockSpec((TOPK_ROWS, lw), lambda i: (i, 0)),
        out_shape=jax.ShapeDtypeStruct((r, lw), F32),
        scratch_shapes=[pltpu.VMEM((TOPK_ROWS, lw), I32)],
        compiler_params=_cparams("parallel"),
        name="sample_topk",
    )(scores)


def _s_dsa_body(pt_ref, q_ref, bias_ref, *refs, n_pages, nt):
    kps, knew = refs[:n_pages], refs[n_pages]
    vps, vnew = refs[n_pages + 1:2 * n_pages + 1], refs[2 * n_pages + 1]
    o_ref = refs[2 * n_pages + 2]
    q = q_ref[...]
    bias = bias_ref[...]
    logits = []
    for j in range(n_pages + 1):
        kp = kps[j][...].astype(BF16) if j < n_pages else knew[:, 0:LANES]
        bj = jnp.concatenate([jnp.broadcast_to(bias[t:t + 1, j * LANES:(j + 1) * LANES], (B_HEADS, LANES))
                              for t in range(nt)], axis=0)
        logits.append(_dot_nt(q, kp) + bj)
    m = functools.reduce(jnp.maximum, [jnp.max(s, axis=-1, keepdims=True) for s in logits])
    l = jnp.zeros_like(m)
    acc = jnp.zeros((nt * B_HEADS, LANES), F32)
    for j in range(n_pages + 1):
        pr = jnp.exp(logits[j] - m)
        l = l + jnp.sum(pr, axis=-1, keepdims=True)
        vp = vps[j][...].astype(BF16) if j < n_pages else vnew[:, LANES:2 * LANES]
        acc = acc + _dot(pr.astype(BF16), vp)
    o_ref[...] = acc / l


def _s_dsa(pt, qb_s, bias_s, cache_bk, cache_bv, kv_new, *, layer, ns, n_pages, nt):
    lw = (n_pages + 1) * LANES
    return pl.pallas_call(
        functools.partial(_s_dsa_body, n_pages=n_pages, nt=nt),
        grid_spec=pltpu.PrefetchScalarGridSpec(
            num_scalar_prefetch=1, grid=(ns,),
            in_specs=[pl.BlockSpec((None, nt * B_HEADS, LANES), lambda b, pt: (b, 0, 0)),
                      pl.BlockSpec((None, nt, lw), lambda b, pt: (b, 0, 0))]
            + _page_specs(n_pages, layer, (LANES, LANES))
            + [pl.BlockSpec((None, LANES, 2 * LANES), lambda b, pt: (b, 0, 0))]
            + _page_specs(n_pages, layer, (LANES, LANES))
            + [pl.BlockSpec((None, LANES, 2 * LANES), lambda b, pt: (b, 0, 0))],
            out_specs=pl.BlockSpec((None, nt * B_HEADS, LANES), lambda b, pt: (b, 0, 0))),
        out_shape=jax.ShapeDtypeStruct((ns, nt * B_HEADS, LANES), F32),
        compiler_params=_cparams("arbitrary"),
        name="sample_dsa",
    )(pt, qb_s, bias_s, *([cache_bk] * n_pages), kv_new, *([cache_bv] * n_pages), kv_new)


def _s_fox_body(pt_ref, q_ref, *refs, n_pages, nt):
    kps, knew = refs[:n_pages], refs[n_pages]
    vps, vnew = refs[n_pages + 1:2 * n_pages + 1], refs[2 * n_pages + 1]
    fps, fnew = refs[2 * n_pages + 2:3 * n_pages + 2], refs[3 * n_pages + 2]
    o_ref = refs[3 * n_pages + 3]
    nr = nt * C_HEADS
    q = q_ref[...]
    r = lax.broadcasted_iota(I32, (LANES, LANES), 0)
    c = lax.broadcasted_iota(I32, (LANES, LANES), 1)
    tri = jnp.where(c <= r, 1.0, 0.0).astype(F32)
    pick = jnp.where(lax.broadcasted_iota(I32, (nr, C_HEADS), 0) % C_HEADS
                     == lax.broadcasted_iota(I32, (nr, C_HEADS), 1), 1.0, 0.0).astype(F32)
    tq = lax.broadcasted_iota(I32, (nr, LANES), 0) // C_HEADS
    col = lax.broadcasted_iota(I32, (nr, LANES), 1)
    carry = jnp.zeros((1, C_HEADS), F32)
    logits = []
    for j in range(n_pages + 1):
        last = j == n_pages
        kp = knew[...] if last else kps[j][...].astype(BF16)
        lf = fnew[...] if last else fps[j][...]
        cum = _dot(tri, lf, precision=lax.Precision.HIGHEST) + carry
        carry = cum[LANES - 1:LANES, :]
        s = _dot_nt(q, kp) - _dot_nt(pick, cum, precision=lax.Precision.HIGHEST)
        if last:
            s = jnp.where(col <= tq, s, NEG)
        logits.append(s)
    m = functools.reduce(jnp.maximum, [jnp.max(s, axis=-1, keepdims=True) for s in logits])
    l = jnp.zeros_like(m)
    acc = jnp.zeros((nr, C_HEADS * HEAD_DIM), F32)
    for j in range(n_pages + 1):
        pr = jnp.exp(logits[j] - m)
        l = l + jnp.sum(pr, axis=-1, keepdims=True)
        vp = vnew[...] if j == n_pages else vps[j][...].astype(BF16)
        acc = acc + _dot(pr.astype(BF16), vp)
    o_ref[...] = acc / l


def _s_fox(pt, qc_s, cache_ck, cache_cv, cache_lf, k_new, v_new, lf_new, *, layer, ns, n_pages, nt):
    w = C_HEADS * HEAD_DIM
    seq = lambda b, pt: (b, 0, 0)
    return pl.pallas_call(
        functools.partial(_s_fox_body, n_pages=n_pages, nt=nt),
        grid_spec=pltpu.PrefetchScalarGridSpec(
            num_scalar_prefetch=1, grid=(ns,),
            in_specs=[pl.BlockSpec((None, nt * C_HEADS, w), seq)]
            + _page_specs(n_pages, layer, (LANES, w)) + [pl.BlockSpec((None, LANES, w), seq)]
            + _page_specs(n_pages, layer, (LANES, w)) + [pl.BlockSpec((None, LANES, w), seq)]
            + _page_specs(n_pages, layer, (LANES, C_HEADS)) + [pl.BlockSpec((None, LANES, C_HEADS), seq)],
            out_specs=pl.BlockSpec((None, nt * C_HEADS, w), seq)),
        out_shape=jax.ShapeDtypeStruct((ns, nt * C_HEADS, w), F32),
        compiler_params=_cparams("arbitrary"),
        name="sample_fox",
    )(pt, qc_s, *([cache_ck] * n_pages), k_new, *([cache_cv] * n_pages), v_new, *([cache_lf] * n_pages), lf_new)


def _mixout_body(x_ref, gn_ref, wg_ref, wb_ref, wo_ref, oa_ref, ob_ref, oc_ref, o_ref):
    x = x_ref[...]
    d = x.shape[1]
    xn = _rms(x, gn_ref[...]).astype(BF16)
    merged = jnp.zeros(x.shape, F32)
    for n, br in enumerate((oa_ref, ob_ref, oc_ref)):
        gate = jax.nn.sigmoid(_dot(xn, wg_ref[:, n * d:(n + 1) * d]))
        merged = merged + _dot(br[...], wb_ref[n]) * gate
    o_ref[...] = x + _dot(merged.astype(BF16), wo_ref[...])


def _mixout(x, gn, wg16, wb16, wo16, oa, ob, oc, *, tm):
    r, d = x.shape
    wbr = oa.shape[1]
    row = lambda i: (i, 0)
    const2 = lambda i: (0, 0)
    return pl.pallas_call(
        _mixout_body,
        grid=(r // tm,),
        in_specs=[
            pl.BlockSpec((tm, d), row),
            pl.BlockSpec((1, d), const2),
            pl.BlockSpec((d, N_BRANCH * d), const2),
            pl.BlockSpec((N_BRANCH, wbr, d), lambda i: (0, 0, 0)),
            pl.BlockSpec((d, d), const2),
            pl.BlockSpec((tm, wbr), row),
            pl.BlockSpec((tm, wbr), row),
            pl.BlockSpec((tm, wbr), row),
        ],
        out_specs=pl.BlockSpec((tm, d), row),
        out_shape=jax.ShapeDtypeStruct((r, d), F32),
        compiler_params=_cparams("parallel"),
        name="mix_out",
    )(x, gn, wg16, wb16, wo16, oa, ob, oc)


def _rope_tables(pos):
    rot = HEAD_DIM // ROT_FRAC
    half = rot // 2
    inv = ROPE_THETA ** (-jnp.arange(half, dtype=F32) / half)
    ang = pos.astype(F32)[:, None] * inv[None, :]
    cos, sin = jnp.cos(ang), jnp.sin(ang)
    ones = jnp.ones((pos.shape[0], HEAD_DIM - rot), F32)
    c64 = jnp.concatenate([cos, cos, ones], axis=1)
    s64 = jnp.concatenate([-sin, sin, 0.0 * ones], axis=1)
    return jnp.tile(c64, (1, LANES // HEAD_DIM)), jnp.tile(s64, (1, LANES // HEAD_DIM))


def _layer_weights(l, norm_ffn1, ffn1_wi, ffn1_wo, norm_mix, w_in, b_forget, a_ln_g, a_ln_b, a_ws, a_bs,
                   w_branch, w_out, norm_ffn2, ffn2_wi, ffn2_wo, nt):
    d = w_in.shape[1]
    w = w_in[l]
    widths = (N_BRANCH * d, d, 512, 128, 128, 256, 64, 4, 512, 512, 512, 8)
    cuts = np.concatenate([[0], np.cumsum(widths)])
    (w_gate, w_a, w_bq, w_bk, w_bv, w_iq, w_ik, w_iw, w_cq, w_ck, w_cv, w_cf) = [
        w[:, int(cuts[i]):int(cuts[i + 1])] for i in range(len(widths))]
    slots = []
    for h in range(B_HEADS):
        g = h // (B_HEADS // B_KV_HEADS)
        wh = w_bq[:, h * HEAD_DIM:(h + 1) * HEAD_DIM]
        slots.append(jnp.pad(wh, ((0, 0), (g * HEAD_DIM, (B_KV_HEADS - 1 - g) * HEAD_DIM))))
    w_misc = jnp.pad(jnp.concatenate([w_cf, w_iw], axis=1), ((0, 0), (0, LANES - 12)))
    w2 = jnp.concatenate([w_a] + slots + [w_bk, w_bv, w_iq, w_ik, w_ik, w_misc, w_cq, w_ck, w_cv], axis=1)
    ws = a_ws[l]
    tril = jnp.tril(jnp.ones((LANES, LANES), bool))
    ws_p = jnp.where(tril, ws, 0.0)
    corner = jnp.where(tril[:nt, :nt], ws[:, :nt, :nt], 0.0)
    ws_s = jnp.einsum("ij,gts->gitjs", jnp.eye(LANES // nt, dtype=F32), corner).reshape(A_GROUPS, LANES, LANES)
    bs = a_bs[l]
    ab_p = jnp.repeat(bs.T, LANES, axis=1)
    ab_s = jnp.repeat(jnp.tile(bs[:, :nt].T, (LANES // nt, 1)), LANES, axis=1)
    return dict(
        n1=norm_ffn1[l][None], wi1=ffn1_wi[l].astype(BF16), wo1=ffn1_wo[l].astype(BF16),
        n2=norm_ffn2[l][None], wi2=ffn2_wi[l].astype(BF16), wo2=ffn2_wo[l].astype(BF16),
        nm=norm_mix[l][None], w2=w2.astype(BF16), wg=w_gate.astype(BF16),
        wb=w_branch[l].astype(BF16), wo=w_out[l].astype(BF16),
        lng=a_ln_g[l][None], lnb=a_ln_b[l][None],
        ws_p=ws_p.astype(BF16), ws_s=ws_s.astype(BF16), ab_p=ab_p, ab_s=ab_s,
        bf=jnp.pad(b_forget[l], (0, LANES - C_HEADS))[None],
    )


def _pad_rows(a, n):
    return jnp.pad(a, ((0, 0), (0, n - a.shape[1]), (0, 0)))


def kernel(x_prompt, x_sample, cache_b_k, cache_b_v, cache_b_kidx, cache_c_k, cache_c_v, cache_c_logf, page_table,
           norm_ffn1, ffn1_wi, ffn1_wo, norm_mix, w_in, b_forget, a_ln_g, a_ln_b, a_ws, a_bs, w_branch, w_out,
           norm_ffn2, ffn2_wi, ffn2_wo, norm_final):
    nb, t, d = x_prompt.shape
    ns, nt, _ = x_sample.shape
    depth = w_in.shape[0]
    n_pages = page_table.shape[1]
    page = cache_b_k.shape[2]
    past = n_pages * page
    n_pool = cache_b_k.shape[1]
    assert page == LANES and d == 1024 and t % 512 == 0 and (ns * nt) % TOPK_ROWS == 0 and LANES % nt == 0
    k_prompt = min(TOPK_MAX, t // 4)
    k_sample = min(TOPK_MAX, (past + nt) // 4)
    rs_rows = ns * nt
    tm_s = min(256, rs_rows)

    pos_p = jnp.arange(t, dtype=I32)
    pos_s = past + (jnp.arange(rs_rows, dtype=I32) % nt)
    rc_p, rs_p = _rope_tables(pos_p)
    rc_s, rs_s = _rope_tables(pos_s)
    gfin = norm_final[None]

    cbk = cache_b_k.reshape(depth, n_pool, page, B_KV_HEADS * HEAD_DIM)
    cbv = cache_b_v.reshape(depth, n_pool, page, B_KV_HEADS * HEAD_DIM)
    cck = cache_c_k.reshape(depth, n_pool, page, C_HEADS * HEAD_DIM)
    ccv = cache_c_v.reshape(depth, n_pool, page, C_HEADS * HEAD_DIM)
    head_mask = (jnp.arange(C_HEADS * HEAD_DIM)[None, :] // HEAD_DIM == jnp.arange(C_HEADS)[:, None])

    hp = x_prompt.reshape(nb * t, d)
    hs = x_sample.reshape(rs_rows, d)
    rows_p, rows_s = [], []
    for l in range(depth):
        lw = _layer_weights(l, norm_ffn1, ffn1_wi, ffn1_wo, norm_mix, w_in, b_forget, a_ln_g, a_ln_b, a_ws, a_bs,
                            w_branch, w_out, norm_ffn2, ffn2_wi, ffn2_wo, nt)
        last = l == depth - 1
        hp = _ffn(hp, lw["n1"], lw["wi1"], lw["wo1"], gfin, final_norm=False, tm=512, tf=1408)
        (oa, _, qb, kvb32, kvb16, qi, ki32, ki16, misc, qc, kc32, vc32, kc16, vc16) = _mixin(
            hp, lw["nm"], lw["w2"], lw["lng"], lw["lnb"], lw["ws_p"], lw["ab_p"], lw["bf"], rc_p, rs_p, tm=256)
        cum_t = _cum_t(misc, nb, t)
        oc = _fox_prompt(qc, kc16, vc16, cum_t, nb=nb, t=t, tq=256)
        bias = _idx_topk_prompt(qi, misc, ki16, nb=nb, t=t, k=k_prompt)
        ob = _dsa_prompt(qb, kvb16, bias, nb=nb, t=t, tq=TOPK_ROWS, ch=TOPK_CH)
        hp = _mixout(hp, lw["nm"], lw["wg"], lw["wb"], lw["wo"], oa, ob, oc, tm=256)
        hp = _ffn(hp, lw["n2"], lw["wi2"], lw["wo2"], gfin, final_norm=last, tm=512, tf=1408)
        rows_p.append(dict(k_b=kvb32[:, :128], v_b=kvb32[:, 128:], k_i=ki32[:, :HEAD_DIM], k_c=kc32, v_c=vc32,
                           logf=misc[:, MISC_LOGF:MISC_LOGF + C_HEADS]))
        hs = _ffn(hs, lw["n1"], lw["wi1"], lw["wo1"], gfin, final_norm=False, tm=tm_s, tf=1408)
        (oa, va, qb, kvb32, kvb16, qi, ki32, ki16, misc, qc, kc32, vc32, kc16, vc16) = _mixin(
            hs, lw["nm"], lw["w2"], lw["lng"], lw["lnb"], lw["ws_s"], lw["ab_s"], lw["bf"], rc_s, rs_s, tm=tm_s)
        qi_s = _pad_rows(qi.reshape(ns, nt, IDX_HEADS, HEAD_DIM).transpose(0, 2, 1, 3).reshape(ns * IDX_HEADS, nt, HEAD_DIM),
                         8).reshape(ns, IDX_HEADS * 8, HEAD_DIM)
        wi_s = _pad_rows(misc[:, MISC_WI:MISC_WI + IDX_HEADS].reshape(ns, nt, IDX_HEADS), 8)
        ki_new = _pad_rows(ki16[:, :HEAD_DIM].reshape(ns, nt, HEAD_DIM), page)
        scores = _s_score(page_table, qi_s, wi_s, cache_b_kidx, ki_new, layer=l, ns=ns, n_pages=n_pages)
        bias_s = _s_topk(scores.reshape(rs_rows, -1), past=past, nt=nt, k=k_sample).reshape(ns, nt, -1)
        kv_new = _pad_rows(kvb16.reshape(ns, nt, 2 * LANES), page)
        ob_raw = _s_dsa(page_table, qb.reshape(ns, nt * B_HEADS, LANES), bias_s, cbk, cbv, kv_new,
                        layer=l, ns=ns, n_pages=n_pages, nt=nt)
        ob_raw = ob_raw.reshape(ns, nt, B_KV_HEADS, B_HEADS // B_KV_HEADS, B_KV_HEADS, HEAD_DIM)
        ob = jnp.stack([ob_raw[:, :, g, :, g, :] for g in range(B_KV_HEADS)], axis=2).reshape(rs_rows, 512).astype(BF16)
        qc_s = jnp.where(head_mask[None, None], qc.reshape(ns, nt, 1, 512), jnp.zeros((), BF16)).reshape(ns, nt * C_HEADS, 512)
        kc_new = _pad_rows(kc16.reshape(ns, nt, 512), page)
        vc_new = _pad_rows(vc16.reshape(ns, nt, 512), page)
        lf_new = _pad_rows(misc[:, MISC_LOGF:MISC_LOGF + C_HEADS].reshape(ns, nt, C_HEADS), page)
        oc_raw = _s_fox(page_table, qc_s, cck, ccv, cache_c_logf, kc_new, vc_new, lf_new,
                        layer=l, ns=ns, n_pages=n_pages, nt=nt)
        oc_raw = oc_raw.reshape(ns, nt, C_HEADS, C_HEADS, HEAD_DIM)
        oc = jnp.stack([oc_raw[:, :, h, h, :] for h in range(C_HEADS)], axis=2).reshape(rs_rows, 512).astype(BF16)
        hs = _mixout(hs, lw["nm"], lw["wg"], lw["wb"], lw["wo"], oa, ob, oc, tm=tm_s)
        hs = _ffn(hs, lw["n2"], lw["wi2"], lw["wo2"], gfin, final_norm=last, tm=tm_s, tf=1408)
        rows_s.append(dict(k_b=kvb32[:, :128], v_b=kvb32[:, 128:], k_i=ki32[:, :HEAD_DIM], k_c=kc32, v_c=vc32,
                           logf=misc[:, MISC_LOGF:MISC_LOGF + C_HEADS], v_a=va))

    def stack(rows, key, shape):
        return jnp.stack([r[key] for r in rows]).reshape((depth,) + shape)

    y_prompt = hp.reshape(nb, t, d)
    y_sample = hs.reshape(ns, nt, d)
    outs = [y_prompt, y_sample]
    for rows, lead in ((rows_p, (nb, t)), (rows_s, (ns, nt))):
        outs += [stack(rows, "k_b", lead + (B_KV_HEADS, HEAD_DIM)), stack(rows, "v_b", lead + (B_KV_HEADS, HEAD_DIM)),
                 stack(rows, "k_i", lead + (HEAD_DIM,)), stack(rows, "k_c", lead + (C_HEADS, HEAD_DIM)),
                 stack(rows, "v_c", lead + (C_HEADS, HEAD_DIM)), stack(rows, "logf", lead + (C_HEADS,))]
    outs.append(stack(rows_s, "v_a", (ns, nt, 512)))
    return tuple(outs)
```

```python
import functools

import numpy as np
import jax
import jax.numpy as jnp
from jax import lax
from jax.experimental import pallas as pl
from jax.experimental.pallas import tpu as pltpu

F32 = jnp.float32
BF16 = jnp.bfloat16
I32 = jnp.int32

LANES = 128
HEAD_DIM = 64
N_BRANCH = 3
A_GROUPS = 4
B_HEADS = 8
B_KV_HEADS = 2
IDX_HEADS = 4
C_HEADS = 8
TOPK_MAX = 256
ROT_FRAC = 4
ROPE_THETA = 500000.0
EPS = 1e-6
NEG = -1e30
INT_MIN = -2 ** 31
VMEM_LIMIT = 56 * 1024 * 1024

_NT = (((1,), (1,)), ((), ()))


def _cparams(*sem):
    return pltpu.CompilerParams(dimension_semantics=sem, vmem_limit_bytes=VMEM_LIMIT)


def _dot(a, b, precision=None):
    return jnp.dot(a, b, preferred_element_type=F32, precision=precision)


def _dot_nt(a, b, precision=None):
    return lax.dot_general(a, b, _NT, preferred_element_type=F32, precision=precision)


def _rms(x, g):
    return x * lax.rsqrt(jnp.mean(x * x, axis=-1, keepdims=True) + EPS) * g


def _ffn_body(x_ref, gn_ref, wig_ref, wiu_ref, wo_ref, gf_ref, o_ref, xn_ref, acc_ref, *, nf, final_norm):
    f = pl.program_id(1)

    @pl.when(f == 0)
    def _():
        xn_ref[...] = _rms(x_ref[...], gn_ref[...]).astype(BF16)
        acc_ref[...] = jnp.zeros_like(acc_ref)

    xn = xn_ref[...]
    g = _dot(xn, wig_ref[...])
    u = _dot(xn, wiu_ref[...])
    h = (jax.nn.silu(g) * u).astype(BF16)
    acc_ref[...] += _dot(h, wo_ref[...])

    @pl.when(f == nf - 1)
    def _():
        y = x_ref[...] + 0.5 * acc_ref[...]
        if final_norm:
            y = _rms(y, gf_ref[...])
        o_ref[...] = y


def _ffn(x, gn, wi16, wo16, gf, *, final_norm, tm, tf):
    r, d = x.shape
    dff = wo16.shape[0]
    nf = dff // tf
    return pl.pallas_call(
        functools.partial(_ffn_body, nf=nf, final_norm=final_norm),
        grid=(r // tm, nf),
        in_specs=[
            pl.BlockSpec((tm, d), lambda i, f: (i, 0)),
            pl.BlockSpec((1, d), lambda i, f: (0, 0)),
            pl.BlockSpec((d, tf), lambda i, f: (0, f)),
            pl.BlockSpec((d, tf), lambda i, f: (0, nf + f)),
            pl.BlockSpec((tf, d), lambda i, f: (f, 0)),
            pl.BlockSpec((1, d), lambda i, f: (0, 0)),
        ],
        out_specs=pl.BlockSpec((tm, d), lambda i, f: (i, 0)),
        out_shape=jax.ShapeDtypeStruct((r, d), F32),
        scratch_shapes=[pltpu.VMEM((tm, d), BF16), pltpu.VMEM((tm, d), F32)],
        compiler_params=_cparams("parallel", "arbitrary"),
        name="ffn",
    )(x, gn, wi16, wi16, wo16, gf)


C_A = 0
C_BQ = 1024
C_BKV = 2048
C_IQ = 2304
C_KI = 2560
C_MISC = 2688
C_CQ = 2816
C_CK = 3840
C_CV = 4352
C_END = 4864
MISC_LOGF = 0
MISC_WI = 8


def _mixin_body(x_ref, gn_ref, w_ref, lng_ref, lnb_ref, ws_ref, ab_ref, bf_ref, rc_ref, rs_ref,
                oa_ref, va_ref, qb_ref, kvb32_ref, kvb16_ref, qi_ref, ki32_ref, ki16_ref, misc_ref,
                qc_ref, kc32_ref, vc32_ref, kc16_ref, vc16_ref, *, tm):
    xn = _rms(x_ref[...], gn_ref[...]).astype(BF16)
    rc = rc_ref[...]
    rs = rs_ref[...]
    lane = lax.broadcasted_iota(I32, (tm, LANES), 1)
    first_half = (lane % HEAD_DIM) < (HEAD_DIM // ROT_FRAC // 2)

    def rope(v):
        sw = jnp.where(first_half, pltpu.roll(v, LANES - 8, 1), pltpu.roll(v, 8, 1))
        return v * rc + sw * rs

    def proj(a, b):
        return _dot(xn, w_ref[:, a:b])

    ga = jax.nn.gelu(proj(C_A, C_A + 1024))
    u = ga[:, :512]
    v = ga[:, 512:]
    mu = jnp.mean(v, axis=-1, keepdims=True)
    vc = v - mu
    var = jnp.mean(vc * vc, axis=-1, keepdims=True)
    vn = vc * lax.rsqrt(var + EPS) * lng_ref[...] + lnb_ref[...]
    va_ref[...] = vn
    vn16 = vn.astype(BF16)
    for c in range(tm // LANES):
        rows = slice(c * LANES, (c + 1) * LANES)
        for g in range(A_GROUPS):
            cols = slice(g * LANES, (g + 1) * LANES)
            mixed = _dot(ws_ref[g], vn16[rows, cols]) + ab_ref[:, cols]
            oa_ref[rows, cols] = (u[rows, cols] * mixed).astype(BF16)

    for s in range(B_HEADS):
        cols = slice(s * LANES, (s + 1) * LANES)
        qb_ref[:, cols] = (rope(proj(C_BQ + s * LANES, C_BQ + (s + 1) * LANES)) * 0.125).astype(BF16)
    hk = rope(proj(C_BKV, C_BKV + 128))
    hv = proj(C_BKV + 128, C_BKV + 256)
    kvb32_ref[:, :128] = hk
    kvb32_ref[:, 128:] = hv
    kvb16_ref[:, :128] = hk.astype(BF16)
    kvb16_ref[:, 128:] = hv.astype(BF16)
    for s in range(2):
        cols = slice(s * LANES, (s + 1) * LANES)
        qi_ref[:, cols] = (rope(proj(C_IQ + s * LANES, C_IQ + (s + 1) * LANES)) * 0.125).astype(BF16)
    hki = rope(proj(C_KI, C_KI + 128))
    ki32_ref[...] = hki
    ki16_ref[...] = hki.astype(BF16)
    hm = proj(C_MISC, C_MISC + 128)
    lf = jax.nn.log_sigmoid(hm + bf_ref[...])
    misc_ref[...] = jnp.where(lane < MISC_WI, lf, jnp.where(lane < MISC_WI + IDX_HEADS, hm * 0.5, 0.0))

    for s in range(C_HEADS):
        cols = slice(s * LANES, (s + 1) * LANES)
        qc_ref[:, cols] = (proj(C_CQ + s * LANES, C_CQ + (s + 1) * LANES) * 0.125).astype(BF16)
    hck = proj(C_CK, C_CK + 512)
    kc32_ref[...] = hck
    kc16_ref[...] = hck.astype(BF16)
    hcv = proj(C_CV, C_CV + 512)
    vc32_ref[...] = hcv
    vc16_ref[...] = hcv.astype(BF16)


def _mixin(x, gn, w2, lng, lnb, ws16, abias, bfg, rc, rs, *, tm):
    r, d = x.shape
    npos = rc.shape[0] // tm
    row = lambda i: (i, 0)
    const2 = lambda i: (0, 0)
    widths = [(512, BF16), (512, F32), (1024, BF16), (256, F32), (256, BF16), (256, BF16), (128, F32), (128, BF16),
              (128, F32), (1024, BF16), (512, F32), (512, F32), (512, BF16), (512, BF16)]
    return pl.pallas_call(
        functools.partial(_mixin_body, tm=tm),
        grid=(r // tm,),
        in_specs=[
            pl.BlockSpec((tm, d), row),
            pl.BlockSpec((1, d), const2),
            pl.BlockSpec((d, C_END), const2),
            pl.BlockSpec((1, 512), const2),
            pl.BlockSpec((1, 512), const2),
            pl.BlockSpec((A_GROUPS, LANES, LANES), lambda i: (0, 0, 0)),
            pl.BlockSpec((LANES, 512), const2),
            pl.BlockSpec((1, LANES), const2),
            pl.BlockSpec((tm, LANES), lambda i: (i % npos, 0)),
            pl.BlockSpec((tm, LANES), lambda i: (i % npos, 0)),
        ],
        out_specs=[pl.BlockSpec((tm, w), row) for w, _ in widths],
        out_shape=[jax.ShapeDtypeStruct((r, w), dt) for w, dt in widths],
        compiler_params=_cparams("parallel"),
        name="mix_in",
    )(x, gn, w2, lng, lnb, ws16, abias, bfg, rc, rs)


def _cum_body(misc_ref, o_ref, *, t):
    r = lax.broadcasted_iota(I32, (LANES, LANES), 0)
    c = lax.broadcasted_iota(I32, (LANES, LANES), 1)
    tri = jnp.where(c <= r, 1.0, 0.0).astype(F32)
    carry = jnp.zeros((1, LANES), F32)
    for ch in range(t // LANES):
        xs = misc_ref[ch * LANES:(ch + 1) * LANES, :]
        cum = _dot(tri, xs, precision=lax.Precision.HIGHEST) + carry
        carry = cum[LANES - 1:LANES, :]
        o_ref[0, :, ch * LANES:(ch + 1) * LANES] = cum.T[0:C_HEADS, :]


def _cum_t(misc, nb, t):
    return pl.pallas_call(
        functools.partial(_cum_body, t=t),
        grid=(nb,),
        in_specs=[pl.BlockSpec((t, LANES), lambda b: (b, 0))],
        out_specs=pl.BlockSpec((1, C_HEADS, t), lambda b: (b, 0, 0)),
        out_shape=jax.ShapeDtypeStruct((nb, C_HEADS, t), F32),
        compiler_params=_cparams("parallel"),
        name="cum_logf",
    )(misc)


def _fox_body(q_ref, k_ref, v_ref, cum_ref, o_ref, *, tq, tk):
    i = pl.program_id(1)
    lo = lax.broadcasted_iota(I32, (tq, LANES), 1) < HEAD_DIM
    t0 = pl.multiple_of(i * tq, tq)
    nfull = t0 // tk
    qpos = t0 + lax.broadcasted_iota(I32, (2 * tq, tk), 0) % tq
    col = lax.broadcasted_iota(I32, (2 * tq, tk), 1)
    c0 = [cum_ref[0, h:h + 1, pl.ds(t0, LANES)][:, 0:1] for h in range(C_HEADS)]

    def step(j, carry, masked):
        s0 = pl.multiple_of(j * tk, tk)
        out = []
        for p in range(C_HEADS // 2):
            cols = slice(p * LANES, (p + 1) * LANES)
            m, l, acc = carry[p]
            qp = jnp.concatenate([q_ref[:, 2 * p * LANES:(2 * p + 1) * LANES],
                                  q_ref[:, (2 * p + 1) * LANES:(2 * p + 2) * LANES]], axis=0)
            s = _dot_nt(qp, k_ref[pl.ds(s0, tk), cols])
            ba = c0[2 * p] - cum_ref[0, 2 * p:2 * p + 1, pl.ds(s0, tk)]
            bb = c0[2 * p + 1] - cum_ref[0, 2 * p + 1:2 * p + 2, pl.ds(s0, tk)]
            s = jnp.concatenate([s[:tq] + ba, s[tq:] + bb], axis=0)
            if masked:
                s = jnp.where(s0 + col <= qpos, s, NEG)
            m_new = jnp.maximum(m, jnp.max(s, axis=-1, keepdims=True))
            alpha = jnp.exp(m - m_new)
            pr = jnp.exp(s - m_new)
            l = alpha * l + jnp.sum(pr, axis=-1, keepdims=True)
            acc = alpha * acc + _dot(pr.astype(BF16), v_ref[pl.ds(s0, tk), cols])
            out.append((m_new, l, acc))
        return tuple(out)

    init = tuple((jnp.full((2 * tq, 1), NEG, F32), jnp.zeros((2 * tq, 1), F32), jnp.zeros((2 * tq, LANES), F32))
                 for _ in range(C_HEADS // 2))
    carry = lax.fori_loop(0, nfull, functools.partial(step, masked=False), init)
    carry = step(nfull, carry, True)
    for p in range(C_HEADS // 2):
        _, l, acc = carry[p]
        out = acc / l
        o_ref[:, p * LANES:(p + 1) * LANES] = jnp.where(lo, out[:tq], out[tq:]).astype(BF16)


def _fox_prompt(qc16, kc16, vc16, cum_t, *, nb, t, tq, tk):
    nq = t // tq
    w = kc16.shape[1]
    return pl.pallas_call(
        functools.partial(_fox_body, tq=tq, tk=tk),
        grid=(nb, nq),
        in_specs=[
            pl.BlockSpec((tq, C_HEADS * LANES), lambda b, i: (b * nq + i, 0)),
            pl.BlockSpec((t, w), lambda b, i: (b, 0)),
            pl.BlockSpec((t, w), lambda b, i: (b, 0)),
            pl.BlockSpec((1, C_HEADS, t), lambda b, i: (b, 0, 0)),
        ],
        out_specs=pl.BlockSpec((tq, w), lambda b, i: (b * nq + i, 0)),
        out_shape=jax.ShapeDtypeStruct(kc16.shape, BF16),
        compiler_params=_cparams("parallel", "arbitrary"),
        name="fox_prompt",
    )(qc16, kc16, vc16, cum_t)


def _float_key(score):
    b = pltpu.bitcast(score, I32)
    return jnp.where(score == 0.0, 0, jnp.where(b >= 0, b, b ^ jnp.int32(0x7FFFFFFF)))


def _topk_select(keys_ref, bias_ref, *, rows, nch, ch, k, idx_of, n_total_ch):
    kf = float(k)

    def count(pred):
        def body(c, part):
            kc = keys_ref[:, pl.ds(pl.multiple_of(c * ch, ch), ch)]
            hit = jnp.where(pred(kc, c), 1.0, 0.0)
            for b in range(ch // LANES):
                part = part + hit[:, b * LANES:(b + 1) * LANES]
            return part
        part = lax.fori_loop(0, nch, body, jnp.zeros((rows, LANES), F32))
        return jnp.sum(part, axis=-1, keepdims=True)

    thr = jnp.where(count(lambda kc, c: kc >= 0) >= kf, jnp.int32(0), jnp.int32(INT_MIN)) + jnp.zeros((rows, 1), I32)

    def vbit(b, thr):
        cand = thr + (jnp.int32(1) << (jnp.int32(30) - b))
        return jnp.where(count(lambda kc, c: kc >= cand) >= kf, cand, thr)

    thr = lax.fori_loop(0, 31, vbit, thr)
    need = kf - count(lambda kc, c: kc > thr)
    ties = count(lambda kc, c: kc == thr)

    def ibit(b, j):
        cand = j + (jnp.int32(1) << (jnp.int32(13) - b))
        return jnp.where(count(lambda kc, c: (kc == thr) & (idx_of(c) < cand)) <= need, cand, j)

    jthr = lax.cond(jnp.max(ties - need) > 0.0,
                    lambda: lax.fori_loop(0, 14, ibit, jnp.zeros((rows, 1), I32)),
                    lambda: jnp.full((rows, 1), 1 << 14, I32))

    def emit(c, _):
        off = pl.multiple_of(c * ch, ch)
        kc = keys_ref[:, pl.ds(off, ch)]
        sel = ((kc > thr) | ((kc == thr) & (idx_of(c) < jthr))) & (kc > INT_MIN)
        bias_ref[:, pl.ds(off, ch)] = jnp.where(sel, 0.0, NEG).astype(bias_ref.dtype)
        return 0

    lax.fori_loop(0, nch, emit, 0)

    def fill(c, _):
        bias_ref[:, pl.ds(pl.multiple_of(c * ch, ch), ch)] = jnp.full((rows, ch), NEG, bias_ref.dtype)
        return 0

    lax.fori_loop(nch, n_total_ch, fill, 0)


TOPK_ROWS = 128
TOPK_CH = 512
IDX_ROWS = 256


def _idx_topk_body(qi_ref, misc_ref, ki_ref, bias_ref, keys_ref, *, t, k):
    i = pl.program_id(1)
    rows, ch = IDX_ROWS, TOPK_CH
    q0 = i * rows
    nch = (q0 + rows + ch - 1) // ch
    lane = lax.broadcasted_iota(I32, (rows, LANES), 1)
    lo = lane < HEAD_DIM
    qhs = []
    for p in range(IDX_HEADS // 2):
        q2 = qi_ref[:, p * LANES:(p + 1) * LANES]
        zero = jnp.zeros_like(q2)
        qhs += [jnp.where(lo, q2, zero), jnp.where(lo, zero, q2)]
    ws = [misc_ref[:, MISC_WI + h:MISC_WI + h + 1] for h in range(IDX_HEADS)]
    qpos = q0 + lax.broadcasted_iota(I32, (rows, ch), 0)
    col = lax.broadcasted_iota(I32, (rows, ch), 1)

    def score_chunk(c, _):
        off = pl.multiple_of(c * ch, ch)
        kk = ki_ref[pl.ds(off, ch), :]
        score = jnp.zeros((rows, ch), F32)
        for h in range(IDX_HEADS):
            score = score + ws[h] * jnp.maximum(_dot_nt(qhs[h], kk), 0.0)
        keys_ref[:, pl.ds(off, ch)] = jnp.where(off + col <= qpos, _float_key(score), INT_MIN)
        return 0

    lax.fori_loop(0, nch, score_chunk, 0)
    _topk_select(keys_ref, bias_ref, rows=rows, nch=nch, ch=ch, k=k,
                 idx_of=lambda c: c * ch + col, n_total_ch=t // ch)


def _idx_topk_prompt(qi16, misc, ki16, *, nb, t, k):
    nq = t // IDX_ROWS
    return pl.pallas_call(
        functools.partial(_idx_topk_body, t=t, k=k),
        grid=(nb, nq),
        in_specs=[
            pl.BlockSpec((IDX_ROWS, 256), lambda b, i: (b * nq + i, 0)),
            pl.BlockSpec((IDX_ROWS, LANES), lambda b, i: (b * nq + i, 0)),
            pl.BlockSpec((t, LANES), lambda b, i: (b, 0)),
        ],
        out_specs=pl.BlockSpec((IDX_ROWS, t), lambda b, i: (b * nq + i, 0)),
        out_shape=jax.ShapeDtypeStruct((nb * t, t), BF16),
        scratch_shapes=[pltpu.VMEM((IDX_ROWS, t), I32)],
        compiler_params=_cparams("parallel", "arbitrary"),
        name="idx_topk_prompt",
    )(qi16, misc, ki16)


def _dsa_body(q_ref, kv_ref, bias_ref, o_ref, *, tq, ch):
    i = pl.program_id(1)
    nch = (i * tq + tq + ch - 1) // ch
    lane = lax.broadcasted_iota(I32, (tq, LANES), 1)
    lo = lane < HEAD_DIM

    q_all = jnp.concatenate([q_ref[:, h * LANES:(h + 1) * LANES] for h in range(B_HEADS)], axis=0)

    def step(c, carry):
        m, l, acc = carry
        off = pl.multiple_of(c * ch, ch)
        bias = bias_ref[:, pl.ds(off, ch)].astype(F32)
        s = _dot_nt(q_all, kv_ref[pl.ds(off, ch), 0:LANES]) + jnp.concatenate([bias] * B_HEADS, axis=0)
        m_new = jnp.maximum(m, jnp.max(s, axis=-1, keepdims=True))
        alpha = jnp.exp(m - m_new)
        pr = jnp.exp(s - m_new)
        l = alpha * l + jnp.sum(pr, axis=-1, keepdims=True)
        acc = alpha * acc + _dot(pr.astype(BF16), kv_ref[pl.ds(off, ch), LANES:2 * LANES])
        return m_new, l, acc

    rows = B_HEADS * tq
    init = (jnp.full((rows, 1), 2 * NEG, F32), jnp.zeros((rows, 1), F32), jnp.zeros((rows, LANES), F32))
    _, l, acc = lax.fori_loop(0, nch, step, init)
    out = acc / l
    res = [out[h * tq:(h + 1) * tq] for h in range(B_HEADS)]
    for p in range(B_HEADS // 2):
        g = (2 * p) // (B_HEADS // B_KV_HEADS)
        a, b = res[2 * p], res[2 * p + 1]
        if g == 0:
            b = pltpu.roll(b, HEAD_DIM, 1)
        else:
            a = pltpu.roll(a, HEAD_DIM, 1)
        o_ref[:, p * LANES:(p + 1) * LANES] = jnp.where(lo, a, b).astype(BF16)


def _dsa_prompt(qb16, kvb16, bias, *, nb, t, tq, ch):
    nq = t // tq
    return pl.pallas_call(
        functools.partial(_dsa_body, tq=tq, ch=ch),
        grid=(nb, nq),
        in_specs=[
            pl.BlockSpec((tq, B_HEADS * LANES), lambda b, i: (b * nq + i, 0)),
            pl.BlockSpec((t, 2 * LANES), lambda b, i: (b, 0)),
            pl.BlockSpec((tq, t), lambda b, i: (b * nq + i, 0)),
        ],
        out_specs=pl.BlockSpec((tq, 512), lambda b, i: (b * nq + i, 0)),
        out_shape=jax.ShapeDtypeStruct((nb * t, 512), BF16),
        compiler_params=_cparams("parallel", "arbitrary"),
        name="dsa_prompt",
    )(qb16, kvb16, bias)


def _page_specs(n_pages, layer, block):
    def spec(j):
        return pl.BlockSpec((None, None) + block, lambda b, pt: (layer, pt[b, j]) + (0,) * len(block))
    return [spec(j) for j in range(n_pages)]


def _s_score_body(pt_ref, q_ref, w_ref, *refs, n_pages):
    pages, new_ref, o_ref = refs[:n_pages], refs[n_pages], refs[n_pages + 1]
    q = q_ref[...]
    w = w_ref[...]
    for j in range(n_pages + 1):
        s = _dot(q, pages[j][...].astype(BF16)) if j < n_pages else _dot_nt(q, new_ref[...])
        s = jnp.maximum(s, 0.0)
        score = jnp.zeros((8, LANES), F32)
        for h in range(IDX_HEADS):
            score = score + w[:, h:h + 1] * s[h * 8:(h + 1) * 8, :]
        o_ref[:, j * LANES:(j + 1) * LANES] = score[0:4, :]


def _s_score(pt, qi_s, wi_s, cache_kidx, ki_new, *, layer, ns, n_pages):
    lw = (n_pages + 1) * LANES
    return pl.pallas_call(
        functools.partial(_s_score_body, n_pages=n_pages),
        grid_spec=pltpu.PrefetchScalarGridSpec(
            num_scalar_prefetch=1, grid=(ns,),
            in_specs=[pl.BlockSpec((None, 32, HEAD_DIM), lambda b, pt: (b, 0, 0)),
                      pl.BlockSpec((None, 8, IDX_HEADS), lambda b, pt: (b, 0, 0))]
            + _page_specs(n_pages, layer, (HEAD_DIM, LANES))
            + [pl.BlockSpec((None, LANES, HEAD_DIM), lambda b, pt: (b, 0, 0))],
            out_specs=pl.BlockSpec((None, 4, lw), lambda b, pt: (b, 0, 0))),
        out_shape=jax.ShapeDtypeStruct((ns, 4, lw), F32),
        compiler_params=_cparams("arbitrary"),
        name="sample_idx_score",
    )(pt, qi_s, wi_s, *([cache_kidx] * n_pages), ki_new)


def _s_topk_body(s_ref, bias_ref, keys_ref, *, past, nt, k, lw):
    rows = TOPK_ROWS
    col = lax.broadcasted_iota(I32, (rows, LANES), 1)
    tq = lax.broadcasted_iota(I32, (rows, LANES), 0) % nt
    nch = lw // LANES
    for c in range(nch):
        cols = slice(c * LANES, (c + 1) * LANES)
        key = _float_key(s_ref[:, cols])
        if (c + 1) * LANES > past:
            key = jnp.where(c * LANES + col - past <= tq, key, INT_MIN)
        keys_ref[:, cols] = key
    _topk_select(keys_ref, bias_ref, rows=rows, nch=nch, ch=LANES, k=k,
                 idx_of=lambda c: c * LANES + col, n_total_ch=nch)


def _s_topk(scores, *, past, nt, k):
    r, lw = scores.shape
    return pl.pallas_call(
        functools.partial(_s_topk_body, past=past, nt=nt, k=k, lw=lw),
        grid=(r // TOPK_ROWS,),
        in_specs=[pl.BlockSpec((TOPK_ROWS, lw), lambda i: (i, 0))],
        out_specs=pl.BlockSpec((TOPK_ROWS, lw), lambda i: (i, 0)),
        out_shape=jax.ShapeDtypeStruct((r, lw), F32),
        scratch_shapes=[pltpu.VMEM((TOPK_ROWS, lw), I32)],
        compiler_params=_cparams("parallel"),
        name="sample_topk",
    )(scores)


NT_PAD = 8


def _softmax_pv(s_past, s_new, vt_past, v_new):
    m = jnp.maximum(jnp.max(s_past, axis=-1, keepdims=True), jnp.max(s_new, axis=-1, keepdims=True))
    p_past = jnp.exp(s_past - m)
    p_new = jnp.exp(s_new - m)
    l = jnp.sum(p_past, axis=-1, keepdims=True) + jnp.sum(p_new, axis=-1, keepdims=True)
    return (_dot_nt(p_past.astype(BF16), vt_past) + _dot(p_new.astype(BF16), v_new)) / l


def _head_t(page_refs, h):
    return jnp.concatenate([r[h] for r in page_refs], axis=1).astype(BF16)


def _s_dsa_body(pt_ref, q_ref, bias_ref, *refs, n_pages):
    kps, knew = refs[:n_pages], refs[n_pages]
    vps, vnew = refs[n_pages + 1:2 * n_pages + 1], refs[2 * n_pages + 1]
    o_ref = refs[2 * n_pages + 2]
    past = n_pages * LANES
    hg = B_HEADS // B_KV_HEADS
    bias = bias_ref[...]
    b_past = jnp.concatenate([bias[:, :past]] * hg, axis=0)
    b_new = jnp.concatenate([bias[:, past:past + NT_PAD]] * hg, axis=0)
    for g in range(B_KV_HEADS):
        q = q_ref[g]
        s_past = _dot(q, _head_t(kps, g)) + b_past
        s_new = _dot_nt(q, knew[:, g, :].astype(BF16)) + b_new
        o_ref[g] = _softmax_pv(s_past, s_new, _head_t(vps, g), vnew[:, g, :].astype(BF16))


def _s_dsa(pt, qb_s, bias_s, cache_bk, cache_bv, k_new, v_new, *, layer, ns, n_pages):
    lw = (n_pages + 1) * LANES
    hg = B_HEADS // B_KV_HEADS
    seq4 = lambda b, pt: (b, 0, 0, 0)
    page = (B_KV_HEADS, HEAD_DIM, LANES)
    new = pl.BlockSpec((None, NT_PAD, B_KV_HEADS, HEAD_DIM), seq4)
    return pl.pallas_call(
        functools.partial(_s_dsa_body, n_pages=n_pages),
        grid_spec=pltpu.PrefetchScalarGridSpec(
            num_scalar_prefetch=1, grid=(ns,),
            in_specs=[pl.BlockSpec((None, B_KV_HEADS, hg * NT_PAD, HEAD_DIM), seq4),
                      pl.BlockSpec((None, NT_PAD, lw), lambda b, pt: (b, 0, 0))]
            + _page_specs(n_pages, layer, page) + [new]
            + _page_specs(n_pages, layer, page) + [new],
            out_specs=pl.BlockSpec((None, B_KV_HEADS, hg * NT_PAD, HEAD_DIM), seq4)),
        out_shape=jax.ShapeDtypeStruct((ns, B_KV_HEADS, hg * NT_PAD, HEAD_DIM), F32),
        compiler_params=_cparams("arbitrary"),
        name="sample_dsa",
    )(pt, qb_s, bias_s, *([cache_bk] * n_pages), k_new, *([cache_bv] * n_pages), v_new)


def _s_fox_body(pt_ref, q_ref, *refs, n_pages):
    kps, knew = refs[:n_pages], refs[n_pages]
    vps, vnew = refs[n_pages + 1:2 * n_pages + 1], refs[2 * n_pages + 1]
    fps, fnew = refs[2 * n_pages + 2:3 * n_pages + 2], refs[3 * n_pages + 2]
    o_ref = refs[3 * n_pages + 3]
    hi = lax.Precision.HIGHEST
    r = lax.broadcasted_iota(I32, (LANES, LANES), 0)
    c = lax.broadcasted_iota(I32, (LANES, LANES), 1)
    triu = jnp.where(r <= c, 1.0, 0.0).astype(F32)
    triu_new = triu[:NT_PAD, :NT_PAD]
    off = jnp.zeros((C_HEADS, 1), F32)
    cums = []
    for j in range(n_pages):
        cum = _dot(fps[j][...], triu, precision=hi)
        cums.append(cum + off)
        off = off + cum[:, LANES - 1:LANES]
    cum_past = jnp.concatenate(cums, axis=1)
    cum_new = _dot(fnew[...], triu_new, precision=hi) + off
    causal_new = c[:NT_PAD, :NT_PAD] <= r[:NT_PAD, :NT_PAD]
    for h in range(C_HEADS):
        q = q_ref[h]
        s_past = _dot(q, _head_t(kps, h)) - cum_past[h:h + 1, :]
        s_new = jnp.where(causal_new, _dot_nt(q, knew[:, h, :].astype(BF16)) - cum_new[h:h + 1, :], NEG)
        o_ref[h] = _softmax_pv(s_past, s_new, _head_t(vps, h), vnew[:, h, :].astype(BF16))


def _s_fox(pt, qc_s, cache_ck, cache_cv, cache_lf, k_new, v_new, lf_new, *, layer, ns, n_pages):
    seq4 = lambda b, pt: (b, 0, 0, 0)
    page = (C_HEADS, HEAD_DIM, LANES)
    new = pl.BlockSpec((None, NT_PAD, C_HEADS, HEAD_DIM), seq4)
    return pl.pallas_call(
        functools.partial(_s_fox_body, n_pages=n_pages),
        grid_spec=pltpu.PrefetchScalarGridSpec(
            num_scalar_prefetch=1, grid=(ns,),
            in_specs=[pl.BlockSpec((None, C_HEADS, NT_PAD, HEAD_DIM), seq4)]
            + _page_specs(n_pages, layer, page) + [new]
            + _page_specs(n_pages, layer, page) + [new]
            + _page_specs(n_pages, layer, (C_HEADS, LANES))
            + [pl.BlockSpec((None, C_HEADS, NT_PAD), lambda b, pt: (b, 0, 0))],
            out_specs=pl.BlockSpec((None, C_HEADS, NT_PAD, HEAD_DIM), seq4)),
        out_shape=jax.ShapeDtypeStruct((ns, C_HEADS, NT_PAD, HEAD_DIM), F32),
        compiler_params=_cparams("arbitrary"),
        name="sample_fox",
    )(pt, qc_s, *([cache_ck] * n_pages), k_new, *([cache_cv] * n_pages), v_new, *([cache_lf] * n_pages), lf_new)


def _mixout_body(x_ref, gn_ref, wg_ref, wb_ref, wo_ref, oa_ref, ob_ref, oc_ref, o_ref):
    x = x_ref[...]
    d = x.shape[1]
    xn = _rms(x, gn_ref[...]).astype(BF16)
    merged = jnp.zeros(x.shape, F32)
    for n, br in enumerate((oa_ref, ob_ref, oc_ref)):
        gate = jax.nn.sigmoid(_dot(xn, wg_ref[:, n * d:(n + 1) * d]))
        merged = merged + _dot(br[...], wb_ref[n]) * gate
    o_ref[...] = x + _dot(merged.astype(BF16), wo_ref[...])


def _mixout(x, gn, wg16, wb16, wo16, oa, ob, oc, *, tm):
    r, d = x.shape
    wbr = oa.shape[1]
    row = lambda i: (i, 0)
    const2 = lambda i: (0, 0)
    return pl.pallas_call(
        _mixout_body,
        grid=(r // tm,),
        in_specs=[
            pl.BlockSpec((tm, d), row),
            pl.BlockSpec((1, d), const2),
            pl.BlockSpec((d, N_BRANCH * d), const2),
            pl.BlockSpec((N_BRANCH, wbr, d), lambda i: (0, 0, 0)),
            pl.BlockSpec((d, d), const2),
            pl.BlockSpec((tm, wbr), row),
            pl.BlockSpec((tm, wbr), row),
            pl.BlockSpec((tm, wbr), row),
        ],
        out_specs=pl.BlockSpec((tm, d), row),
        out_shape=jax.ShapeDtypeStruct((r, d), F32),
        compiler_params=_cparams("parallel"),
        name="mix_out",
    )(x, gn, wg16, wb16, wo16, oa, ob, oc)


def _rope_tables(pos):
    rot = HEAD_DIM // ROT_FRAC
    half = rot // 2
    inv = ROPE_THETA ** (-jnp.arange(half, dtype=F32) / half)
    ang = pos.astype(F32)[:, None] * inv[None, :]
    cos, sin = jnp.cos(ang), jnp.sin(ang)
    ones = jnp.ones((pos.shape[0], HEAD_DIM - rot), F32)
    c64 = jnp.concatenate([cos, cos, ones], axis=1)
    s64 = jnp.concatenate([-sin, sin, 0.0 * ones], axis=1)
    return jnp.tile(c64, (1, LANES // HEAD_DIM)), jnp.tile(s64, (1, LANES // HEAD_DIM))


def _layer_weights(l, norm_ffn1, ffn1_wi, ffn1_wo, norm_mix, w_in, b_forget, a_ln_g, a_ln_b, a_ws, a_bs,
                   w_branch, w_out, norm_ffn2, ffn2_wi, ffn2_wo, nt):
    d = w_in.shape[1]
    w = w_in[l]
    widths = (N_BRANCH * d, d, 512, 128, 128, 256, 64, 4, 512, 512, 512, 8)
    cuts = np.concatenate([[0], np.cumsum(widths)])
    (w_gate, w_a, w_bq, w_bk, w_bv, w_iq, w_ik, w_iw, w_cq, w_ck, w_cv, w_cf) = [
        w[:, int(cuts[i]):int(cuts[i + 1])] for i in range(len(widths))]
    slots = []
    for h in range(B_HEADS):
        g = h // (B_HEADS // B_KV_HEADS)
        wh = w_bq[:, h * HEAD_DIM:(h + 1) * HEAD_DIM]
        slots.append(jnp.pad(wh, ((0, 0), (g * HEAD_DIM, (B_KV_HEADS - 1 - g) * HEAD_DIM))))
    cslots = []
    for h in range(C_HEADS):
        wh = w_cq[:, h * HEAD_DIM:(h + 1) * HEAD_DIM]
        cslots.append(jnp.pad(wh, ((0, 0), ((h % 2) * HEAD_DIM, (1 - h % 2) * HEAD_DIM))))
    w_misc = jnp.pad(jnp.concatenate([w_cf, w_iw], axis=1), ((0, 0), (0, LANES - 12)))
    w2 = jnp.concatenate([w_a] + slots + [w_bk, w_bv, w_iq, w_ik, w_ik, w_misc] + cslots + [w_ck, w_cv], axis=1)
    ws = a_ws[l]
    tril = jnp.tril(jnp.ones((LANES, LANES), bool))
    ws_p = jnp.where(tril, ws, 0.0)
    corner = jnp.where(tril[:nt, :nt], ws[:, :nt, :nt], 0.0)
    ws_s = jnp.einsum("ij,gts->gitjs", jnp.eye(LANES // nt, dtype=F32), corner).reshape(A_GROUPS, LANES, LANES)
    bs = a_bs[l]
    ab_p = jnp.repeat(bs.T, LANES, axis=1)
    ab_s = jnp.repeat(jnp.tile(bs[:, :nt].T, (LANES // nt, 1)), LANES, axis=1)
    return dict(
        n1=norm_ffn1[l][None], wi1=ffn1_wi[l].astype(BF16), wo1=ffn1_wo[l].astype(BF16),
        n2=norm_ffn2[l][None], wi2=ffn2_wi[l].astype(BF16), wo2=ffn2_wo[l].astype(BF16),
        nm=norm_mix[l][None], w2=w2.astype(BF16), wg=w_gate.astype(BF16),
        wb=w_branch[l].astype(BF16), wo=w_out[l].astype(BF16),
        lng=a_ln_g[l][None], lnb=a_ln_b[l][None],
        ws_p=ws_p.astype(BF16), ws_s=ws_s.astype(BF16), ab_p=ab_p, ab_s=ab_s,
        bf=jnp.pad(b_forget[l], (0, LANES - C_HEADS))[None],
    )


def _pad_rows(a, n):
    return jnp.pad(a, ((0, 0), (0, n - a.shape[1]), (0, 0)))


def _slot_heads(x, ns, nt, lane_half):
    x = x.reshape(ns, nt, -1, LANES)
    heads = [x[:, :, h, lane_half(h) * HEAD_DIM:(lane_half(h) + 1) * HEAD_DIM] for h in range(x.shape[2])]
    return jnp.pad(jnp.stack(heads, axis=1), ((0, 0), (0, 0), (0, NT_PAD - nt), (0, 0)))


def _new_rows(x, ns, nt):
    x = x.reshape(ns, nt, -1, HEAD_DIM)
    return jnp.pad(x, ((0, 0), (0, NT_PAD - nt), (0, 0), (0, 0)))


def _unpad_heads(o, nt):
    ns, nh = o.shape[:2]
    return o[:, :, :nt].transpose(0, 2, 1, 3).reshape(ns * nt, nh * HEAD_DIM).astype(BF16)


def kernel(x_prompt, x_sample, cache_b_k, cache_b_v, cache_b_kidx, cache_c_k, cache_c_v, cache_c_logf, page_table,
           norm_ffn1, ffn1_wi, ffn1_wo, norm_mix, w_in, b_forget, a_ln_g, a_ln_b, a_ws, a_bs, w_branch, w_out,
           norm_ffn2, ffn2_wi, ffn2_wo, norm_final):
    nb, t, d = x_prompt.shape
    ns, nt, _ = x_sample.shape
    depth = w_in.shape[0]
    n_pages = page_table.shape[1]
    page = cache_b_k.shape[2]
    past = n_pages * page
    assert page == LANES and d == 1024 and t % 512 == 0 and (ns * nt) % TOPK_ROWS == 0 and LANES % nt == 0
    assert nt <= NT_PAD
    k_prompt = min(TOPK_MAX, t // 4)
    k_sample = min(TOPK_MAX, (past + nt) // 4)
    rs_rows = ns * nt
    tm_s = min(256, rs_rows)

    pos_p = jnp.arange(t, dtype=I32)
    pos_s = past + (jnp.arange(rs_rows, dtype=I32) % nt)
    rc_p, rs_p = _rope_tables(pos_p)
    rc_s, rs_s = _rope_tables(pos_s)
    gfin = norm_final[None]

    cbk = cache_b_k.transpose(0, 1, 3, 4, 2)
    cbv = cache_b_v.transpose(0, 1, 3, 4, 2)
    cki = cache_b_kidx.transpose(0, 1, 3, 2)
    cck = cache_c_k.transpose(0, 1, 3, 4, 2)
    ccv = cache_c_v.transpose(0, 1, 3, 4, 2)
    clf = cache_c_logf.transpose(0, 1, 3, 2)

    hp = x_prompt.reshape(nb * t, d)
    hs = x_sample.reshape(rs_rows, d)
    rows_p, rows_s = [], []
    for l in range(depth):
        lw = _layer_weights(l, norm_ffn1, ffn1_wi, ffn1_wo, norm_mix, w_in, b_forget, a_ln_g, a_ln_b, a_ws, a_bs,
                            w_branch, w_out, norm_ffn2, ffn2_wi, ffn2_wo, nt)
        last = l == depth - 1
        hp = _ffn(hp, lw["n1"], lw["wi1"], lw["wo1"], gfin, final_norm=False, tm=512, tf=1408)
        (oa, _, qb, kvb32, kvb16, qi, ki32, ki16, misc, qc, kc32, vc32, kc16, vc16) = _mixin(
            hp, lw["nm"], lw["w2"], lw["lng"], lw["lnb"], lw["ws_p"], lw["ab_p"], lw["bf"], rc_p, rs_p, tm=256)
        cum_t = _cum_t(misc, nb, t)
        oc = _fox_prompt(qc, kc16, vc16, cum_t, nb=nb, t=t, tq=128, tk=512)
        bias = _idx_topk_prompt(qi, misc, ki16, nb=nb, t=t, k=k_prompt)
        ob = _dsa_prompt(qb, kvb16, bias, nb=nb, t=t, tq=TOPK_ROWS, ch=TOPK_CH)
        hp = _mixout(hp, lw["nm"], lw["wg"], lw["wb"], lw["wo"], oa, ob, oc, tm=256)
        hp = _ffn(hp, lw["n2"], lw["wi2"], lw["wo2"], gfin, final_norm=last, tm=512, tf=1408)
        rows_p.append(dict(k_b=kvb32[:, :128], v_b=kvb32[:, 128:], k_i=ki32[:, :HEAD_DIM], k_c=kc32, v_c=vc32,
                           logf=misc[:, MISC_LOGF:MISC_LOGF + C_HEADS]))
        hs = _ffn(hs, lw["n1"], lw["wi1"], lw["wo1"], gfin, final_norm=False, tm=tm_s, tf=1408)
        (oa, va, qb, kvb32, kvb16, qi, ki32, ki16, misc, qc, kc32, vc32, kc16, vc16) = _mixin(
            hs, lw["nm"], lw["w2"], lw["lng"], lw["lnb"], lw["ws_s"], lw["ab_s"], lw["bf"], rc_s, rs_s, tm=tm_s)
        qi_s = _pad_rows(qi.reshape(ns, nt, IDX_HEADS, HEAD_DIM).transpose(0, 2, 1, 3).reshape(ns * IDX_HEADS, nt, HEAD_DIM),
                         8).reshape(ns, IDX_HEADS * 8, HEAD_DIM)
        wi_s = _pad_rows(misc[:, MISC_WI:MISC_WI + IDX_HEADS].reshape(ns, nt, IDX_HEADS), 8)
        ki_new = _pad_rows(ki16[:, :HEAD_DIM].reshape(ns, nt, HEAD_DIM), page)
        scores = _s_score(page_table, qi_s, wi_s, cki, ki_new, layer=l, ns=ns, n_pages=n_pages)
        bias_s = _s_topk(scores.reshape(rs_rows, -1), past=past, nt=nt, k=k_sample).reshape(ns, nt, -1)
        hg = B_HEADS // B_KV_HEADS
        qb_s = _slot_heads(qb, ns, nt, lambda h: h // hg).reshape(ns, B_KV_HEADS, hg * NT_PAD, HEAD_DIM)
        ob_raw = _s_dsa(page_table, qb_s, _pad_rows(bias_s, NT_PAD), cbk, cbv,
                        _new_rows(kvb32[:, :128], ns, nt), _new_rows(kvb32[:, 128:], ns, nt),
                        layer=l, ns=ns, n_pages=n_pages)
        ob = _unpad_heads(ob_raw.reshape(ns, B_HEADS, NT_PAD, HEAD_DIM), nt)
        qc_s = _slot_heads(qc, ns, nt, lambda h: h % 2)
        lf_new = _pad_rows(misc[:, MISC_LOGF:MISC_LOGF + C_HEADS].reshape(ns, nt, C_HEADS), NT_PAD).transpose(0, 2, 1)
        oc_raw = _s_fox(page_table, qc_s, cck, ccv, clf,
                        _new_rows(kc32, ns, nt), _new_rows(vc32, ns, nt), lf_new, layer=l, ns=ns, n_pages=n_pages)
        oc = _unpad_heads(oc_raw, nt)
        hs = _mixout(hs, lw["nm"], lw["wg"], lw["wb"], lw["wo"], oa, ob, oc, tm=tm_s)
        hs = _ffn(hs, lw["n2"], lw["wi2"], lw["wo2"], gfin, final_norm=last, tm=tm_s, tf=1408)
        rows_s.append(dict(k_b=kvb32[:, :128], v_b=kvb32[:, 128:], k_i=ki32[:, :HEAD_DIM], k_c=kc32, v_c=vc32,
                           logf=misc[:, MISC_LOGF:MISC_LOGF + C_HEADS], v_a=va))

    def stack(rows, key, shape):
        return jnp.stack([r[key] for r in rows]).reshape((depth,) + shape)

    y_prompt = hp.reshape(nb, t, d)
    y_sample = hs.reshape(ns, nt, d)
    outs = [y_prompt, y_sample]
    for rows, lead in ((rows_p, (nb, t)), (rows_s, (ns, nt))):
        outs += [stack(rows, "k_b", lead + (B_KV_HEADS, HEAD_DIM)), stack(rows, "v_b", lead + (B_KV_HEADS, HEAD_DIM)),
                 stack(rows, "k_i", lead + (HEAD_DIM,)), stack(rows, "k_c", lead + (C_HEADS, HEAD_DIM)),
                 stack(rows, "v_c", lead + (C_HEADS, HEAD_DIM)), stack(rows, "logf", lead + (C_HEADS,))]
    outs.append(stack(rows_s, "v_a", (ns, nt, 512)))
    return tuple(outs)
```

```python
import functools

import numpy as np
import jax
import jax.numpy as jnp
from jax import lax
from jax.experimental import pallas as pl
from jax.experimental.pallas import tpu as pltpu

F32 = jnp.float32
BF16 = jnp.bfloat16
I32 = jnp.int32

LANES = 128
HEAD_DIM = 64
N_BRANCH = 3
A_GROUPS = 4
B_HEADS = 8
B_KV_HEADS = 2
IDX_HEADS = 4
C_HEADS = 8
TOPK_MAX = 256
ROT_FRAC = 4
ROPE_THETA = 500000.0
EPS = 1e-6
LOG2E = 1.4426950408889634
QK_SCALE = HEAD_DIM ** -0.5 * LOG2E
IDX_SCALE = HEAD_DIM ** -0.5
NEG = -1e30
INT_MIN = -2 ** 31
VMEM_LIMIT = 56 * 1024 * 1024

_NT = (((1,), (1,)), ((), ()))


def _cparams(*sem):
    return pltpu.CompilerParams(dimension_semantics=sem, vmem_limit_bytes=VMEM_LIMIT)


def _dot(a, b, precision=None):
    return jnp.dot(a, b, preferred_element_type=F32, precision=precision)


def _dot_nt(a, b, precision=None):
    return lax.dot_general(a, b, _NT, preferred_element_type=F32, precision=precision)


def _rms(x, g):
    return x * lax.rsqrt(jnp.mean(x * x, axis=-1, keepdims=True) + EPS) * g


def _ffn_body(x_ref, gn_ref, wig_ref, wiu_ref, wo_ref, gf_ref, o_ref, xn_ref, acc_ref, *, nf, final_norm):
    f = pl.program_id(1)

    @pl.when(f == 0)
    def _():
        xn_ref[...] = _rms(x_ref[...], gn_ref[...]).astype(BF16)
        acc_ref[...] = jnp.zeros_like(acc_ref)

    xn = xn_ref[...]
    g = _dot(xn, wig_ref[...])
    u = _dot(xn, wiu_ref[...])
    h = (jax.nn.silu(g) * u).astype(BF16)
    acc_ref[...] += _dot(h, wo_ref[...])

    @pl.when(f == nf - 1)
    def _():
        y = x_ref[...] + 0.5 * acc_ref[...]
        if final_norm:
            y = _rms(y, gf_ref[...])
        o_ref[...] = y


def _ffn(x, gn, wi16, wo16, gf, *, final_norm, tm, tf):
    r, d = x.shape
    dff = wo16.shape[0]
    nf = dff // tf
    return pl.pallas_call(
        functools.partial(_ffn_body, nf=nf, final_norm=final_norm),
        grid=(r // tm, nf),
        in_specs=[
            pl.BlockSpec((tm, d), lambda i, f: (i, 0)),
            pl.BlockSpec((1, d), lambda i, f: (0, 0)),
            pl.BlockSpec((d, tf), lambda i, f: (0, f)),
            pl.BlockSpec((d, tf), lambda i, f: (0, nf + f)),
            pl.BlockSpec((tf, d), lambda i, f: (f, 0)),
            pl.BlockSpec((1, d), lambda i, f: (0, 0)),
        ],
        out_specs=pl.BlockSpec((tm, d), lambda i, f: (i, 0)),
        out_shape=jax.ShapeDtypeStruct((r, d), F32),
        scratch_shapes=[pltpu.VMEM((tm, d), BF16), pltpu.VMEM((tm, d), F32)],
        compiler_params=_cparams("parallel", "arbitrary"),
        name="ffn",
    )(x, gn, wi16, wi16, wo16, gf)


C_A = 0
C_BQ = 1024
C_BKV = 2048
C_IQ = 2304
C_KI = 2560
C_MISC = 2688
C_CQ = 2816
C_CK = 3840
C_CV = 4352
C_END = 4864
MISC_LOGF = 0
MISC_WI = 8


def _mixin_body(x_ref, gn_ref, w_ref, lng_ref, lnb_ref, ws_ref, ab_ref, bf_ref, rc_ref, rs_ref,
                oa_ref, va_ref, qb_ref, kvb32_ref, kvb16_ref, qi_ref, ki32_ref, ki16_ref, misc_ref,
                qc_ref, kc32_ref, vc32_ref, kc16_ref, vc16_ref, vbt_ref, *, tm):
    xn = _rms(x_ref[...], gn_ref[...]).astype(BF16)
    rc = rc_ref[...]
    rs = rs_ref[...]
    lane = lax.broadcasted_iota(I32, (tm, LANES), 1)
    first_half = (lane % HEAD_DIM) < (HEAD_DIM // ROT_FRAC // 2)

    def rope(v):
        sw = jnp.where(first_half, pltpu.roll(v, LANES - 8, 1), pltpu.roll(v, 8, 1))
        return v * rc + sw * rs

    def proj(a, b):
        return _dot(xn, w_ref[:, a:b])

    ga = jax.nn.gelu(proj(C_A, C_A + 1024))
    u = ga[:, :512]
    v = ga[:, 512:]
    mu = jnp.mean(v, axis=-1, keepdims=True)
    vc = v - mu
    var = jnp.mean(vc * vc, axis=-1, keepdims=True)
    vn = vc * lax.rsqrt(var + EPS) * lng_ref[...] + lnb_ref[...]
    va_ref[...] = vn
    vn16 = vn.astype(BF16)
    for c in range(tm // LANES):
        rows = slice(c * LANES, (c + 1) * LANES)
        for g in range(A_GROUPS):
            cols = slice(g * LANES, (g + 1) * LANES)
            mixed = _dot(ws_ref[g], vn16[rows, cols]) + ab_ref[:, cols]
            oa_ref[rows, cols] = (u[rows, cols] * mixed).astype(BF16)

    for s in range(B_HEADS):
        cols = slice(s * LANES, (s + 1) * LANES)
        qb_ref[:, cols] = (rope(proj(C_BQ + s * LANES, C_BQ + (s + 1) * LANES)) * QK_SCALE).astype(BF16)
    hk = rope(proj(C_BKV, C_BKV + 128))
    hv = proj(C_BKV + 128, C_BKV + 256)
    kvb32_ref[:, :128] = hk
    kvb32_ref[:, 128:] = hv
    kvb16_ref[:, :128] = hk.astype(BF16)
    kvb16_ref[:, 128:] = hv.astype(BF16)
    for c in range(tm // LANES):
        vbt_ref[:, c * LANES:(c + 1) * LANES] = hv[c * LANES:(c + 1) * LANES, :].T.astype(BF16)
    for s in range(2):
        cols = slice(s * LANES, (s + 1) * LANES)
        qi_ref[:, cols] = (rope(proj(C_IQ + s * LANES, C_IQ + (s + 1) * LANES)) * IDX_SCALE).astype(BF16)
    hki = rope(proj(C_KI, C_KI + 128))
    ki32_ref[...] = hki
    ki16_ref[...] = hki.astype(BF16)
    hm = proj(C_MISC, C_MISC + 128)
    lf = jax.nn.log_sigmoid(hm + bf_ref[...])
    misc_ref[...] = jnp.where(lane < MISC_WI, lf, jnp.where(lane < MISC_WI + IDX_HEADS, hm * 0.5, 0.0))

    for s in range(C_HEADS):
        cols = slice(s * LANES, (s + 1) * LANES)
        qc_ref[:, cols] = (proj(C_CQ + s * LANES, C_CQ + (s + 1) * LANES) * QK_SCALE).astype(BF16)
    hck = proj(C_CK, C_CK + 512)
    kc32_ref[...] = hck
    kc16_ref[...] = hck.astype(BF16)
    hcv = proj(C_CV, C_CV + 512)
    vc32_ref[...] = hcv
    vc16_ref[...] = hcv.astype(BF16)


def _mixin(x, gn, w2, lng, lnb, ws16, abias, bfg, rc, rs, *, tm):
    r, d = x.shape
    npos = rc.shape[0] // tm
    row = lambda i: (i, 0)
    const2 = lambda i: (0, 0)
    widths = [(512, BF16), (512, F32), (1024, BF16), (256, F32), (256, BF16), (256, BF16), (128, F32), (128, BF16),
              (128, F32), (1024, BF16), (512, F32), (512, F32), (512, BF16), (512, BF16)]
    return pl.pallas_call(
        functools.partial(_mixin_body, tm=tm),
        grid=(r // tm,),
        in_specs=[
            pl.BlockSpec((tm, d), row),
            pl.BlockSpec((1, d), const2),
            pl.BlockSpec((d, C_END), const2),
            pl.BlockSpec((1, 512), const2),
            pl.BlockSpec((1, 512), const2),
            pl.BlockSpec((A_GROUPS, LANES, LANES), lambda i: (0, 0, 0)),
            pl.BlockSpec((LANES, 512), const2),
            pl.BlockSpec((1, LANES), const2),
            pl.BlockSpec((tm, LANES), lambda i: (i % npos, 0)),
            pl.BlockSpec((tm, LANES), lambda i: (i % npos, 0)),
        ],
        out_specs=[pl.BlockSpec((tm, w), row) for w, _ in widths] + [pl.BlockSpec((LANES, tm), lambda i: (0, i))],
        out_shape=[jax.ShapeDtypeStruct((r, w), dt) for w, dt in widths] + [jax.ShapeDtypeStruct((LANES, r), BF16)],
        compiler_params=_cparams("parallel"),
        name="mix_in",
    )(x, gn, w2, lng, lnb, ws16, abias, bfg, rc, rs)


def _cum_body(misc_ref, o_ref, *, t):
    r = lax.broadcasted_iota(I32, (LANES, LANES), 0)
    c = lax.broadcasted_iota(I32, (LANES, LANES), 1)
    tri = jnp.where(c <= r, 1.0, 0.0).astype(F32)
    carry = jnp.zeros((1, LANES), F32)
    for ch in range(t // LANES):
        xs = misc_ref[ch * LANES:(ch + 1) * LANES, :]
        cum = _dot(tri, xs, precision=lax.Precision.HIGHEST) + carry
        carry = cum[LANES - 1:LANES, :]
        o_ref[0, :, ch * LANES:(ch + 1) * LANES] = cum.T[0:C_HEADS, :]


def _cum_t(misc, nb, t):
    return pl.pallas_call(
        functools.partial(_cum_body, t=t),
        grid=(nb,),
        in_specs=[pl.BlockSpec((t, LANES), lambda b: (b, 0))],
        out_specs=pl.BlockSpec((1, C_HEADS, t), lambda b: (b, 0, 0)),
        out_shape=jax.ShapeDtypeStruct((nb, C_HEADS, t), F32),
        compiler_params=_cparams("parallel"),
        name="cum_logf",
    )(misc)


def _fox_body(q_ref, k_ref, v_ref, cum_ref, o_ref, *, tq, tk):
    i = pl.program_id(1)
    lo = lax.broadcasted_iota(I32, (tq, LANES), 1) < HEAD_DIM
    t0 = pl.multiple_of(i * tq, tq)
    nfull = t0 // tk
    qpos = t0 + lax.broadcasted_iota(I32, (2 * tq, tk), 0) % tq
    col = lax.broadcasted_iota(I32, (2 * tq, tk), 1)
    c0 = [cum_ref[0, h:h + 1, pl.ds(t0, LANES)][:, 0:1] for h in range(C_HEADS)]

    def step(j, carry, masked):
        s0 = pl.multiple_of(j * tk, tk)
        out = []
        for p in range(C_HEADS // 2):
            cols = slice(p * LANES, (p + 1) * LANES)
            m, l, acc = carry[p]
            qp = jnp.concatenate([q_ref[:, 2 * p * LANES:(2 * p + 1) * LANES],
                                  q_ref[:, (2 * p + 1) * LANES:(2 * p + 2) * LANES]], axis=0)
            s = _dot_nt(qp, k_ref[pl.ds(s0, tk), cols])
            ba = (c0[2 * p] - cum_ref[0, 2 * p:2 * p + 1, pl.ds(s0, tk)]) * LOG2E
            bb = (c0[2 * p + 1] - cum_ref[0, 2 * p + 1:2 * p + 2, pl.ds(s0, tk)]) * LOG2E
            s = jnp.concatenate([s[:tq] + ba, s[tq:] + bb], axis=0)
            if masked:
                s = jnp.where(s0 + col <= qpos, s, NEG)
            m_new = jnp.maximum(m, jnp.max(s, axis=-1, keepdims=True))
            alpha = jnp.exp2(m - m_new)
            pr = jnp.exp2(s - m_new)
            l = alpha * l + jnp.sum(pr, axis=-1, keepdims=True)
            acc = alpha * acc + _dot(pr.astype(BF16), v_ref[pl.ds(s0, tk), cols])
            out.append((m_new, l, acc))
        return tuple(out)

    init = tuple((jnp.full((2 * tq, 1), NEG, F32), jnp.zeros((2 * tq, 1), F32), jnp.zeros((2 * tq, LANES), F32))
                 for _ in range(C_HEADS // 2))
    carry = lax.fori_loop(0, nfull, functools.partial(step, masked=False), init)
    carry = step(nfull, carry, True)
    for p in range(C_HEADS // 2):
        _, l, acc = carry[p]
        out = acc / l
        o_ref[:, p * LANES:(p + 1) * LANES] = jnp.where(lo, out[:tq], out[tq:]).astype(BF16)


def _fox_prompt(qc16, kc16, vc16, cum_t, *, nb, t, tq, tk):
    nq = t // tq
    w = kc16.shape[1]
    return pl.pallas_call(
        functools.partial(_fox_body, tq=tq, tk=tk),
        grid=(nb, nq),
        in_specs=[
            pl.BlockSpec((tq, C_HEADS * LANES), lambda b, i: (b * nq + i, 0)),
            pl.BlockSpec((t, w), lambda b, i: (b, 0)),
            pl.BlockSpec((t, w), lambda b, i: (b, 0)),
            pl.BlockSpec((1, C_HEADS, t), lambda b, i: (b, 0, 0)),
        ],
        out_specs=pl.BlockSpec((tq, w), lambda b, i: (b * nq + i, 0)),
        out_shape=jax.ShapeDtypeStruct(kc16.shape, BF16),
        compiler_params=_cparams("parallel", "arbitrary"),
        name="fox_prompt",
    )(qc16, kc16, vc16, cum_t)


def _float_key(score):
    b = pltpu.bitcast(score, I32)
    return jnp.where(score == 0.0, 0, jnp.where(b >= 0, b, b ^ jnp.int32(0x7FFFFFFF)))


def _topk_select(keys_ref, bias_ref, *, rows, nch, ch, k, idx_of, n_total_ch):
    kf = float(k)

    def count(pred):
        def body(c, part):
            kc = keys_ref[:, pl.ds(pl.multiple_of(c * ch, ch), ch)]
            hit = jnp.where(pred(kc, c), 1.0, 0.0)
            for b in range(ch // LANES):
                part = part + hit[:, b * LANES:(b + 1) * LANES]
            return part
        part = lax.fori_loop(0, nch, body, jnp.zeros((rows, LANES), F32))
        return jnp.sum(part, axis=-1, keepdims=True)

    thr = jnp.where(count(lambda kc, c: kc >= 0) >= kf, jnp.int32(0), jnp.int32(INT_MIN)) + jnp.zeros((rows, 1), I32)

    def vbit(b, thr):
        cand = thr + (jnp.int32(1) << (jnp.int32(30) - b))
        return jnp.where(count(lambda kc, c: kc >= cand) >= kf, cand, thr)

    thr = lax.fori_loop(0, 31, vbit, thr)
    need = kf - count(lambda kc, c: kc > thr)
    ties = count(lambda kc, c: kc == thr)

    def ibit(b, j):
        cand = j + (jnp.int32(1) << (jnp.int32(13) - b))
        return jnp.where(count(lambda kc, c: (kc == thr) & (idx_of(c) < cand)) <= need, cand, j)

    jthr = lax.cond(jnp.max(ties - need) > 0.0,
                    lambda: lax.fori_loop(0, 14, ibit, jnp.zeros((rows, 1), I32)),
                    lambda: jnp.full((rows, 1), 1 << 14, I32))

    def emit(c, _):
        off = pl.multiple_of(c * ch, ch)
        kc = keys_ref[:, pl.ds(off, ch)]
        sel = ((kc > thr) | ((kc == thr) & (idx_of(c) < jthr))) & (kc > INT_MIN)
        bias_ref[:, pl.ds(off, ch)] = jnp.where(sel, 0.0, NEG).astype(bias_ref.dtype)
        return 0

    lax.fori_loop(0, nch, emit, 0)

    def fill(c, _):
        bias_ref[:, pl.ds(pl.multiple_of(c * ch, ch), ch)] = jnp.full((rows, ch), NEG, bias_ref.dtype)
        return 0

    lax.fori_loop(nch, n_total_ch, fill, 0)


TOPK_ROWS = 128
TOPK_CH = 512
COUNT_ACC = 32
BITS_PER_CHECK = 4


def _topk_select_t(keys_ref, bias_ref, *, nch, ch, k, n_total_ch):
    kf = float(k)
    krow = lax.broadcasted_iota(I32, (ch, LANES), 0)

    def count(pred):
        def body(c, part):
            kc = keys_ref[pl.ds(pl.multiple_of(c * ch, ch), ch), :]
            hit = jnp.where(pred(kc, c), 1.0, 0.0)
            return part + jnp.sum(hit.reshape(ch // COUNT_ACC, COUNT_ACC, LANES), axis=0)
        part = lax.fori_loop(0, nch, body, jnp.zeros((COUNT_ACC, LANES), F32))
        return jnp.sum(part, axis=0, keepdims=True)

    done0 = jnp.where(count(lambda kc, c: kc > INT_MIN) <= kf, 1.0, 0.0)
    thr0 = jnp.full((1, LANES), INT_MIN, I32)

    def vcond(state):
        b, _, done = state
        return (b >= 0) & (jnp.min(done) < 0.5)

    def vbits(state):
        b, thr, done = state
        for _ in range(BITS_PER_CHECK):
            cand = jnp.where(done > 0.5, thr, thr + (jnp.int32(1) << b))
            cnt = count(lambda kc, c: kc >= cand)
            thr = jnp.where(cnt >= kf, cand, thr)
            done = jnp.where(cnt == kf, 1.0, done)
            b = b - 1
        return b, thr, done

    _, thr, _ = lax.while_loop(vcond, vbits, (jnp.int32(31), thr0, done0))
    need = kf - count(lambda kc, c: kc > thr)
    ties = jnp.where(thr > INT_MIN, count(lambda kc, c: kc == thr), 0.0)

    def ibit(b, j):
        cand = j + (jnp.int32(1) << (jnp.int32(13) - b))
        hits = count(lambda kc, c: (kc == thr) & (c * ch + krow < cand))
        return jnp.where(hits <= need, cand, j)

    jthr = lax.cond(jnp.max(ties - need) > 0.0,
                    lambda: lax.fori_loop(0, 14, ibit, jnp.zeros((1, LANES), I32)),
                    lambda: jnp.full((1, LANES), 1 << 14, I32))

    def emit(c, _):
        off = pl.multiple_of(c * ch, ch)
        kc = keys_ref[pl.ds(off, ch), :]
        sel = ((kc > thr) | ((kc == thr) & (off + krow < jthr))) & (kc > INT_MIN)
        bias_ref[pl.ds(off, ch), :] = jnp.where(sel, 0.0, NEG).astype(bias_ref.dtype)
        return 0

    lax.fori_loop(0, nch, emit, 0)

    def fill(c, _):
        bias_ref[pl.ds(pl.multiple_of(c * ch, ch), ch), :] = jnp.full((ch, LANES), NEG, bias_ref.dtype)
        return 0

    lax.fori_loop(nch, n_total_ch, fill, 0)


def _idx_topk_body(qi_ref, misc_ref, ki_ref, bias_ref, keys_ref, *, t, k):
    i = pl.program_id(1)
    rows, ch = TOPK_ROWS, TOPK_CH
    q0 = i * rows
    nch = (q0 + rows + ch - 1) // ch
    lo = lax.broadcasted_iota(I32, (rows, LANES), 1) < HEAD_DIM
    qhs = []
    for p in range(IDX_HEADS // 2):
        q2 = qi_ref[:, p * LANES:(p + 1) * LANES]
        zero = jnp.zeros_like(q2)
        qhs += [jnp.where(lo, q2, zero), jnp.where(lo, zero, q2)]
    w_t = misc_ref[...].T
    ws = [w_t[MISC_WI + h:MISC_WI + h + 1, :] for h in range(IDX_HEADS)]
    qpos = q0 + lax.broadcasted_iota(I32, (ch, LANES), 1)
    krow = lax.broadcasted_iota(I32, (ch, LANES), 0)

    def score_chunk(c, _):
        off = pl.multiple_of(c * ch, ch)
        kk = ki_ref[pl.ds(off, ch), :]
        score = jnp.zeros((ch, LANES), F32)
        for h in range(IDX_HEADS):
            score = score + ws[h] * jnp.maximum(_dot_nt(kk, qhs[h]), 0.0)
        keys_ref[pl.ds(off, ch), :] = jnp.where(off + krow <= qpos, _float_key(score), INT_MIN)
        return 0

    lax.fori_loop(0, nch, score_chunk, 0)
    _topk_select_t(keys_ref, bias_ref, nch=nch, ch=ch, k=k, n_total_ch=t // ch)


def _idx_topk_prompt(qi16, misc, ki16, *, nb, t, k):
    nq = t // TOPK_ROWS
    return pl.pallas_call(
        functools.partial(_idx_topk_body, t=t, k=k),
        grid=(nb, nq),
        in_specs=[
            pl.BlockSpec((TOPK_ROWS, 256), lambda b, i: (b * nq + i, 0)),
            pl.BlockSpec((TOPK_ROWS, LANES), lambda b, i: (b * nq + i, 0)),
            pl.BlockSpec((t, LANES), lambda b, i: (b, 0)),
        ],
        out_specs=pl.BlockSpec((None, t, TOPK_ROWS), lambda b, i: (b * nq + i, 0, 0)),
        out_shape=jax.ShapeDtypeStruct((nb * nq, t, TOPK_ROWS), BF16),
        scratch_shapes=[pltpu.VMEM((t, TOPK_ROWS), I32)],
        compiler_params=_cparams("parallel", "arbitrary"),
        name="idx_topk_prompt",
    )(qi16, misc, ki16)


def _dsa_body(q_ref, k_ref, vt_ref, bias_ref, o_ref, *, tq, ch):
    i = pl.program_id(1)
    nch = (i * tq + tq + ch - 1) // ch
    lane = lax.broadcasted_iota(I32, (tq, LANES), 1)
    lo = lane < HEAD_DIM

    q_all = jnp.concatenate([q_ref[:, h * LANES:(h + 1) * LANES] for h in range(B_HEADS)], axis=0)

    def step(c, carry):
        m, l, acc = carry
        off = pl.multiple_of(c * ch, ch)
        bias = bias_ref[pl.ds(off, ch), :].astype(F32)
        s = _dot_nt(k_ref[pl.ds(off, ch), :], q_all) + jnp.concatenate([bias] * B_HEADS, axis=1)
        m_new = jnp.maximum(m, jnp.max(s, axis=0, keepdims=True))
        alpha = jnp.exp2(m - m_new)
        pr = jnp.exp2(s - m_new)
        l = alpha * l + jnp.sum(pr, axis=0, keepdims=True)
        acc = alpha * acc + _dot(vt_ref[:, pl.ds(off, ch)], pr.astype(BF16))
        return m_new, l, acc

    cols = B_HEADS * tq
    init = (jnp.full((1, cols), 2 * NEG, F32), jnp.zeros((1, cols), F32), jnp.zeros((LANES, cols), F32))
    _, l, acc = lax.fori_loop(0, nch, step, init)
    out = acc / l
    res = [out[:, h * tq:(h + 1) * tq].T for h in range(B_HEADS)]
    for p in range(B_HEADS // 2):
        g = (2 * p) // (B_HEADS // B_KV_HEADS)
        a, b = res[2 * p], res[2 * p + 1]
        if g == 0:
            b = pltpu.roll(b, HEAD_DIM, 1)
        else:
            a = pltpu.roll(a, HEAD_DIM, 1)
        o_ref[:, p * LANES:(p + 1) * LANES] = jnp.where(lo, a, b).astype(BF16)


def _dsa_prompt(qb16, kvb16, vbt16, bias_t, *, nb, t, tq, ch):
    nq = t // tq
    return pl.pallas_call(
        functools.partial(_dsa_body, tq=tq, ch=ch),
        grid=(nb, nq),
        in_specs=[
            pl.BlockSpec((tq, B_HEADS * LANES), lambda b, i: (b * nq + i, 0)),
            pl.BlockSpec((t, LANES), lambda b, i: (b, 0)),
            pl.BlockSpec((LANES, t), lambda b, i: (0, b)),
            pl.BlockSpec((None, t, tq), lambda b, i: (b * nq + i, 0, 0)),
        ],
        out_specs=pl.BlockSpec((tq, 512), lambda b, i: (b * nq + i, 0)),
        out_shape=jax.ShapeDtypeStruct((nb * t, 512), BF16),
        compiler_params=_cparams("parallel", "arbitrary"),
        name="dsa_prompt",
    )(qb16, kvb16, vbt16, bias_t)


def _page_specs(n_pages, layer, block):
    def spec(j):
        return pl.BlockSpec((None, None) + block, lambda b, pt: (layer, pt[b, j]) + (0,) * len(block))
    return [spec(j) for j in range(n_pages)]


def _s_score_body(pt_ref, q_ref, w_ref, *refs, n_pages):
    pages, new_ref, o_ref = refs[:n_pages], refs[n_pages], refs[n_pages + 1]
    q = q_ref[...]
    w = w_ref[...]
    for j in range(n_pages + 1):
        s = _dot(q, pages[j][...].astype(BF16)) if j < n_pages else _dot_nt(q, new_ref[...])
        s = jnp.maximum(s, 0.0)
        score = jnp.zeros((8, LANES), F32)
        for h in range(IDX_HEADS):
            score = score + w[:, h:h + 1] * s[h * 8:(h + 1) * 8, :]
        o_ref[:, j * LANES:(j + 1) * LANES] = score[0:4, :]


def _s_score(pt, qi_s, wi_s, cache_kidx, ki_new, *, layer, ns, n_pages):
    lw = (n_pages + 1) * LANES
    return pl.pallas_call(
        functools.partial(_s_score_body, n_pages=n_pages),
        grid_spec=pltpu.PrefetchScalarGridSpec(
            num_scalar_prefetch=1, grid=(ns,),
            in_specs=[pl.BlockSpec((None, 32, HEAD_DIM), lambda b, pt: (b, 0, 0)),
                      pl.BlockSpec((None, 8, IDX_HEADS), lambda b, pt: (b, 0, 0))]
            + _page_specs(n_pages, layer, (HEAD_DIM, LANES))
            + [pl.BlockSpec((None, LANES, HEAD_DIM), lambda b, pt: (b, 0, 0))],
            out_specs=pl.BlockSpec((None, 4, lw), lambda b, pt: (b, 0, 0))),
        out_shape=jax.ShapeDtypeStruct((ns, 4, lw), F32),
        compiler_params=_cparams("arbitrary"),
        name="sample_idx_score",
    )(pt, qi_s, wi_s, *([cache_kidx] * n_pages), ki_new)


def _s_topk_body(s_ref, bias_ref, keys_ref, *, past, nt, k, lw):
    rows = TOPK_ROWS
    col = lax.broadcasted_iota(I32, (rows, LANES), 1)
    tq = lax.broadcasted_iota(I32, (rows, LANES), 0) % nt
    nch = lw // LANES
    for c in range(nch):
        cols = slice(c * LANES, (c + 1) * LANES)
        key = _float_key(s_ref[:, cols])
        if (c + 1) * LANES > past:
            key = jnp.where(c * LANES + col - past <= tq, key, INT_MIN)
        keys_ref[:, cols] = key
    _topk_select(keys_ref, bias_ref, rows=rows, nch=nch, ch=LANES, k=k,
                 idx_of=lambda c: c * LANES + col, n_total_ch=nch)


def _s_topk(scores, *, past, nt, k):
    r, lw = scores.shape
    return pl.pallas_call(
        functools.partial(_s_topk_body, past=past, nt=nt, k=k, lw=lw),
        grid=(r // TOPK_ROWS,),
        in_specs=[pl.BlockSpec((TOPK_ROWS, lw), lambda i: (i, 0))],
        out_specs=pl.BlockSpec((TOPK_ROWS, lw), lambda i: (i, 0)),
        out_shape=jax.ShapeDtypeStruct((r, lw), F32),
        scratch_shapes=[pltpu.VMEM((TOPK_ROWS, lw), I32)],
        compiler_params=_cparams("parallel"),
        name="sample_topk",
    )(scores)


NT_PAD = 8


def _softmax_pv(s_past, s_new, vt_past, v_new):
    m = jnp.maximum(jnp.max(s_past, axis=-1, keepdims=True), jnp.max(s_new, axis=-1, keepdims=True))
    p_past = jnp.exp2(s_past - m)
    p_new = jnp.exp2(s_new - m)
    l = jnp.sum(p_past, axis=-1, keepdims=True) + jnp.sum(p_new, axis=-1, keepdims=True)
    return (_dot_nt(p_past.astype(BF16), vt_past) + _dot(p_new.astype(BF16), v_new)) / l


def _head_t(page_refs, h):
    return jnp.concatenate([r[h] for r in page_refs], axis=1).astype(BF16)


def _s_dsa_body(pt_ref, q_ref, bias_ref, *refs, n_pages):
    kps, knew = refs[:n_pages], refs[n_pages]
    vps, vnew = refs[n_pages + 1:2 * n_pages + 1], refs[2 * n_pages + 1]
    o_ref = refs[2 * n_pages + 2]
    past = n_pages * LANES
    hg = B_HEADS // B_KV_HEADS
    bias = bias_ref[...]
    b_past = jnp.concatenate([bias[:, :past]] * hg, axis=0)
    b_new = jnp.concatenate([bias[:, past:past + NT_PAD]] * hg, axis=0)
    for g in range(B_KV_HEADS):
        q = q_ref[g]
        s_past = _dot(q, _head_t(kps, g)) + b_past
        s_new = _dot_nt(q, knew[:, g, :].astype(BF16)) + b_new
        o_ref[g] = _softmax_pv(s_past, s_new, _head_t(vps, g), vnew[:, g, :].astype(BF16))


def _s_dsa(pt, qb_s, bias_s, cache_bk, cache_bv, k_new, v_new, *, layer, ns, n_pages):
    lw = (n_pages + 1) * LANES
    hg = B_HEADS // B_KV_HEADS
    seq4 = lambda b, pt: (b, 0, 0, 0)
    page = (B_KV_HEADS, HEAD_DIM, LANES)
    new = pl.BlockSpec((None, NT_PAD, B_KV_HEADS, HEAD_DIM), seq4)
    return pl.pallas_call(
        functools.partial(_s_dsa_body, n_pages=n_pages),
        grid_spec=pltpu.PrefetchScalarGridSpec(
            num_scalar_prefetch=1, grid=(ns,),
            in_specs=[pl.BlockSpec((None, B_KV_HEADS, hg * NT_PAD, HEAD_DIM), seq4),
                      pl.BlockSpec((None, NT_PAD, lw), lambda b, pt: (b, 0, 0))]
            + _page_specs(n_pages, layer, page) + [new]
            + _page_specs(n_pages, layer, page) + [new],
            out_specs=pl.BlockSpec((None, B_KV_HEADS, hg * NT_PAD, HEAD_DIM), seq4)),
        out_shape=jax.ShapeDtypeStruct((ns, B_KV_HEADS, hg * NT_PAD, HEAD_DIM), F32),
        compiler_params=_cparams("arbitrary"),
        name="sample_dsa",
    )(pt, qb_s, bias_s, *([cache_bk] * n_pages), k_new, *([cache_bv] * n_pages), v_new)


def _s_fox_body(pt_ref, q_ref, *refs, n_pages):
    kps, knew = refs[:n_pages], refs[n_pages]
    vps, vnew = refs[n_pages + 1:2 * n_pages + 1], refs[2 * n_pages + 1]
    fps, fnew = refs[2 * n_pages + 2:3 * n_pages + 2], refs[3 * n_pages + 2]
    o_ref = refs[3 * n_pages + 3]
    hi = lax.Precision.HIGHEST
    r = lax.broadcasted_iota(I32, (LANES, LANES), 0)
    c = lax.broadcasted_iota(I32, (LANES, LANES), 1)
    triu = jnp.where(r <= c, 1.0, 0.0).astype(F32)
    triu_new = triu[:NT_PAD, :NT_PAD]
    off = jnp.zeros((C_HEADS, 1), F32)
    cums = []
    for j in range(n_pages):
        cum = _dot(fps[j][...], triu, precision=hi)
        cums.append(cum + off)
        off = off + cum[:, LANES - 1:LANES]
    cum_past = jnp.concatenate(cums, axis=1) * LOG2E
    cum_new = (_dot(fnew[...], triu_new, precision=hi) + off) * LOG2E
    causal_new = c[:NT_PAD, :NT_PAD] <= r[:NT_PAD, :NT_PAD]
    for h in range(C_HEADS):
        q = q_ref[h]
        s_past = _dot(q, _head_t(kps, h)) - cum_past[h:h + 1, :]
        s_new = jnp.where(causal_new, _dot_nt(q, knew[:, h, :].astype(BF16)) - cum_new[h:h + 1, :], NEG)
        o_ref[h] = _softmax_pv(s_past, s_new, _head_t(vps, h), vnew[:, h, :].astype(BF16))


def _s_fox(pt, qc_s, cache_ck, cache_cv, cache_lf, k_new, v_new, lf_new, *, layer, ns, n_pages):
    seq4 = lambda b, pt: (b, 0, 0, 0)
    page = (C_HEADS, HEAD_DIM, LANES)
    new = pl.BlockSpec((None, NT_PAD, C_HEADS, HEAD_DIM), seq4)
    return pl.pallas_call(
        functools.partial(_s_fox_body, n_pages=n_pages),
        grid_spec=pltpu.PrefetchScalarGridSpec(
            num_scalar_prefetch=1, grid=(ns,),
            in_specs=[pl.BlockSpec((None, C_HEADS, NT_PAD, HEAD_DIM), seq4)]
            + _page_specs(n_pages, layer, page) + [new]
            + _page_specs(n_pages, layer, page) + [new]
            + _page_specs(n_pages, layer, (C_HEADS, LANES))
            + [pl.BlockSpec((None, C_HEADS, NT_PAD), lambda b, pt: (b, 0, 0))],
            out_specs=pl.BlockSpec((None, C_HEADS, NT_PAD, HEAD_DIM), seq4)),
        out_shape=jax.ShapeDtypeStruct((ns, C_HEADS, NT_PAD, HEAD_DIM), F32),
        compiler_params=_cparams("arbitrary"),
        name="sample_fox",
    )(pt, qc_s, *([cache_ck] * n_pages), k_new, *([cache_cv] * n_pages), v_new, *([cache_lf] * n_pages), lf_new)


def _mixout_body(x_ref, gn_ref, wg_ref, wb_ref, wo_ref, oa_ref, ob_ref, oc_ref, o_ref):
    x = x_ref[...]
    d = x.shape[1]
    xn = _rms(x, gn_ref[...]).astype(BF16)
    merged = jnp.zeros(x.shape, F32)
    for n, br in enumerate((oa_ref, ob_ref, oc_ref)):
        gate = jax.nn.sigmoid(_dot(xn, wg_ref[:, n * d:(n + 1) * d]))
        merged = merged + _dot(br[...], wb_ref[n]) * gate
    o_ref[...] = x + _dot(merged.astype(BF16), wo_ref[...])


def _mixout(x, gn, wg16, wb16, wo16, oa, ob, oc, *, tm):
    r, d = x.shape
    wbr = oa.shape[1]
    row = lambda i: (i, 0)
    const2 = lambda i: (0, 0)
    return pl.pallas_call(
        _mixout_body,
        grid=(r // tm,),
        in_specs=[
            pl.BlockSpec((tm, d), row),
            pl.BlockSpec((1, d), const2),
            pl.BlockSpec((d, N_BRANCH * d), const2),
            pl.BlockSpec((N_BRANCH, wbr, d), lambda i: (0, 0, 0)),
            pl.BlockSpec((d, d), const2),
            pl.BlockSpec((tm, wbr), row),
            pl.BlockSpec((tm, wbr), row),
            pl.BlockSpec((tm, wbr), row),
        ],
        out_specs=pl.BlockSpec((tm, d), row),
        out_shape=jax.ShapeDtypeStruct((r, d), F32),
        compiler_params=_cparams("parallel"),
        name="mix_out",
    )(x, gn, wg16, wb16, wo16, oa, ob, oc)


def _rope_tables(pos):
    rot = HEAD_DIM // ROT_FRAC
    half = rot // 2
    inv = ROPE_THETA ** (-jnp.arange(half, dtype=F32) / half)
    ang = pos.astype(F32)[:, None] * inv[None, :]
    cos, sin = jnp.cos(ang), jnp.sin(ang)
    ones = jnp.ones((pos.shape[0], HEAD_DIM - rot), F32)
    c64 = jnp.concatenate([cos, cos, ones], axis=1)
    s64 = jnp.concatenate([-sin, sin, 0.0 * ones], axis=1)
    return jnp.tile(c64, (1, LANES // HEAD_DIM)), jnp.tile(s64, (1, LANES // HEAD_DIM))


def _layer_weights(l, norm_ffn1, ffn1_wi, ffn1_wo, norm_mix, w_in, b_forget, a_ln_g, a_ln_b, a_ws, a_bs,
                   w_branch, w_out, norm_ffn2, ffn2_wi, ffn2_wo, nt):
    d = w_in.shape[1]
    w = w_in[l]
    widths = (N_BRANCH * d, d, 512, 128, 128, 256, 64, 4, 512, 512, 512, 8)
    cuts = np.concatenate([[0], np.cumsum(widths)])
    (w_gate, w_a, w_bq, w_bk, w_bv, w_iq, w_ik, w_iw, w_cq, w_ck, w_cv, w_cf) = [
        w[:, int(cuts[i]):int(cuts[i + 1])] for i in range(len(widths))]
    slots = []
    for h in range(B_HEADS):
        g = h // (B_HEADS // B_KV_HEADS)
        wh = w_bq[:, h * HEAD_DIM:(h + 1) * HEAD_DIM]
        slots.append(jnp.pad(wh, ((0, 0), (g * HEAD_DIM, (B_KV_HEADS - 1 - g) * HEAD_DIM))))
    cslots = []
    for h in range(C_HEADS):
        wh = w_cq[:, h * HEAD_DIM:(h + 1) * HEAD_DIM]
        cslots.append(jnp.pad(wh, ((0, 0), ((h % 2) * HEAD_DIM, (1 - h % 2) * HEAD_DIM))))
    w_misc = jnp.pad(jnp.concatenate([w_cf, w_iw], axis=1), ((0, 0), (0, LANES - 12)))
    w2 = jnp.concatenate([w_a] + slots + [w_bk, w_bv, w_iq, w_ik, w_ik, w_misc] + cslots + [w_ck, w_cv], axis=1)
    ws = a_ws[l]
    tril = jnp.tril(jnp.ones((LANES, LANES), bool))
    ws_p = jnp.where(tril, ws, 0.0)
    corner = jnp.where(tril[:nt, :nt], ws[:, :nt, :nt], 0.0)
    ws_s = jnp.einsum("ij,gts->gitjs", jnp.eye(LANES // nt, dtype=F32), corner).reshape(A_GROUPS, LANES, LANES)
    bs = a_bs[l]
    ab_p = jnp.repeat(bs.T, LANES, axis=1)
    ab_s = jnp.repeat(jnp.tile(bs[:, :nt].T, (LANES // nt, 1)), LANES, axis=1)
    return dict(
        n1=norm_ffn1[l][None], wi1=ffn1_wi[l].astype(BF16), wo1=ffn1_wo[l].astype(BF16),
        n2=norm_ffn2[l][None], wi2=ffn2_wi[l].astype(BF16), wo2=ffn2_wo[l].astype(BF16),
        nm=norm_mix[l][None], w2=w2.astype(BF16), wg=w_gate.astype(BF16),
        wb=w_branch[l].astype(BF16), wo=w_out[l].astype(BF16),
        lng=a_ln_g[l][None], lnb=a_ln_b[l][None],
        ws_p=ws_p.astype(BF16), ws_s=ws_s.astype(BF16), ab_p=ab_p, ab_s=ab_s,
        bf=jnp.pad(b_forget[l], (0, LANES - C_HEADS))[None],
    )


def _pad_rows(a, n):
    return jnp.pad(a, ((0, 0), (0, n - a.shape[1]), (0, 0)))


def _slot_heads(x, ns, nt, lane_half):
    x = x.reshape(ns, nt, -1, LANES)
    heads = [x[:, :, h, lane_half(h) * HEAD_DIM:(lane_half(h) + 1) * HEAD_DIM] for h in range(x.shape[2])]
    return jnp.pad(jnp.stack(heads, axis=1), ((0, 0), (0, 0), (0, NT_PAD - nt), (0, 0)))


def _new_rows(x, ns, nt):
    x = x.reshape(ns, nt, -1, HEAD_DIM)
    return jnp.pad(x, ((0, 0), (0, NT_PAD - nt), (0, 0), (0, 0)))


def _unpad_heads(o, nt):
    ns, nh = o.shape[:2]
    return o[:, :, :nt].transpose(0, 2, 1, 3).reshape(ns * nt, nh * HEAD_DIM).astype(BF16)


def kernel(x_prompt, x_sample, cache_b_k, cache_b_v, cache_b_kidx, cache_c_k, cache_c_v, cache_c_logf, page_table,
           norm_ffn1, ffn1_wi, ffn1_wo, norm_mix, w_in, b_forget, a_ln_g, a_ln_b, a_ws, a_bs, w_branch, w_out,
           norm_ffn2, ffn2_wi, ffn2_wo, norm_final):
    nb, t, d = x_prompt.shape
    ns, nt, _ = x_sample.shape
    depth = w_in.shape[0]
    n_pages = page_table.shape[1]
    page = cache_b_k.shape[2]
    past = n_pages * page
    assert page == LANES and d == 1024 and t % 512 == 0 and (ns * nt) % TOPK_ROWS == 0 and LANES % nt == 0
    assert nt <= NT_PAD
    k_prompt = min(TOPK_MAX, t // 4)
    k_sample = min(TOPK_MAX, (past + nt) // 4)
    rs_rows = ns * nt
    tm_s = min(256, rs_rows)

    pos_p = jnp.arange(t, dtype=I32)
    pos_s = past + (jnp.arange(rs_rows, dtype=I32) % nt)
    rc_p, rs_p = _rope_tables(pos_p)
    rc_s, rs_s = _rope_tables(pos_s)
    gfin = norm_final[None]

    cbk = cache_b_k.transpose(0, 1, 3, 4, 2)
    cbv = cache_b_v.transpose(0, 1, 3, 4, 2)
    cki = cache_b_kidx.transpose(0, 1, 3, 2)
    cck = cache_c_k.transpose(0, 1, 3, 4, 2)
    ccv = cache_c_v.transpose(0, 1, 3, 4, 2)
    clf = cache_c_logf.transpose(0, 1, 3, 2)

    hp = x_prompt.reshape(nb * t, d)
    hs = x_sample.reshape(rs_rows, d)
    rows_p, rows_s = [], []
    for l in range(depth):
        lw = _layer_weights(l, norm_ffn1, ffn1_wi, ffn1_wo, norm_mix, w_in, b_forget, a_ln_g, a_ln_b, a_ws, a_bs,
                            w_branch, w_out, norm_ffn2, ffn2_wi, ffn2_wo, nt)
        last = l == depth - 1
        hp = _ffn(hp, lw["n1"], lw["wi1"], lw["wo1"], gfin, final_norm=False, tm=512, tf=1408)
        (oa, _, qb, kvb32, kvb16, qi, ki32, ki16, misc, qc, kc32, vc32, kc16, vc16, vbt) = _mixin(
            hp, lw["nm"], lw["w2"], lw["lng"], lw["lnb"], lw["ws_p"], lw["ab_p"], lw["bf"], rc_p, rs_p, tm=256)
        cum_t = _cum_t(misc, nb, t)
        oc = _fox_prompt(qc, kc16, vc16, cum_t, nb=nb, t=t, tq=128, tk=512)
        bias = _idx_topk_prompt(qi, misc, ki16, nb=nb, t=t, k=k_prompt)
        ob = _dsa_prompt(qb, kvb16, vbt, bias, nb=nb, t=t, tq=TOPK_ROWS, ch=TOPK_CH)
        hp = _mixout(hp, lw["nm"], lw["wg"], lw["wb"], lw["wo"], oa, ob, oc, tm=256)
        hp = _ffn(hp, lw["n2"], lw["wi2"], lw["wo2"], gfin, final_norm=last, tm=512, tf=1408)
        rows_p.append(dict(k_b=kvb32[:, :128], v_b=kvb32[:, 128:], k_i=ki32[:, :HEAD_DIM], k_c=kc32, v_c=vc32,
                           logf=misc[:, MISC_LOGF:MISC_LOGF + C_HEADS]))
        hs = _ffn(hs, lw["n1"], lw["wi1"], lw["wo1"], gfin, final_norm=False, tm=tm_s, tf=1408)
        (oa, va, qb, kvb32, kvb16, qi, ki32, ki16, misc, qc, kc32, vc32, kc16, vc16, _) = _mixin(
            hs, lw["nm"], lw["w2"], lw["lng"], lw["lnb"], lw["ws_s"], lw["ab_s"], lw["bf"], rc_s, rs_s, tm=tm_s)
        qi_s = _pad_rows(qi.reshape(ns, nt, IDX_HEADS, HEAD_DIM).transpose(0, 2, 1, 3).reshape(ns * IDX_HEADS, nt, HEAD_DIM),
                         8).reshape(ns, IDX_HEADS * 8, HEAD_DIM)
        wi_s = _pad_rows(misc[:, MISC_WI:MISC_WI + IDX_HEADS].reshape(ns, nt, IDX_HEADS), 8)
        ki_new = _pad_rows(ki16[:, :HEAD_DIM].reshape(ns, nt, HEAD_DIM), page)
        scores = _s_score(page_table, qi_s, wi_s, cki, ki_new, layer=l, ns=ns, n_pages=n_pages)
        bias_s = _s_topk(scores.reshape(rs_rows, -1), past=past, nt=nt, k=k_sample).reshape(ns, nt, -1)
        hg = B_HEADS // B_KV_HEADS
        qb_s = _slot_heads(qb, ns, nt, lambda h: h // hg).reshape(ns, B_KV_HEADS, hg * NT_PAD, HEAD_DIM)
        ob_raw = _s_dsa(page_table, qb_s, _pad_rows(bias_s, NT_PAD), cbk, cbv,
                        _new_rows(kvb32[:, :128], ns, nt), _new_rows(kvb32[:, 128:], ns, nt),
                        layer=l, ns=ns, n_pages=n_pages)
        ob = _unpad_heads(ob_raw.reshape(ns, B_HEADS, NT_PAD, HEAD_DIM), nt)
        qc_s = _slot_heads(qc, ns, nt, lambda h: h % 2)
        lf_new = _pad_rows(misc[:, MISC_LOGF:MISC_LOGF + C_HEADS].reshape(ns, nt, C_HEADS), NT_PAD).transpose(0, 2, 1)
        oc_raw = _s_fox(page_table, qc_s, cck, ccv, clf,
                        _new_rows(kc32, ns, nt), _new_rows(vc32, ns, nt), lf_new, layer=l, ns=ns, n_pages=n_pages)
        oc = _unpad_heads(oc_raw, nt)
        hs = _mixout(hs, lw["nm"], lw["wg"], lw["wb"], lw["wo"], oa, ob, oc, tm=tm_s)
        hs = _ffn(hs, lw["n2"], lw["wi2"], lw["wo2"], gfin, final_norm=last, tm=tm_s, tf=1408)
        rows_s.append(dict(k_b=kvb32[:, :128], v_b=kvb32[:, 128:], k_i=ki32[:, :HEAD_DIM], k_c=kc32, v_c=vc32,
                           logf=misc[:, MISC_LOGF:MISC_LOGF + C_HEADS], v_a=va))

    def stack(rows, key, shape):
        return jnp.stack([r[key] for r in rows]).reshape((depth,) + shape)

    y_prompt = hp.reshape(nb, t, d)
    y_sample = hs.reshape(ns, nt, d)
    outs = [y_prompt, y_sample]
    for rows, lead in ((rows_p, (nb, t)), (rows_s, (ns, nt))):
        outs += [stack(rows, "k_b", lead + (B_KV_HEADS, HEAD_DIM)), stack(rows, "v_b", lead + (B_KV_HEADS, HEAD_DIM)),
                 stack(rows, "k_i", lead + (HEAD_DIM,)), stack(rows, "k_c", lead + (C_HEADS, HEAD_DIM)),
                 stack(rows, "v_c", lead + (C_HEADS, HEAD_DIM)), stack(rows, "logf", lead + (C_HEADS,))]
    outs.append(stack(rows_s, "v_a", (ns, nt, 512)))
    return tuple(outs)
```

```python
import functools

import numpy as np
import jax
import jax.numpy as jnp
from jax import lax
from jax.experimental import pallas as pl
from jax.experimental.pallas import tpu as pltpu

F32 = jnp.float32
BF16 = jnp.bfloat16
I32 = jnp.int32

LANES = 128
HEAD_DIM = 64
N_BRANCH = 3
A_GROUPS = 4
B_HEADS = 8
B_KV_HEADS = 2
IDX_HEADS = 4
C_HEADS = 8
TOPK_MAX = 256
ROT_FRAC = 4
ROPE_THETA = 500000.0
EPS = 1e-6
LOG2E = 1.4426950408889634
QK_SCALE = HEAD_DIM ** -0.5 * LOG2E
IDX_SCALE = HEAD_DIM ** -0.5
NEG = -1e30
INT_MIN = -2 ** 31
VMEM_LIMIT = 56 * 1024 * 1024

_NT = (((1,), (1,)), ((), ()))


def _cparams(*sem):
    return pltpu.CompilerParams(dimension_semantics=sem, vmem_limit_bytes=VMEM_LIMIT)


def _dot(a, b, precision=None):
    return jnp.dot(a, b, preferred_element_type=F32, precision=precision)


def _dot_nt(a, b, precision=None):
    return lax.dot_general(a, b, _NT, preferred_element_type=F32, precision=precision)


def _rms(x, g):
    return x * lax.rsqrt(jnp.mean(x * x, axis=-1, keepdims=True) + EPS) * g


def _ffn_body(x_ref, gn_ref, wig_ref, wiu_ref, wo_ref, gf_ref, o_ref, xn_ref, acc_ref, *, nf, final_norm):
    f = pl.program_id(1)

    @pl.when(f == 0)
    def _():
        xn_ref[...] = _rms(x_ref[...], gn_ref[...]).astype(BF16)
        acc_ref[...] = jnp.zeros_like(acc_ref)

    xn = xn_ref[...]
    g = _dot(xn, wig_ref[...])
    u = _dot(xn, wiu_ref[...])
    h = (jax.nn.silu(g) * u).astype(BF16)
    acc_ref[...] += _dot(h, wo_ref[...])

    @pl.when(f == nf - 1)
    def _():
        y = x_ref[...] + 0.5 * acc_ref[...]
        if final_norm:
            y = _rms(y, gf_ref[...])
        o_ref[...] = y


def _ffn(x, gn, wi16, wo16, gf, *, final_norm, tm, tf):
    r, d = x.shape
    dff = wo16.shape[0]
    nf = dff // tf
    return pl.pallas_call(
        functools.partial(_ffn_body, nf=nf, final_norm=final_norm),
        grid=(r // tm, nf),
        in_specs=[
            pl.BlockSpec((tm, d), lambda i, f: (i, 0)),
            pl.BlockSpec((1, d), lambda i, f: (0, 0)),
            pl.BlockSpec((d, tf), lambda i, f: (0, f)),
            pl.BlockSpec((d, tf), lambda i, f: (0, nf + f)),
            pl.BlockSpec((tf, d), lambda i, f: (f, 0)),
            pl.BlockSpec((1, d), lambda i, f: (0, 0)),
        ],
        out_specs=pl.BlockSpec((tm, d), lambda i, f: (i, 0)),
        out_shape=jax.ShapeDtypeStruct((r, d), F32),
        scratch_shapes=[pltpu.VMEM((tm, d), BF16), pltpu.VMEM((tm, d), F32)],
        compiler_params=_cparams("parallel", "arbitrary"),
        name="ffn",
    )(x, gn, wi16, wi16, wo16, gf)


C_A = 0
C_BQ = 1024
C_BKV = 2048
C_IQ = 2304
C_KI = 2560
C_MISC = 2688
C_CQ = 2816
C_CK = 3840
C_CV = 4352
C_END = 4864
MISC_LOGF = 0
MISC_WI = 8


def _mixin_body(x_ref, gn_ref, w_ref, lng_ref, lnb_ref, ws_ref, ab_ref, bf_ref, rc_ref, rs_ref,
                oa_ref, va_ref, qb_ref, kvb32_ref, kvb16_ref, qi_ref, ki32_ref, ki16_ref, misc_ref,
                qc_ref, kc32_ref, vc32_ref, kc16_ref, vc16_ref, vbt_ref, *, tm):
    xn = _rms(x_ref[...], gn_ref[...]).astype(BF16)
    rc = rc_ref[...]
    rs = rs_ref[...]
    lane = lax.broadcasted_iota(I32, (tm, LANES), 1)
    first_half = (lane % HEAD_DIM) < (HEAD_DIM // ROT_FRAC // 2)

    def rope(v):
        sw = jnp.where(first_half, pltpu.roll(v, LANES - 8, 1), pltpu.roll(v, 8, 1))
        return v * rc + sw * rs

    def proj(a, b):
        return _dot(xn, w_ref[:, a:b])

    ga = jax.nn.gelu(proj(C_A, C_A + 1024))
    u = ga[:, :512]
    v = ga[:, 512:]
    mu = jnp.mean(v, axis=-1, keepdims=True)
    vc = v - mu
    var = jnp.mean(vc * vc, axis=-1, keepdims=True)
    vn = vc * lax.rsqrt(var + EPS) * lng_ref[...] + lnb_ref[...]
    va_ref[...] = vn
    vn16 = vn.astype(BF16)
    for c in range(tm // LANES):
        rows = slice(c * LANES, (c + 1) * LANES)
        for g in range(A_GROUPS):
            cols = slice(g * LANES, (g + 1) * LANES)
            mixed = _dot(ws_ref[g], vn16[rows, cols]) + ab_ref[:, cols]
            oa_ref[rows, cols] = (u[rows, cols] * mixed).astype(BF16)

    for s in range(B_HEADS):
        cols = slice(s * LANES, (s + 1) * LANES)
        qb_ref[:, cols] = (rope(proj(C_BQ + s * LANES, C_BQ + (s + 1) * LANES)) * QK_SCALE).astype(BF16)
    hk = rope(proj(C_BKV, C_BKV + 128))
    hv = proj(C_BKV + 128, C_BKV + 256)
    kvb32_ref[:, :128] = hk
    kvb32_ref[:, 128:] = hv
    kvb16_ref[:, :128] = hk.astype(BF16)
    kvb16_ref[:, 128:] = hv.astype(BF16)
    for c in range(tm // LANES):
        vbt_ref[:, c * LANES:(c + 1) * LANES] = hv[c * LANES:(c + 1) * LANES, :].T.astype(BF16)
    for s in range(2):
        cols = slice(s * LANES, (s + 1) * LANES)
        qi_ref[:, cols] = (rope(proj(C_IQ + s * LANES, C_IQ + (s + 1) * LANES)) * IDX_SCALE).astype(BF16)
    hki = rope(proj(C_KI, C_KI + 128))
    ki32_ref[...] = hki
    ki16_ref[...] = hki.astype(BF16)
    hm = proj(C_MISC, C_MISC + 128)
    lf = jax.nn.log_sigmoid(hm + bf_ref[...])
    misc_ref[...] = jnp.where(lane < MISC_WI, lf, jnp.where(lane < MISC_WI + IDX_HEADS, hm * 0.5, 0.0))

    for s in range(C_HEADS):
        cols = slice(s * LANES, (s + 1) * LANES)
        qc_ref[:, cols] = (proj(C_CQ + s * LANES, C_CQ + (s + 1) * LANES) * QK_SCALE).astype(BF16)
    hck = proj(C_CK, C_CK + 512)
    kc32_ref[...] = hck
    kc16_ref[...] = hck.astype(BF16)
    hcv = proj(C_CV, C_CV + 512)
    vc32_ref[...] = hcv
    vc16_ref[...] = hcv.astype(BF16)


def _mixin(x, gn, w2, lng, lnb, ws16, abias, bfg, rc, rs, *, tm):
    r, d = x.shape
    npos = rc.shape[0] // tm
    row = lambda i: (i, 0)
    const2 = lambda i: (0, 0)
    widths = [(512, BF16), (512, F32), (1024, BF16), (256, F32), (256, BF16), (256, BF16), (128, F32), (128, BF16),
              (128, F32), (1024, BF16), (512, F32), (512, F32), (512, BF16), (512, BF16)]
    return pl.pallas_call(
        functools.partial(_mixin_body, tm=tm),
        grid=(r // tm,),
        in_specs=[
            pl.BlockSpec((tm, d), row),
            pl.BlockSpec((1, d), const2),
            pl.BlockSpec((d, C_END), const2),
            pl.BlockSpec((1, 512), const2),
            pl.BlockSpec((1, 512), const2),
            pl.BlockSpec((A_GROUPS, LANES, LANES), lambda i: (0, 0, 0)),
            pl.BlockSpec((LANES, 512), const2),
            pl.BlockSpec((1, LANES), const2),
            pl.BlockSpec((tm, LANES), lambda i: (i % npos, 0)),
            pl.BlockSpec((tm, LANES), lambda i: (i % npos, 0)),
        ],
        out_specs=[pl.BlockSpec((tm, w), row) for w, _ in widths] + [pl.BlockSpec((LANES, tm), lambda i: (0, i))],
        out_shape=[jax.ShapeDtypeStruct((r, w), dt) for w, dt in widths] + [jax.ShapeDtypeStruct((LANES, r), BF16)],
        compiler_params=_cparams("parallel"),
        name="mix_in",
    )(x, gn, w2, lng, lnb, ws16, abias, bfg, rc, rs)


def _cum_body(misc_ref, o_ref, *, t):
    r = lax.broadcasted_iota(I32, (LANES, LANES), 0)
    c = lax.broadcasted_iota(I32, (LANES, LANES), 1)
    tri = jnp.where(c <= r, 1.0, 0.0).astype(F32)
    carry = jnp.zeros((1, LANES), F32)
    for ch in range(t // LANES):
        xs = misc_ref[ch * LANES:(ch + 1) * LANES, :]
        cum = _dot(tri, xs, precision=lax.Precision.HIGHEST) + carry
        carry = cum[LANES - 1:LANES, :]
        o_ref[0, :, ch * LANES:(ch + 1) * LANES] = cum.T[0:C_HEADS, :]


def _cum_t(misc, nb, t):
    return pl.pallas_call(
        functools.partial(_cum_body, t=t),
        grid=(nb,),
        in_specs=[pl.BlockSpec((t, LANES), lambda b: (b, 0))],
        out_specs=pl.BlockSpec((1, C_HEADS, t), lambda b: (b, 0, 0)),
        out_shape=jax.ShapeDtypeStruct((nb, C_HEADS, t), F32),
        compiler_params=_cparams("parallel"),
        name="cum_logf",
    )(misc)


def _fox_body(q_ref, k_ref, v_ref, cum_ref, o_ref, *, tq, tk):
    i = pl.program_id(1)
    lo = lax.broadcasted_iota(I32, (tq, LANES), 1) < HEAD_DIM
    t0 = pl.multiple_of(i * tq, tq)
    nfull = t0 // tk
    qpos = t0 + lax.broadcasted_iota(I32, (2 * tq, tk), 0) % tq
    col = lax.broadcasted_iota(I32, (2 * tq, tk), 1)
    c0 = [cum_ref[0, h:h + 1, pl.ds(t0, LANES)][:, 0:1] for h in range(C_HEADS)]

    def step(j, carry, masked):
        s0 = pl.multiple_of(j * tk, tk)
        out = []
        for p in range(C_HEADS // 2):
            cols = slice(p * LANES, (p + 1) * LANES)
            m, l, acc = carry[p]
            qp = jnp.concatenate([q_ref[:, 2 * p * LANES:(2 * p + 1) * LANES],
                                  q_ref[:, (2 * p + 1) * LANES:(2 * p + 2) * LANES]], axis=0)
            s = _dot_nt(qp, k_ref[pl.ds(s0, tk), cols])
            ba = (c0[2 * p] - cum_ref[0, 2 * p:2 * p + 1, pl.ds(s0, tk)]) * LOG2E
            bb = (c0[2 * p + 1] - cum_ref[0, 2 * p + 1:2 * p + 2, pl.ds(s0, tk)]) * LOG2E
            s = jnp.concatenate([s[:tq] + ba, s[tq:] + bb], axis=0)
            if masked:
                s = jnp.where(s0 + col <= qpos, s, NEG)
            m_new = jnp.maximum(m, jnp.max(s, axis=-1, keepdims=True))
            alpha = jnp.exp2(m - m_new)
            pr = jnp.exp2(s - m_new)
            l = alpha * l + jnp.sum(pr, axis=-1, keepdims=True)
            acc = alpha * acc + _dot(pr.astype(BF16), v_ref[pl.ds(s0, tk), cols])
            out.append((m_new, l, acc))
        return tuple(out)

    init = tuple((jnp.full((2 * tq, 1), NEG, F32), jnp.zeros((2 * tq, 1), F32), jnp.zeros((2 * tq, LANES), F32))
                 for _ in range(C_HEADS // 2))
    carry = lax.fori_loop(0, nfull, functools.partial(step, masked=False), init)
    carry = step(nfull, carry, True)
    for p in range(C_HEADS // 2):
        _, l, acc = carry[p]
        out = acc / l
        o_ref[:, p * LANES:(p + 1) * LANES] = jnp.where(lo, out[:tq], out[tq:]).astype(BF16)


def _fox_prompt(qc16, kc16, vc16, cum_t, *, nb, t, tq, tk):
    nq = t // tq
    w = kc16.shape[1]
    return pl.pallas_call(
        functools.partial(_fox_body, tq=tq, tk=tk),
        grid=(nb, nq),
        in_specs=[
            pl.BlockSpec((tq, C_HEADS * LANES), lambda b, i: (b * nq + i, 0)),
            pl.BlockSpec((t, w), lambda b, i: (b, 0)),
            pl.BlockSpec((t, w), lambda b, i: (b, 0)),
            pl.BlockSpec((1, C_HEADS, t), lambda b, i: (b, 0, 0)),
        ],
        out_specs=pl.BlockSpec((tq, w), lambda b, i: (b * nq + i, 0)),
        out_shape=jax.ShapeDtypeStruct(kc16.shape, BF16),
        compiler_params=_cparams("parallel", "arbitrary"),
        name="fox_prompt",
    )(qc16, kc16, vc16, cum_t)


TOPK_ROWS = 128
TOPK_CH = 512
COUNT_ACC = 32
IDX_BITS = 14
F32_MIN_NORMAL_BITS = 0x00800000


def _pattern_value(key):
    bits = jnp.where(key >= 0, key, key ^ jnp.int32(0x7FFFFFFF))
    bits = jnp.where((bits > 0) & (bits < F32_MIN_NORMAL_BITS), F32_MIN_NORMAL_BITS, bits)
    return pltpu.bitcast(bits, F32)


def _topk_select(s_ref, bias_ref, *, key_axis, nq, nch, ch, k, n_total_ch):
    kf = float(k)
    q_shape = (1, nq) if key_axis == 0 else (nq, 1)
    blk = (ch, nq) if key_axis == 0 else (nq, ch)
    kidx = lax.broadcasted_iota(I32, blk, key_axis)

    def window(c):
        ds = pl.ds(pl.multiple_of(c * ch, ch), ch)
        return (ds, slice(None)) if key_axis == 0 else (slice(None), ds)

    def count(pred):
        def body(c, part):
            hit = jnp.where(pred(s_ref[window(c)], c), 1.0, 0.0)
            if key_axis == 0:
                return part + jnp.sum(hit.reshape(ch // COUNT_ACC, COUNT_ACC, nq), axis=0)
            for b in range(ch // LANES):
                part = part + hit[:, b * LANES:(b + 1) * LANES]
            return part
        init = jnp.zeros((COUNT_ACC, nq) if key_axis == 0 else (nq, LANES), F32)
        return jnp.sum(lax.fori_loop(0, nch, body, init), axis=key_axis, keepdims=True)

    def vbit(b, key):
        cand = key + (jnp.int32(1) << (jnp.int32(31) - b))
        cv = _pattern_value(cand)
        return jnp.where(count(lambda s, c: s >= cv) >= kf, cand, key)

    thr = _pattern_value(lax.fori_loop(0, 32, vbit, jnp.full(q_shape, INT_MIN, I32)))
    take_all = count(lambda s, c: s > -jnp.inf) <= kf
    thr = jnp.where(take_all, -jnp.inf, thr)
    need = kf - count(lambda s, c: s > thr)
    ties = jnp.where(take_all, 0.0, count(lambda s, c: s == thr))

    def ibit(b, j):
        cand = j + (jnp.int32(1) << (jnp.int32(IDX_BITS - 1) - b))
        return jnp.where(count(lambda s, c: (s == thr) & (c * ch + kidx < cand)) <= need, cand, j)

    jthr = lax.cond(jnp.max(ties - need) > 0.0,
                    lambda: lax.fori_loop(0, IDX_BITS, ibit, jnp.zeros(q_shape, I32)),
                    lambda: jnp.full(q_shape, 1 << IDX_BITS, I32))

    def emit(c, _):
        s = s_ref[window(c)]
        sel = ((s > thr) | ((s == thr) & (c * ch + kidx < jthr))) & (s > -jnp.inf)
        bias_ref[window(c)] = jnp.where(sel, 0.0, NEG).astype(bias_ref.dtype)
        return 0

    lax.fori_loop(0, nch, emit, 0)

    def fill(c, _):
        bias_ref[window(c)] = jnp.full(blk, NEG, bias_ref.dtype)
        return 0

    lax.fori_loop(nch, n_total_ch, fill, 0)


def _idx_topk_body(qi_ref, misc_ref, ki_ref, bias_ref, s_ref, *, t, k):
    i = pl.program_id(1)
    rows, ch = TOPK_ROWS, TOPK_CH
    q0 = i * rows
    nch = (q0 + rows + ch - 1) // ch
    lo = lax.broadcasted_iota(I32, (rows, LANES), 1) < HEAD_DIM
    qhs = []
    for p in range(IDX_HEADS // 2):
        q2 = qi_ref[:, p * LANES:(p + 1) * LANES]
        zero = jnp.zeros_like(q2)
        qhs += [jnp.where(lo, q2, zero), jnp.where(lo, zero, q2)]
    w_t = misc_ref[...].T
    ws = [w_t[MISC_WI + h:MISC_WI + h + 1, :] for h in range(IDX_HEADS)]
    qpos = q0 + lax.broadcasted_iota(I32, (ch, LANES), 1)
    krow = lax.broadcasted_iota(I32, (ch, LANES), 0)

    def score_chunk(c, _):
        off = pl.multiple_of(c * ch, ch)
        kk = ki_ref[pl.ds(off, ch), :]
        score = jnp.zeros((ch, LANES), F32)
        for h in range(IDX_HEADS):
            score = score + ws[h] * jnp.maximum(_dot_nt(kk, qhs[h]), 0.0)
        s_ref[pl.ds(off, ch), :] = jnp.where(off + krow <= qpos, score, -jnp.inf)
        return 0

    lax.fori_loop(0, nch, score_chunk, 0)
    _topk_select(s_ref, bias_ref, key_axis=0, nq=rows, nch=nch, ch=ch, k=k, n_total_ch=t // ch)


def _idx_topk_prompt(qi16, misc, ki16, *, nb, t, k):
    nq = t // TOPK_ROWS
    row = lambda b, i: (b * nq + i, 0)
    return pl.pallas_call(
        functools.partial(_idx_topk_body, t=t, k=k),
        grid=(nb, nq),
        in_specs=[
            pl.BlockSpec((TOPK_ROWS, IDX_HEADS * HEAD_DIM), row),
            pl.BlockSpec((TOPK_ROWS, LANES), row),
            pl.BlockSpec((t, LANES), lambda b, i: (b, 0)),
        ],
        out_specs=pl.BlockSpec((None, t, TOPK_ROWS), lambda b, i: (b * nq + i, 0, 0)),
        out_shape=jax.ShapeDtypeStruct((nb * nq, t, TOPK_ROWS), BF16),
        scratch_shapes=[pltpu.VMEM((t, TOPK_ROWS), F32)],
        compiler_params=_cparams("parallel", "arbitrary"),
        name="idx_topk_prompt",
    )(qi16, misc, ki16)


def _dsa_body(q_ref, k_ref, vt_ref, bias_ref, o_ref):
    tq, ch = TOPK_ROWS, TOPK_CH
    nch = (pl.program_id(1) * tq + tq + ch - 1) // ch
    lane = lax.broadcasted_iota(I32, (tq, LANES), 1)
    lo = lane < HEAD_DIM

    q_all = jnp.concatenate([q_ref[:, h * LANES:(h + 1) * LANES] for h in range(B_HEADS)], axis=0)

    def step(c, carry):
        m, l, acc = carry
        off = pl.multiple_of(c * ch, ch)
        bias = bias_ref[pl.ds(off, ch), :].astype(F32)
        s = _dot_nt(k_ref[pl.ds(off, ch), :], q_all) + jnp.concatenate([bias] * B_HEADS, axis=1)
        m_new = jnp.maximum(m, jnp.max(s, axis=0, keepdims=True))
        alpha = jnp.exp2(m - m_new)
        pr = jnp.exp2(s - m_new)
        l = alpha * l + jnp.sum(pr, axis=0, keepdims=True)
        acc = alpha * acc + _dot(vt_ref[:, pl.ds(off, ch)], pr.astype(BF16))
        return m_new, l, acc

    cols = B_HEADS * tq
    init = (jnp.full((1, cols), 2 * NEG, F32), jnp.zeros((1, cols), F32), jnp.zeros((LANES, cols), F32))
    _, l, acc = lax.fori_loop(0, nch, step, init)
    out = acc / l
    res = [out[:, h * tq:(h + 1) * tq].T for h in range(B_HEADS)]
    for p in range(B_HEADS // 2):
        g = (2 * p) // (B_HEADS // B_KV_HEADS)
        a, b = res[2 * p], res[2 * p + 1]
        if g == 0:
            b = pltpu.roll(b, HEAD_DIM, 1)
        else:
            a = pltpu.roll(a, HEAD_DIM, 1)
        o_ref[:, p * LANES:(p + 1) * LANES] = jnp.where(lo, a, b).astype(BF16)


def _dsa_prompt(qb16, kvb16, vbt16, bias_t, *, nb, t):
    tq = TOPK_ROWS
    nq = t // tq
    row = lambda b, i: (b * nq + i, 0)
    return pl.pallas_call(
        _dsa_body,
        grid=(nb, nq),
        in_specs=[
            pl.BlockSpec((tq, B_HEADS * LANES), row),
            pl.BlockSpec((t, LANES), lambda b, i: (b, 0)),
            pl.BlockSpec((LANES, t), lambda b, i: (0, b)),
            pl.BlockSpec((None, t, tq), lambda b, i: (b * nq + i, 0, 0)),
        ],
        out_specs=pl.BlockSpec((tq, 512), row),
        out_shape=jax.ShapeDtypeStruct((nb * t, 512), BF16),
        compiler_params=_cparams("parallel", "arbitrary"),
        name="dsa_prompt",
    )(qb16, kvb16, vbt16, bias_t)


def _page_specs(n_pages, layer, block):
    def spec(j):
        return pl.BlockSpec((None, None) + block, lambda b, pt: (layer, pt[b, j]) + (0,) * len(block))
    return [spec(j) for j in range(n_pages)]


def _s_score_body(pt_ref, q_ref, w_ref, *refs, n_pages):
    pages, new_ref, o_ref = refs[:n_pages], refs[n_pages], refs[n_pages + 1]
    q = q_ref[...]
    w = w_ref[...]
    for j in range(n_pages + 1):
        s = _dot(q, pages[j][...].astype(BF16)) if j < n_pages else _dot_nt(q, new_ref[...])
        s = jnp.maximum(s, 0.0)
        score = jnp.zeros((8, LANES), F32)
        for h in range(IDX_HEADS):
            score = score + w[:, h:h + 1] * s[h * 8:(h + 1) * 8, :]
        o_ref[:, j * LANES:(j + 1) * LANES] = score[0:4, :]


def _s_score(pt, qi_s, wi_s, cache_kidx, ki_new, *, layer, ns, n_pages):
    lw = (n_pages + 1) * LANES
    return pl.pallas_call(
        functools.partial(_s_score_body, n_pages=n_pages),
        grid_spec=pltpu.PrefetchScalarGridSpec(
            num_scalar_prefetch=1, grid=(ns,),
            in_specs=[pl.BlockSpec((None, 32, HEAD_DIM), lambda b, pt: (b, 0, 0)),
                      pl.BlockSpec((None, 8, IDX_HEADS), lambda b, pt: (b, 0, 0))]
            + _page_specs(n_pages, layer, (HEAD_DIM, LANES))
            + [pl.BlockSpec((None, LANES, HEAD_DIM), lambda b, pt: (b, 0, 0))],
            out_specs=pl.BlockSpec((None, 4, lw), lambda b, pt: (b, 0, 0))),
        out_shape=jax.ShapeDtypeStruct((ns, 4, lw), F32),
        compiler_params=_cparams("arbitrary"),
        name="sample_idx_score",
    )(pt, qi_s, wi_s, *([cache_kidx] * n_pages), ki_new)


def _s_topk_body(s_ref, bias_ref, m_ref, *, past, nt, k, lw):
    rows = TOPK_ROWS
    col = lax.broadcasted_iota(I32, (rows, LANES), 1)
    tq = lax.broadcasted_iota(I32, (rows, LANES), 0) % nt
    nch = lw // LANES
    for c in range(nch):
        cols = slice(c * LANES, (c + 1) * LANES)
        s = s_ref[:, cols]
        if (c + 1) * LANES > past:
            s = jnp.where(c * LANES + col - past <= tq, s, -jnp.inf)
        m_ref[:, cols] = s
    _topk_select(m_ref, bias_ref, key_axis=1, nq=rows, nch=nch, ch=LANES, k=k, n_total_ch=nch)


def _s_topk(scores, *, past, nt, k):
    r, lw = scores.shape
    return pl.pallas_call(
        functools.partial(_s_topk_body, past=past, nt=nt, k=k, lw=lw),
        grid=(r // TOPK_ROWS,),
        in_specs=[pl.BlockSpec((TOPK_ROWS, lw), lambda i: (i, 0))],
        out_specs=pl.BlockSpec((TOPK_ROWS, lw), lambda i: (i, 0)),
        out_shape=jax.ShapeDtypeStruct((r, lw), F32),
        scratch_shapes=[pltpu.VMEM((TOPK_ROWS, lw), F32)],
        compiler_params=_cparams("parallel"),
        name="sample_topk",
    )(scores)


NT_PAD = 8


def _softmax_pv(s_past, s_new, vt_past, v_new):
    m = jnp.maximum(jnp.max(s_past, axis=-1, keepdims=True), jnp.max(s_new, axis=-1, keepdims=True))
    p_past = jnp.exp2(s_past - m)
    p_new = jnp.exp2(s_new - m)
    l = jnp.sum(p_past, axis=-1, keepdims=True) + jnp.sum(p_new, axis=-1, keepdims=True)
    return (_dot_nt(p_past.astype(BF16), vt_past) + _dot(p_new.astype(BF16), v_new)) / l


def _head_t(page_refs, h):
    return jnp.concatenate([r[h] for r in page_refs], axis=1).astype(BF16)


def _seq_page_specs(n_pages, layer, block, spb, s):
    def spec(j):
        return pl.BlockSpec((None, None) + block, lambda b, pt: (layer, pt[b * spb + s, j]) + (0,) * len(block))
    return [spec(j) for j in range(n_pages)]


DSA_SEQS_PER_STEP = 4
FOX_SEQS_PER_STEP = 1


def _s_dsa_body(pt_ref, q_ref, bias_ref, *refs, n_pages, spb):
    pages = refs[:2 * n_pages * spb]
    knew, vnew, o_ref = refs[2 * n_pages * spb:]
    past = n_pages * LANES
    hg = B_HEADS // B_KV_HEADS
    for s in range(spb):
        kps = pages[2 * n_pages * s:2 * n_pages * s + n_pages]
        vps = pages[2 * n_pages * s + n_pages:2 * n_pages * (s + 1)]
        bias = bias_ref[s]
        b_past = jnp.concatenate([bias[:, :past]] * hg, axis=0)
        b_new = jnp.concatenate([bias[:, past:past + NT_PAD]] * hg, axis=0)
        for g in range(B_KV_HEADS):
            q = q_ref[s, g]
            s_past = _dot(q, _head_t(kps, g)) + b_past
            s_new = _dot_nt(q, knew[s, :, g, :].astype(BF16)) + b_new
            o_ref[s, g] = _softmax_pv(s_past, s_new, _head_t(vps, g), vnew[s, :, g, :].astype(BF16))


def _s_dsa(pt, qb_s, bias_s, cache_bk, cache_bv, k_new, v_new, *, layer, ns, n_pages):
    lw = (n_pages + 1) * LANES
    hg = B_HEADS // B_KV_HEADS
    spb = DSA_SEQS_PER_STEP if ns % DSA_SEQS_PER_STEP == 0 else 1
    seq4 = lambda b, pt: (b, 0, 0, 0)
    page = (B_KV_HEADS, HEAD_DIM, LANES)
    page_specs, page_args = [], []
    for s in range(spb):
        page_specs += _seq_page_specs(n_pages, layer, page, spb, s) * 2
        page_args += [cache_bk] * n_pages + [cache_bv] * n_pages
    new = pl.BlockSpec((spb, NT_PAD, B_KV_HEADS, HEAD_DIM), seq4)
    return pl.pallas_call(
        functools.partial(_s_dsa_body, n_pages=n_pages, spb=spb),
        grid_spec=pltpu.PrefetchScalarGridSpec(
            num_scalar_prefetch=1, grid=(ns // spb,),
            in_specs=[pl.BlockSpec((spb, B_KV_HEADS, hg * NT_PAD, HEAD_DIM), seq4),
                      pl.BlockSpec((spb, NT_PAD, lw), lambda b, pt: (b, 0, 0))]
            + page_specs + [new, new],
            out_specs=pl.BlockSpec((spb, B_KV_HEADS, hg * NT_PAD, HEAD_DIM), seq4)),
        out_shape=jax.ShapeDtypeStruct((ns, B_KV_HEADS, hg * NT_PAD, HEAD_DIM), F32),
        compiler_params=_cparams("arbitrary"),
        name="sample_dsa",
    )(pt, qb_s, bias_s, *page_args, k_new, v_new)


def _s_fox_body(pt_ref, q_ref, *refs, n_pages, spb):
    pages = refs[:3 * n_pages * spb]
    knew, vnew, fnew, o_ref = refs[3 * n_pages * spb:]
    hi = lax.Precision.HIGHEST
    r = lax.broadcasted_iota(I32, (LANES, LANES), 0)
    c = lax.broadcasted_iota(I32, (LANES, LANES), 1)
    triu = jnp.where(r <= c, 1.0, 0.0).astype(F32)
    triu_new = triu[:NT_PAD, :NT_PAD]
    causal_new = c[:NT_PAD, :NT_PAD] <= r[:NT_PAD, :NT_PAD]
    for s in range(spb):
        base = 3 * n_pages * s
        kps, vps, fps = (pages[base + i * n_pages:base + (i + 1) * n_pages] for i in range(3))
        off = jnp.zeros((C_HEADS, 1), F32)
        cums = []
        for j in range(n_pages):
            cum = _dot(fps[j][...], triu, precision=hi)
            cums.append(cum + off)
            off = off + cum[:, LANES - 1:LANES]
        cum_past = jnp.concatenate(cums, axis=1) * LOG2E
        cum_new = (_dot(fnew[s], triu_new, precision=hi) + off) * LOG2E
        for h in range(C_HEADS):
            q = q_ref[s, h]
            s_past = _dot(q, _head_t(kps, h)) - cum_past[h:h + 1, :]
            s_new = jnp.where(causal_new, _dot_nt(q, knew[s, :, h, :].astype(BF16)) - cum_new[h:h + 1, :], NEG)
            o_ref[s, h] = _softmax_pv(s_past, s_new, _head_t(vps, h), vnew[s, :, h, :].astype(BF16))


def _s_fox(pt, qc_s, cache_ck, cache_cv, cache_lf, k_new, v_new, lf_new, *, layer, ns, n_pages):
    spb = FOX_SEQS_PER_STEP if ns % FOX_SEQS_PER_STEP == 0 else 1
    seq4 = lambda b, pt: (b, 0, 0, 0)
    page = (C_HEADS, HEAD_DIM, LANES)
    page_specs, page_args = [], []
    for s in range(spb):
        page_specs += _seq_page_specs(n_pages, layer, page, spb, s) * 2
        page_specs += _seq_page_specs(n_pages, layer, (C_HEADS, LANES), spb, s)
        page_args += [cache_ck] * n_pages + [cache_cv] * n_pages + [cache_lf] * n_pages
    new = pl.BlockSpec((spb, NT_PAD, C_HEADS, HEAD_DIM), seq4)
    return pl.pallas_call(
        functools.partial(_s_fox_body, n_pages=n_pages, spb=spb),
        grid_spec=pltpu.PrefetchScalarGridSpec(
            num_scalar_prefetch=1, grid=(ns // spb,),
            in_specs=[pl.BlockSpec((spb, C_HEADS, NT_PAD, HEAD_DIM), seq4)] + page_specs
            + [new, new, pl.BlockSpec((spb, C_HEADS, NT_PAD), lambda b, pt: (b, 0, 0))],
            out_specs=pl.BlockSpec((spb, C_HEADS, NT_PAD, HEAD_DIM), seq4)),
        out_shape=jax.ShapeDtypeStruct((ns, C_HEADS, NT_PAD, HEAD_DIM), F32),
        compiler_params=_cparams("arbitrary"),
        name="sample_fox",
    )(pt, qc_s, *page_args, k_new, v_new, lf_new)


def _mixout_body(x_ref, gn_ref, wg_ref, wb_ref, wo_ref, oa_ref, ob_ref, oc_ref, o_ref):
    x = x_ref[...]
    d = x.shape[1]
    xn = _rms(x, gn_ref[...]).astype(BF16)
    merged = jnp.zeros(x.shape, F32)
    for n, br in enumerate((oa_ref, ob_ref, oc_ref)):
        gate = jax.nn.sigmoid(_dot(xn, wg_ref[:, n * d:(n + 1) * d]))
        merged = merged + _dot(br[...], wb_ref[n]) * gate
    o_ref[...] = x + _dot(merged.astype(BF16), wo_ref[...])


def _mixout(x, gn, wg16, wb16, wo16, oa, ob, oc, *, tm):
    r, d = x.shape
    wbr = oa.shape[1]
    row = lambda i: (i, 0)
    const2 = lambda i: (0, 0)
    return pl.pallas_call(
        _mixout_body,
        grid=(r // tm,),
        in_specs=[
            pl.BlockSpec((tm, d), row),
            pl.BlockSpec((1, d), const2),
            pl.BlockSpec((d, N_BRANCH * d), const2),
            pl.BlockSpec((N_BRANCH, wbr, d), lambda i: (0, 0, 0)),
            pl.BlockSpec((d, d), const2),
            pl.BlockSpec((tm, wbr), row),
            pl.BlockSpec((tm, wbr), row),
            pl.BlockSpec((tm, wbr), row),
        ],
        out_specs=pl.BlockSpec((tm, d), row),
        out_shape=jax.ShapeDtypeStruct((r, d), F32),
        compiler_params=_cparams("parallel"),
        name="mix_out",
    )(x, gn, wg16, wb16, wo16, oa, ob, oc)


def _rope_tables(pos):
    rot = HEAD_DIM // ROT_FRAC
    half = rot // 2
    inv = ROPE_THETA ** (-jnp.arange(half, dtype=F32) / half)
    ang = pos.astype(F32)[:, None] * inv[None, :]
    cos, sin = jnp.cos(ang), jnp.sin(ang)
    ones = jnp.ones((pos.shape[0], HEAD_DIM - rot), F32)
    c64 = jnp.concatenate([cos, cos, ones], axis=1)
    s64 = jnp.concatenate([-sin, sin, 0.0 * ones], axis=1)
    return jnp.tile(c64, (1, LANES // HEAD_DIM)), jnp.tile(s64, (1, LANES // HEAD_DIM))


def _layer_weights(l, norm_ffn1, ffn1_wi, ffn1_wo, norm_mix, w_in, b_forget, a_ln_g, a_ln_b, a_ws, a_bs,
                   w_branch, w_out, norm_ffn2, ffn2_wi, ffn2_wo, nt):
    d = w_in.shape[1]
    w = w_in[l]
    widths = (N_BRANCH * d, d, 512, 128, 128, 256, 64, 4, 512, 512, 512, 8)
    cuts = np.concatenate([[0], np.cumsum(widths)])
    (w_gate, w_a, w_bq, w_bk, w_bv, w_iq, w_ik, w_iw, w_cq, w_ck, w_cv, w_cf) = [
        w[:, int(cuts[i]):int(cuts[i + 1])] for i in range(len(widths))]
    slots = []
    for h in range(B_HEADS):
        g = h // (B_HEADS // B_KV_HEADS)
        wh = w_bq[:, h * HEAD_DIM:(h + 1) * HEAD_DIM]
        slots.append(jnp.pad(wh, ((0, 0), (g * HEAD_DIM, (B_KV_HEADS - 1 - g) * HEAD_DIM))))
    cslots = []
    for h in range(C_HEADS):
        wh = w_cq[:, h * HEAD_DIM:(h + 1) * HEAD_DIM]
        cslots.append(jnp.pad(wh, ((0, 0), ((h % 2) * HEAD_DIM, (1 - h % 2) * HEAD_DIM))))
    w_misc = jnp.pad(jnp.concatenate([w_cf, w_iw], axis=1), ((0, 0), (0, LANES - 12)))
    w2 = jnp.concatenate([w_a] + slots + [w_bk, w_bv, w_iq, w_ik, w_ik, w_misc] + cslots + [w_ck, w_cv], axis=1)
    ws = a_ws[l]
    tril = jnp.tril(jnp.ones((LANES, LANES), bool))
    ws_p = jnp.where(tril, ws, 0.0)
    corner = jnp.where(tril[:nt, :nt], ws[:, :nt, :nt], 0.0)
    ws_s = jnp.einsum("ij,gts->gitjs", jnp.eye(LANES // nt, dtype=F32), corner).reshape(A_GROUPS, LANES, LANES)
    bs = a_bs[l]
    ab_p = jnp.repeat(bs.T, LANES, axis=1)
    ab_s = jnp.repeat(jnp.tile(bs[:, :nt].T, (LANES // nt, 1)), LANES, axis=1)
    return dict(
        n1=norm_ffn1[l][None], wi1=ffn1_wi[l].astype(BF16), wo1=ffn1_wo[l].astype(BF16),
        n2=norm_ffn2[l][None], wi2=ffn2_wi[l].astype(BF16), wo2=ffn2_wo[l].astype(BF16),
        nm=norm_mix[l][None], w2=w2.astype(BF16), wg=w_gate.astype(BF16),
        wb=w_branch[l].astype(BF16), wo=w_out[l].astype(BF16),
        lng=a_ln_g[l][None], lnb=a_ln_b[l][None],
        ws_p=ws_p.astype(BF16), ws_s=ws_s.astype(BF16), ab_p=ab_p, ab_s=ab_s,
        bf=jnp.pad(b_forget[l], (0, LANES - C_HEADS))[None],
    )


def _pad_rows(a, n):
    return jnp.pad(a, ((0, 0), (0, n - a.shape[1]), (0, 0)))


def _slot_heads(x, ns, nt, lane_half):
    x = x.reshape(ns, nt, -1, LANES)
    heads = [x[:, :, h, lane_half(h) * HEAD_DIM:(lane_half(h) + 1) * HEAD_DIM] for h in range(x.shape[2])]
    return jnp.pad(jnp.stack(heads, axis=1), ((0, 0), (0, 0), (0, NT_PAD - nt), (0, 0)))


def _new_rows(x, ns, nt):
    x = x.reshape(ns, nt, -1, HEAD_DIM)
    return jnp.pad(x, ((0, 0), (0, NT_PAD - nt), (0, 0), (0, 0)))


def _unpad_heads(o, nt):
    ns, nh = o.shape[:2]
    return o[:, :, :nt].transpose(0, 2, 1, 3).reshape(ns * nt, nh * HEAD_DIM).astype(BF16)


def kernel(x_prompt, x_sample, cache_b_k, cache_b_v, cache_b_kidx, cache_c_k, cache_c_v, cache_c_logf, page_table,
           norm_ffn1, ffn1_wi, ffn1_wo, norm_mix, w_in, b_forget, a_ln_g, a_ln_b, a_ws, a_bs, w_branch, w_out,
           norm_ffn2, ffn2_wi, ffn2_wo, norm_final):
    nb, t, d = x_prompt.shape
    ns, nt, _ = x_sample.shape
    depth = w_in.shape[0]
    n_pages = page_table.shape[1]
    page = cache_b_k.shape[2]
    past = n_pages * page
    assert page == LANES and d == 1024 and t % 512 == 0 and (ns * nt) % TOPK_ROWS == 0 and LANES % nt == 0
    assert nt <= NT_PAD
    k_prompt = min(TOPK_MAX, t // 4)
    k_sample = min(TOPK_MAX, (past + nt) // 4)
    rs_rows = ns * nt
    tm_s = min(256, rs_rows)

    pos_p = jnp.arange(t, dtype=I32)
    pos_s = past + (jnp.arange(rs_rows, dtype=I32) % nt)
    rc_p, rs_p = _rope_tables(pos_p)
    rc_s, rs_s = _rope_tables(pos_s)
    gfin = norm_final[None]

    cbk = cache_b_k.transpose(0, 1, 3, 4, 2)
    cbv = cache_b_v.transpose(0, 1, 3, 4, 2)
    cki = cache_b_kidx.transpose(0, 1, 3, 2)
    cck = cache_c_k.transpose(0, 1, 3, 4, 2)
    ccv = cache_c_v.transpose(0, 1, 3, 4, 2)
    clf = cache_c_logf.transpose(0, 1, 3, 2)

    hp = x_prompt.reshape(nb * t, d)
    hs = x_sample.reshape(rs_rows, d)
    rows_p, rows_s = [], []
    for l in range(depth):
        lw = _layer_weights(l, norm_ffn1, ffn1_wi, ffn1_wo, norm_mix, w_in, b_forget, a_ln_g, a_ln_b, a_ws, a_bs,
                            w_branch, w_out, norm_ffn2, ffn2_wi, ffn2_wo, nt)
        last = l == depth - 1
        hp = _ffn(hp, lw["n1"], lw["wi1"], lw["wo1"], gfin, final_norm=False, tm=512, tf=1408)
        (oa, _, qb, kvb32, kvb16, qi, ki32, ki16, misc, qc, kc32, vc32, kc16, vc16, vbt) = _mixin(
            hp, lw["nm"], lw["w2"], lw["lng"], lw["lnb"], lw["ws_p"], lw["ab_p"], lw["bf"], rc_p, rs_p, tm=256)
        cum_t = _cum_t(misc, nb, t)
        oc = _fox_prompt(qc, kc16, vc16, cum_t, nb=nb, t=t, tq=128, tk=512)
        bias = _idx_topk_prompt(qi, misc, ki16, nb=nb, t=t, k=k_prompt)
        ob = _dsa_prompt(qb, kvb16, vbt, bias, nb=nb, t=t)
        hp = _mixout(hp, lw["nm"], lw["wg"], lw["wb"], lw["wo"], oa, ob, oc, tm=256)
        hp = _ffn(hp, lw["n2"], lw["wi2"], lw["wo2"], gfin, final_norm=last, tm=512, tf=1408)
        rows_p.append(dict(k_b=kvb32[:, :128], v_b=kvb32[:, 128:], k_i=ki32[:, :HEAD_DIM], k_c=kc32, v_c=vc32,
                           logf=misc[:, MISC_LOGF:MISC_LOGF + C_HEADS]))
        hs = _ffn(hs, lw["n1"], lw["wi1"], lw["wo1"], gfin, final_norm=False, tm=tm_s, tf=1408)
        (oa, va, qb, kvb32, kvb16, qi, ki32, ki16, misc, qc, kc32, vc32, kc16, vc16, _) = _mixin(
            hs, lw["nm"], lw["w2"], lw["lng"], lw["lnb"], lw["ws_s"], lw["ab_s"], lw["bf"], rc_s, rs_s, tm=tm_s)
        qi_s = _pad_rows(qi.reshape(ns, nt, IDX_HEADS, HEAD_DIM).transpose(0, 2, 1, 3).reshape(ns * IDX_HEADS, nt, HEAD_DIM),
                         8).reshape(ns, IDX_HEADS * 8, HEAD_DIM)
        wi_s = _pad_rows(misc[:, MISC_WI:MISC_WI + IDX_HEADS].reshape(ns, nt, IDX_HEADS), 8)
        ki_new = _pad_rows(ki16[:, :HEAD_DIM].reshape(ns, nt, HEAD_DIM), page)
        scores = _s_score(page_table, qi_s, wi_s, cki, ki_new, layer=l, ns=ns, n_pages=n_pages)
        bias_s = _s_topk(scores.reshape(rs_rows, -1), past=past, nt=nt, k=k_sample).reshape(ns, nt, -1)
        hg = B_HEADS // B_KV_HEADS
        qb_s = _slot_heads(qb, ns, nt, lambda h: h // hg).reshape(ns, B_KV_HEADS, hg * NT_PAD, HEAD_DIM)
        ob_raw = _s_dsa(page_table, qb_s, _pad_rows(bias_s, NT_PAD), cbk, cbv,
                        _new_rows(kvb32[:, :128], ns, nt), _new_rows(kvb32[:, 128:], ns, nt),
                        layer=l, ns=ns, n_pages=n_pages)
        ob = _unpad_heads(ob_raw.reshape(ns, B_HEADS, NT_PAD, HEAD_DIM), nt)
        qc_s = _slot_heads(qc, ns, nt, lambda h: h % 2)
        lf_new = _pad_rows(misc[:, MISC_LOGF:MISC_LOGF + C_HEADS].reshape(ns, nt, C_HEADS), NT_PAD).transpose(0, 2, 1)
        oc_raw = _s_fox(page_table, qc_s, cck, ccv, clf,
                        _new_rows(kc32, ns, nt), _new_rows(vc32, ns, nt), lf_new, layer=l, ns=ns, n_pages=n_pages)
        oc = _unpad_heads(oc_raw, nt)
        hs = _mixout(hs, lw["nm"], lw["wg"], lw["wb"], lw["wo"], oa, ob, oc, tm=tm_s)
        hs = _ffn(hs, lw["n2"], lw["wi2"], lw["wo2"], gfin, final_norm=last, tm=tm_s, tf=1408)
        rows_s.append(dict(k_b=kvb32[:, :128], v_b=kvb32[:, 128:], k_i=ki32[:, :HEAD_DIM], k_c=kc32, v_c=vc32,
                           logf=misc[:, MISC_LOGF:MISC_LOGF + C_HEADS], v_a=va))

    def stack(rows, key, shape):
        return jnp.stack([r[key] for r in rows]).reshape((depth,) + shape)

    y_prompt = hp.reshape(nb, t, d)
    y_sample = hs.reshape(ns, nt, d)
    outs = [y_prompt, y_sample]
    for rows, lead in ((rows_p, (nb, t)), (rows_s, (ns, nt))):
        outs += [stack(rows, "k_b", lead + (B_KV_HEADS, HEAD_DIM)), stack(rows, "v_b", lead + (B_KV_HEADS, HEAD_DIM)),
                 stack(rows, "k_i", lead + (HEAD_DIM,)), stack(rows, "k_c", lead + (C_HEADS, HEAD_DIM)),
                 stack(rows, "v_c", lead + (C_HEADS, HEAD_DIM)), stack(rows, "logf", lead + (C_HEADS,))]
    outs.append(stack(rows_s, "v_a", (ns, nt, 512)))
    return tuple(outs)
```

```python
import functools

import numpy as np
import jax
import jax.numpy as jnp
from jax import lax
from jax.experimental import pallas as pl
from jax.experimental.pallas import tpu as pltpu

F32 = jnp.float32
BF16 = jnp.bfloat16
I32 = jnp.int32

LANES = 128
HEAD_DIM = 64
N_BRANCH = 3
A_GROUPS = 4
B_HEADS = 8
B_KV_HEADS = 2
IDX_HEADS = 4
C_HEADS = 8
TOPK_MAX = 256
ROT_FRAC = 4
ROPE_THETA = 500000.0
EPS = 1e-6
LOG2E = 1.4426950408889634
QK_SCALE = HEAD_DIM ** -0.5 * LOG2E
IDX_SCALE = HEAD_DIM ** -0.5
NEG = -1e30
INT_MIN = -2 ** 31
VMEM_LIMIT = 56 * 1024 * 1024

_NT = (((1,), (1,)), ((), ()))


def _cparams(*sem):
    return pltpu.CompilerParams(dimension_semantics=sem, vmem_limit_bytes=VMEM_LIMIT)


def _dot(a, b, precision=None):
    return jnp.dot(a, b, preferred_element_type=F32, precision=precision)


def _dot_nt(a, b, precision=None):
    return lax.dot_general(a, b, _NT, preferred_element_type=F32, precision=precision)


def _rms(x, g):
    return x * lax.rsqrt(jnp.mean(x * x, axis=-1, keepdims=True) + EPS) * g


def _ffn_body(x_ref, gn_ref, wig_ref, wiu_ref, wo_ref, gf_ref, o_ref, xn_ref, acc_ref, *, nf, final_norm):
    f = pl.program_id(1)

    @pl.when(f == 0)
    def _():
        xn_ref[...] = _rms(x_ref[...], gn_ref[...]).astype(BF16)
        acc_ref[...] = jnp.zeros_like(acc_ref)

    xn = xn_ref[...]
    g = _dot(xn, wig_ref[...])
    u = _dot(xn, wiu_ref[...])
    h = (jax.nn.silu(g) * u).astype(BF16)
    acc_ref[...] += _dot(h, wo_ref[...])

    @pl.when(f == nf - 1)
    def _():
        y = x_ref[...] + 0.5 * acc_ref[...]
        if final_norm:
            y = _rms(y, gf_ref[...])
        o_ref[...] = y


def _ffn(x, gn, wi16, wo16, gf, *, final_norm, tm, tf):
    r, d = x.shape
    dff = wo16.shape[0]
    nf = dff // tf
    return pl.pallas_call(
        functools.partial(_ffn_body, nf=nf, final_norm=final_norm),
        grid=(r // tm, nf),
        in_specs=[
            pl.BlockSpec((tm, d), lambda i, f: (i, 0)),
            pl.BlockSpec((1, d), lambda i, f: (0, 0)),
            pl.BlockSpec((d, tf), lambda i, f: (0, f)),
            pl.BlockSpec((d, tf), lambda i, f: (0, nf + f)),
            pl.BlockSpec((tf, d), lambda i, f: (f, 0)),
            pl.BlockSpec((1, d), lambda i, f: (0, 0)),
        ],
        out_specs=pl.BlockSpec((tm, d), lambda i, f: (i, 0)),
        out_shape=jax.ShapeDtypeStruct((r, d), F32),
        scratch_shapes=[pltpu.VMEM((tm, d), BF16), pltpu.VMEM((tm, d), F32)],
        compiler_params=_cparams("parallel", "arbitrary"),
        name="ffn",
    )(x, gn, wi16, wi16, wo16, gf)


C_A = 0
C_BQ = 1024
C_BKV = 2048
C_IQ = 2304
C_KI = 2560
C_MISC = 2688
C_CQ = 2816
C_CK = 3840
C_CV = 4352
C_END = 4864
MISC_LOGF = 0
MISC_WI = 8


def _mixin_body(x_ref, gn_ref, w_ref, lng_ref, lnb_ref, ws_ref, ab_ref, bf_ref, rc_ref, rs_ref,
                oa_ref, va_ref, qb_ref, kvb32_ref, kvb16_ref, qi_ref, ki32_ref, ki16_ref, misc_ref,
                qc_ref, kc32_ref, vc32_ref, kc16_ref, vc16_ref, vbt_ref, *, tm):
    xn = _rms(x_ref[...], gn_ref[...]).astype(BF16)
    rc = rc_ref[...]
    rs = rs_ref[...]
    lane = lax.broadcasted_iota(I32, (tm, LANES), 1)
    first_half = (lane % HEAD_DIM) < (HEAD_DIM // ROT_FRAC // 2)

    def rope(v):
        sw = jnp.where(first_half, pltpu.roll(v, LANES - 8, 1), pltpu.roll(v, 8, 1))
        return v * rc + sw * rs

    def proj(a, b):
        return _dot(xn, w_ref[:, a:b])

    ga = jax.nn.gelu(proj(C_A, C_A + 1024))
    u = ga[:, :512]
    v = ga[:, 512:]
    mu = jnp.mean(v, axis=-1, keepdims=True)
    vc = v - mu
    var = jnp.mean(vc * vc, axis=-1, keepdims=True)
    vn = vc * lax.rsqrt(var + EPS) * lng_ref[...] + lnb_ref[...]
    va_ref[...] = vn
    vn16 = vn.astype(BF16)
    for c in range(tm // LANES):
        rows = slice(c * LANES, (c + 1) * LANES)
        for g in range(A_GROUPS):
            cols = slice(g * LANES, (g + 1) * LANES)
            mixed = _dot(ws_ref[g], vn16[rows, cols]) + ab_ref[:, cols]
            oa_ref[rows, cols] = (u[rows, cols] * mixed).astype(BF16)

    for s in range(B_HEADS):
        cols = slice(s * LANES, (s + 1) * LANES)
        qb_ref[:, cols] = (rope(proj(C_BQ + s * LANES, C_BQ + (s + 1) * LANES)) * QK_SCALE).astype(BF16)
    hk = rope(proj(C_BKV, C_BKV + 128))
    hv = proj(C_BKV + 128, C_BKV + 256)
    kvb32_ref[:, :128] = hk
    kvb32_ref[:, 128:] = hv
    kvb16_ref[:, :128] = hk.astype(BF16)
    kvb16_ref[:, 128:] = hv.astype(BF16)
    for c in range(tm // LANES):
        vbt_ref[:, c * LANES:(c + 1) * LANES] = hv[c * LANES:(c + 1) * LANES, :].T.astype(BF16)
    for s in range(2):
        cols = slice(s * LANES, (s + 1) * LANES)
        qi_ref[:, cols] = (rope(proj(C_IQ + s * LANES, C_IQ + (s + 1) * LANES)) * IDX_SCALE).astype(BF16)
    hki = rope(proj(C_KI, C_KI + 128))
    ki32_ref[...] = hki
    ki16_ref[...] = hki.astype(BF16)
    hm = proj(C_MISC, C_MISC + 128)
    lf = jax.nn.log_sigmoid(hm + bf_ref[...])
    misc_ref[...] = jnp.where(lane < MISC_WI, lf, jnp.where(lane < MISC_WI + IDX_HEADS, hm * 0.5, 0.0))

    for s in range(C_HEADS):
        cols = slice(s * LANES, (s + 1) * LANES)
        qc_ref[:, cols] = (proj(C_CQ + s * LANES, C_CQ + (s + 1) * LANES) * QK_SCALE).astype(BF16)
    hck = proj(C_CK, C_CK + 512)
    kc32_ref[...] = hck
    kc16_ref[...] = hck.astype(BF16)
    hcv = proj(C_CV, C_CV + 512)
    vc32_ref[...] = hcv
    vc16_ref[...] = hcv.astype(BF16)


def _mixin(x, gn, w2, lng, lnb, ws16, abias, bfg, rc, rs, *, tm):
    r, d = x.shape
    npos = rc.shape[0] // tm
    row = lambda i: (i, 0)
    const2 = lambda i: (0, 0)
    widths = [(512, BF16), (512, F32), (1024, BF16), (256, F32), (256, BF16), (256, BF16), (128, F32), (128, BF16),
              (128, F32), (1024, BF16), (512, F32), (512, F32), (512, BF16), (512, BF16)]
    return pl.pallas_call(
        functools.partial(_mixin_body, tm=tm),
        grid=(r // tm,),
        in_specs=[
            pl.BlockSpec((tm, d), row),
            pl.BlockSpec((1, d), const2),
            pl.BlockSpec((d, C_END), const2),
            pl.BlockSpec((1, 512), const2),
            pl.BlockSpec((1, 512), const2),
            pl.BlockSpec((A_GROUPS, LANES, LANES), lambda i: (0, 0, 0)),
            pl.BlockSpec((LANES, 512), const2),
            pl.BlockSpec((1, LANES), const2),
            pl.BlockSpec((tm, LANES), lambda i: (i % npos, 0)),
            pl.BlockSpec((tm, LANES), lambda i: (i % npos, 0)),
        ],
        out_specs=[pl.BlockSpec((tm, w), row) for w, _ in widths] + [pl.BlockSpec((LANES, tm), lambda i: (0, i))],
        out_shape=[jax.ShapeDtypeStruct((r, w), dt) for w, dt in widths] + [jax.ShapeDtypeStruct((LANES, r), BF16)],
        compiler_params=_cparams("parallel"),
        name="mix_in",
    )(x, gn, w2, lng, lnb, ws16, abias, bfg, rc, rs)


def _cum_body(misc_ref, o_ref, *, t):
    r = lax.broadcasted_iota(I32, (LANES, LANES), 0)
    c = lax.broadcasted_iota(I32, (LANES, LANES), 1)
    tri = jnp.where(c <= r, 1.0, 0.0).astype(F32)
    carry = jnp.zeros((1, LANES), F32)
    for ch in range(t // LANES):
        xs = misc_ref[ch * LANES:(ch + 1) * LANES, :]
        cum = _dot(tri, xs, precision=lax.Precision.HIGHEST) + carry
        carry = cum[LANES - 1:LANES, :]
        o_ref[0, :, ch * LANES:(ch + 1) * LANES] = cum.T[0:C_HEADS, :]


def _cum_t(misc, nb, t):
    return pl.pallas_call(
        functools.partial(_cum_body, t=t),
        grid=(nb,),
        in_specs=[pl.BlockSpec((t, LANES), lambda b: (b, 0))],
        out_specs=pl.BlockSpec((1, C_HEADS, t), lambda b: (b, 0, 0)),
        out_shape=jax.ShapeDtypeStruct((nb, C_HEADS, t), F32),
        compiler_params=_cparams("parallel"),
        name="cum_logf",
    )(misc)


def _fox_body(q_ref, k_ref, v_ref, cum_ref, o_ref, *, tq, tk):
    i = pl.program_id(1)
    lo = lax.broadcasted_iota(I32, (tq, LANES), 1) < HEAD_DIM
    t0 = pl.multiple_of(i * tq, tq)
    nfull = t0 // tk
    qpos = t0 + lax.broadcasted_iota(I32, (2 * tq, tk), 0) % tq
    col = lax.broadcasted_iota(I32, (2 * tq, tk), 1)
    c0 = [cum_ref[0, h:h + 1, pl.ds(t0, LANES)][:, 0:1] for h in range(C_HEADS)]

    def step(j, carry, masked):
        s0 = pl.multiple_of(j * tk, tk)
        out = []
        for p in range(C_HEADS // 2):
            cols = slice(p * LANES, (p + 1) * LANES)
            m, l, acc = carry[p]
            qp = jnp.concatenate([q_ref[:, 2 * p * LANES:(2 * p + 1) * LANES],
                                  q_ref[:, (2 * p + 1) * LANES:(2 * p + 2) * LANES]], axis=0)
            s = _dot_nt(qp, k_ref[pl.ds(s0, tk), cols])
            ba = (c0[2 * p] - cum_ref[0, 2 * p:2 * p + 1, pl.ds(s0, tk)]) * LOG2E
            bb = (c0[2 * p + 1] - cum_ref[0, 2 * p + 1:2 * p + 2, pl.ds(s0, tk)]) * LOG2E
            s = jnp.concatenate([s[:tq] + ba, s[tq:] + bb], axis=0)
            if masked:
                s = jnp.where(s0 + col <= qpos, s, NEG)
            m_new = jnp.maximum(m, jnp.max(s, axis=-1, keepdims=True))
            alpha = jnp.exp2(m - m_new)
            pr = jnp.exp2(s - m_new)
            l = alpha * l + jnp.sum(pr, axis=-1, keepdims=True)
            acc = alpha * acc + _dot(pr.astype(BF16), v_ref[pl.ds(s0, tk), cols])
            out.append((m_new, l, acc))
        return tuple(out)

    init = tuple((jnp.full((2 * tq, 1), NEG, F32), jnp.zeros((2 * tq, 1), F32), jnp.zeros((2 * tq, LANES), F32))
                 for _ in range(C_HEADS // 2))
    carry = lax.fori_loop(0, nfull, functools.partial(step, masked=False), init)
    carry = step(nfull, carry, True)
    for p in range(C_HEADS // 2):
        _, l, acc = carry[p]
        out = acc / l
        o_ref[:, p * LANES:(p + 1) * LANES] = jnp.where(lo, out[:tq], out[tq:]).astype(BF16)


def _fox_prompt(qc16, kc16, vc16, cum_t, *, nb, t, tq, tk):
    nq = t // tq
    w = kc16.shape[1]
    return pl.pallas_call(
        functools.partial(_fox_body, tq=tq, tk=tk),
        grid=(nb, nq),
        in_specs=[
            pl.BlockSpec((tq, C_HEADS * LANES), lambda b, i: (b * nq + i, 0)),
            pl.BlockSpec((t, w), lambda b, i: (b, 0)),
            pl.BlockSpec((t, w), lambda b, i: (b, 0)),
            pl.BlockSpec((1, C_HEADS, t), lambda b, i: (b, 0, 0)),
        ],
        out_specs=pl.BlockSpec((tq, w), lambda b, i: (b * nq + i, 0)),
        out_shape=jax.ShapeDtypeStruct(kc16.shape, BF16),
        compiler_params=_cparams("parallel", "arbitrary"),
        name="fox_prompt",
    )(qc16, kc16, vc16, cum_t)


TOPK_ROWS = 128
TOPK_CH = 512
COUNT_ACC = 32
IDX_BITS = 14
F32_MIN_NORMAL_BITS = 0x00800000


def _pattern_value(key):
    bits = jnp.where(key >= 0, key, key ^ jnp.int32(0x7FFFFFFF))
    bits = jnp.where((bits > 0) & (bits < F32_MIN_NORMAL_BITS), F32_MIN_NORMAL_BITS, bits)
    return pltpu.bitcast(bits, F32)


def _chunk_loop(lo, hi, body, init):
    if isinstance(lo, int) and isinstance(hi, int):
        for c in range(lo, hi):
            init = body(c, init)
        return init
    return lax.fori_loop(lo, hi, body, init)


def _chunk_ds(c, ch):
    return pl.ds(c * ch, ch) if isinstance(c, int) else pl.ds(pl.multiple_of(c * ch, ch), ch)


def _topk_select(s_ref, bias_ref, *, key_axis, nq, nch, ch, k, n_total_ch):
    kf = float(k)
    q_shape = (1, nq) if key_axis == 0 else (nq, 1)
    blk = (ch, nq) if key_axis == 0 else (nq, ch)
    kidx = lax.broadcasted_iota(I32, blk, key_axis)

    def window(c):
        ds = _chunk_ds(c, ch)
        return (ds, slice(None)) if key_axis == 0 else (slice(None), ds)

    def count(pred):
        def body(c, part):
            hit = jnp.where(pred(s_ref[window(c)], c), 1.0, 0.0)
            if key_axis == 0:
                return part + jnp.sum(hit.reshape(ch // COUNT_ACC, COUNT_ACC, nq), axis=0)
            for b in range(ch // LANES):
                part = part + hit[:, b * LANES:(b + 1) * LANES]
            return part
        init = jnp.zeros((COUNT_ACC, nq) if key_axis == 0 else (nq, LANES), F32)
        return jnp.sum(_chunk_loop(0, nch, body, init), axis=key_axis, keepdims=True)

    def vbit(b, key):
        cand = key + (jnp.int32(1) << (jnp.int32(31) - b))
        cv = _pattern_value(cand)
        return jnp.where(count(lambda s, c: s >= cv) >= kf, cand, key)

    thr = _pattern_value(lax.fori_loop(0, 32, vbit, jnp.full(q_shape, INT_MIN, I32)))
    take_all = count(lambda s, c: s > -jnp.inf) <= kf
    thr = jnp.where(take_all, -jnp.inf, thr)
    need = kf - count(lambda s, c: s > thr)
    ties = jnp.where(take_all, 0.0, count(lambda s, c: s == thr))

    def ibit(b, j):
        cand = j + (jnp.int32(1) << (jnp.int32(IDX_BITS - 1) - b))
        return jnp.where(count(lambda s, c: (s == thr) & (c * ch + kidx < cand)) <= need, cand, j)

    jthr = lax.cond(jnp.max(ties - need) > 0.0,
                    lambda: lax.fori_loop(0, IDX_BITS, ibit, jnp.zeros(q_shape, I32)),
                    lambda: jnp.full(q_shape, 1 << IDX_BITS, I32))

    def emit(c, _):
        s = s_ref[window(c)]
        sel = ((s > thr) | ((s == thr) & (c * ch + kidx < jthr))) & (s > -jnp.inf)
        bias_ref[window(c)] = jnp.where(sel, 0.0, NEG).astype(bias_ref.dtype)
        return 0

    _chunk_loop(0, nch, emit, 0)

    def fill(c, _):
        bias_ref[window(c)] = jnp.full(blk, NEG, bias_ref.dtype)
        return 0

    _chunk_loop(nch, n_total_ch, fill, 0)


def _idx_topk_body(qi_ref, misc_ref, ki_ref, bias_ref, s_ref, *, t, k):
    i = pl.program_id(1)
    rows, ch = TOPK_ROWS, TOPK_CH
    q0 = i * rows
    nch = (q0 + rows + ch - 1) // ch
    lo = lax.broadcasted_iota(I32, (rows, LANES), 1) < HEAD_DIM
    qhs = []
    for p in range(IDX_HEADS // 2):
        q2 = qi_ref[:, p * LANES:(p + 1) * LANES]
        zero = jnp.zeros_like(q2)
        qhs += [jnp.where(lo, q2, zero), jnp.where(lo, zero, q2)]
    w_t = misc_ref[...].T
    ws = [w_t[MISC_WI + h:MISC_WI + h + 1, :] for h in range(IDX_HEADS)]
    qpos = q0 + lax.broadcasted_iota(I32, (ch, LANES), 1)
    krow = lax.broadcasted_iota(I32, (ch, LANES), 0)

    def score_chunk(c, _):
        rows_c = _chunk_ds(c, ch)
        score = jnp.zeros((ch, LANES), F32)
        for h in range(IDX_HEADS):
            score = score + ws[h] * jnp.maximum(_dot_nt(ki_ref[rows_c, :], qhs[h]), 0.0)
        s_ref[rows_c, :] = jnp.where(c * ch + krow <= qpos, score, -jnp.inf)
        return 0

    def variant(n):
        def run():
            _chunk_loop(0, n, score_chunk, 0)
            _topk_select(s_ref, bias_ref, key_axis=0, nq=rows, nch=n, ch=ch, k=k, n_total_ch=t // ch)
        return run

    lax.switch(nch - 1, [variant(n) for n in range(1, t // ch + 1)])


def _idx_topk_prompt(qi16, misc, ki16, *, nb, t, k):
    nq = t // TOPK_ROWS
    row = lambda b, i: (b * nq + i, 0)
    return pl.pallas_call(
        functools.partial(_idx_topk_body, t=t, k=k),
        grid=(nb, nq),
        in_specs=[
            pl.BlockSpec((TOPK_ROWS, IDX_HEADS * HEAD_DIM), row),
            pl.BlockSpec((TOPK_ROWS, LANES), row),
            pl.BlockSpec((t, LANES), lambda b, i: (b, 0)),
        ],
        out_specs=pl.BlockSpec((None, t, TOPK_ROWS), lambda b, i: (b * nq + i, 0, 0)),
        out_shape=jax.ShapeDtypeStruct((nb * nq, t, TOPK_ROWS), BF16),
        scratch_shapes=[pltpu.VMEM((t, TOPK_ROWS), F32)],
        compiler_params=_cparams("parallel", "arbitrary"),
        name="idx_topk_prompt",
    )(qi16, misc, ki16)


def _dsa_body(q_ref, k_ref, vt_ref, bias_ref, o_ref):
    tq, ch = TOPK_ROWS, TOPK_CH
    nch = (pl.program_id(1) * tq + tq + ch - 1) // ch
    lane = lax.broadcasted_iota(I32, (tq, LANES), 1)
    lo = lane < HEAD_DIM

    q_all = jnp.concatenate([q_ref[:, h * LANES:(h + 1) * LANES] for h in range(B_HEADS)], axis=0)

    def step(c, carry):
        m, l, acc = carry
        off = pl.multiple_of(c * ch, ch)
        bias = bias_ref[pl.ds(off, ch), :].astype(F32)
        s = _dot_nt(k_ref[pl.ds(off, ch), :], q_all) + jnp.concatenate([bias] * B_HEADS, axis=1)
        m_new = jnp.maximum(m, jnp.max(s, axis=0, keepdims=True))
        alpha = jnp.exp2(m - m_new)
        pr = jnp.exp2(s - m_new)
        l = alpha * l + jnp.sum(pr, axis=0, keepdims=True)
        acc = alpha * acc + _dot(vt_ref[:, pl.ds(off, ch)], pr.astype(BF16))
        return m_new, l, acc

    cols = B_HEADS * tq
    init = (jnp.full((1, cols), 2 * NEG, F32), jnp.zeros((1, cols), F32), jnp.zeros((LANES, cols), F32))
    _, l, acc = lax.fori_loop(0, nch, step, init)
    out = acc / l
    res = [out[:, h * tq:(h + 1) * tq].T for h in range(B_HEADS)]
    for p in range(B_HEADS // 2):
        g = (2 * p) // (B_HEADS // B_KV_HEADS)
        a, b = res[2 * p], res[2 * p + 1]
        if g == 0:
            b = pltpu.roll(b, HEAD_DIM, 1)
        else:
            a = pltpu.roll(a, HEAD_DIM, 1)
        o_ref[:, p * LANES:(p + 1) * LANES] = jnp.where(lo, a, b).astype(BF16)


def _dsa_prompt(qb16, kvb16, vbt16, bias_t, *, nb, t):
    tq = TOPK_ROWS
    nq = t // tq
    row = lambda b, i: (b * nq + i, 0)
    return pl.pallas_call(
        _dsa_body,
        grid=(nb, nq),
        in_specs=[
            pl.BlockSpec((tq, B_HEADS * LANES), row),
            pl.BlockSpec((t, LANES), lambda b, i: (b, 0)),
            pl.BlockSpec((LANES, t), lambda b, i: (0, b)),
            pl.BlockSpec((None, t, tq), lambda b, i: (b * nq + i, 0, 0)),
        ],
        out_specs=pl.BlockSpec((tq, 512), row),
        out_shape=jax.ShapeDtypeStruct((nb * t, 512), BF16),
        compiler_params=_cparams("parallel", "arbitrary"),
        name="dsa_prompt",
    )(qb16, kvb16, vbt16, bias_t)


def _page_specs(n_pages, layer, block):
    def spec(j):
        return pl.BlockSpec((None, None) + block, lambda b, pt: (layer, pt[b, j]) + (0,) * len(block))
    return [spec(j) for j in range(n_pages)]


def _s_score_body(pt_ref, q_ref, w_ref, *refs, n_pages):
    pages, new_ref, o_ref = refs[:n_pages], refs[n_pages], refs[n_pages + 1]
    q = q_ref[...]
    w = w_ref[...]
    for j in range(n_pages + 1):
        s = _dot(q, pages[j][...].astype(BF16)) if j < n_pages else _dot_nt(q, new_ref[...])
        s = jnp.maximum(s, 0.0)
        score = jnp.zeros((8, LANES), F32)
        for h in range(IDX_HEADS):
            score = score + w[:, h:h + 1] * s[h * 8:(h + 1) * 8, :]
        o_ref[:, j * LANES:(j + 1) * LANES] = score[0:4, :]


def _s_score(pt, qi_s, wi_s, cache_kidx, ki_new, *, layer, ns, n_pages):
    lw = (n_pages + 1) * LANES
    return pl.pallas_call(
        functools.partial(_s_score_body, n_pages=n_pages),
        grid_spec=pltpu.PrefetchScalarGridSpec(
            num_scalar_prefetch=1, grid=(ns,),
            in_specs=[pl.BlockSpec((None, 32, HEAD_DIM), lambda b, pt: (b, 0, 0)),
                      pl.BlockSpec((None, 8, IDX_HEADS), lambda b, pt: (b, 0, 0))]
            + _page_specs(n_pages, layer, (HEAD_DIM, LANES))
            + [pl.BlockSpec((None, LANES, HEAD_DIM), lambda b, pt: (b, 0, 0))],
            out_specs=pl.BlockSpec((None, 4, lw), lambda b, pt: (b, 0, 0))),
        out_shape=jax.ShapeDtypeStruct((ns, 4, lw), F32),
        compiler_params=_cparams("arbitrary"),
        name="sample_idx_score",
    )(pt, qi_s, wi_s, *([cache_kidx] * n_pages), ki_new)


def _s_topk_body(s_ref, bias_ref, m_ref, *, past, nt, k, lw):
    rows = TOPK_ROWS
    col = lax.broadcasted_iota(I32, (rows, LANES), 1)
    tq = lax.broadcasted_iota(I32, (rows, LANES), 0) % nt
    nch = lw // LANES
    for c in range(nch):
        cols = slice(c * LANES, (c + 1) * LANES)
        s = s_ref[:, cols]
        if (c + 1) * LANES > past:
            s = jnp.where(c * LANES + col - past <= tq, s, -jnp.inf)
        m_ref[:, cols] = s
    _topk_select(m_ref, bias_ref, key_axis=1, nq=rows, nch=nch, ch=LANES, k=k, n_total_ch=nch)


def _s_topk(scores, *, past, nt, k):
    r, lw = scores.shape
    return pl.pallas_call(
        functools.partial(_s_topk_body, past=past, nt=nt, k=k, lw=lw),
        grid=(r // TOPK_ROWS,),
        in_specs=[pl.BlockSpec((TOPK_ROWS, lw), lambda i: (i, 0))],
        out_specs=pl.BlockSpec((TOPK_ROWS, lw), lambda i: (i, 0)),
        out_shape=jax.ShapeDtypeStruct((r, lw), F32),
        scratch_shapes=[pltpu.VMEM((TOPK_ROWS, lw), F32)],
        compiler_params=_cparams("parallel"),
        name="sample_topk",
    )(scores)


NT_PAD = 8


def _softmax_pv(s_past, s_new, vt_past, v_new):
    m = jnp.maximum(jnp.max(s_past, axis=-1, keepdims=True), jnp.max(s_new, axis=-1, keepdims=True))
    p_past = jnp.exp2(s_past - m)
    p_new = jnp.exp2(s_new - m)
    l = jnp.sum(p_past, axis=-1, keepdims=True) + jnp.sum(p_new, axis=-1, keepdims=True)
    return (_dot_nt(p_past.astype(BF16), vt_past) + _dot(p_new.astype(BF16), v_new)) / l


def _head_t(page_refs, h):
    return jnp.concatenate([r[h] for r in page_refs], axis=1).astype(BF16)


def _seq_page_specs(n_pages, layer, block, spb, s):
    def spec(j):
        return pl.BlockSpec((None, None) + block, lambda b, pt: (layer, pt[b * spb + s, j]) + (0,) * len(block))
    return [spec(j) for j in range(n_pages)]


DSA_SEQS_PER_STEP = 4
FOX_SEQS_PER_STEP = 1


def _s_dsa_body(pt_ref, q_ref, bias_ref, *refs, n_pages, spb):
    pages = refs[:2 * n_pages * spb]
    knew, vnew, o_ref = refs[2 * n_pages * spb:]
    past = n_pages * LANES
    hg = B_HEADS // B_KV_HEADS
    for s in range(spb):
        kps = pages[2 * n_pages * s:2 * n_pages * s + n_pages]
        vps = pages[2 * n_pages * s + n_pages:2 * n_pages * (s + 1)]
        bias = bias_ref[s]
        b_past = jnp.concatenate([bias[:, :past]] * hg, axis=0)
        b_new = jnp.concatenate([bias[:, past:past + NT_PAD]] * hg, axis=0)
        for g in range(B_KV_HEADS):
            q = q_ref[s, g]
            s_past = _dot(q, _head_t(kps, g)) + b_past
            s_new = _dot_nt(q, knew[s, :, g, :].astype(BF16)) + b_new
            o_ref[s, g] = _softmax_pv(s_past, s_new, _head_t(vps, g), vnew[s, :, g, :].astype(BF16))


def _s_dsa(pt, qb_s, bias_s, cache_bk, cache_bv, k_new, v_new, *, layer, ns, n_pages):
    lw = (n_pages + 1) * LANES
    hg = B_HEADS // B_KV_HEADS
    spb = DSA_SEQS_PER_STEP if ns % DSA_SEQS_PER_STEP == 0 else 1
    seq4 = lambda b, pt: (b, 0, 0, 0)
    page = (B_KV_HEADS, HEAD_DIM, LANES)
    page_specs, page_args = [], []
    for s in range(spb):
        page_specs += _seq_page_specs(n_pages, layer, page, spb, s) * 2
        page_args += [cache_bk] * n_pages + [cache_bv] * n_pages
    new = pl.BlockSpec((spb, NT_PAD, B_KV_HEADS, HEAD_DIM), seq4)
    return pl.pallas_call(
        functools.partial(_s_dsa_body, n_pages=n_pages, spb=spb),
        grid_spec=pltpu.PrefetchScalarGridSpec(
            num_scalar_prefetch=1, grid=(ns // spb,),
            in_specs=[pl.BlockSpec((spb, B_KV_HEADS, hg * NT_PAD, HEAD_DIM), seq4),
                      pl.BlockSpec((spb, NT_PAD, lw), lambda b, pt: (b, 0, 0))]
            + page_specs + [new, new],
            out_specs=pl.BlockSpec((spb, B_KV_HEADS, hg * NT_PAD, HEAD_DIM), seq4)),
        out_shape=jax.ShapeDtypeStruct((ns, B_KV_HEADS, hg * NT_PAD, HEAD_DIM), F32),
        compiler_params=_cparams("arbitrary"),
        name="sample_dsa",
    )(pt, qb_s, bias_s, *page_args, k_new, v_new)


def _s_fox_body(pt_ref, q_ref, *refs, n_pages, spb):
    pages = refs[:3 * n_pages * spb]
    knew, vnew, fnew, o_ref = refs[3 * n_pages * spb:]
    hi = lax.Precision.HIGHEST
    r = lax.broadcasted_iota(I32, (LANES, LANES), 0)
    c = lax.broadcasted_iota(I32, (LANES, LANES), 1)
    triu = jnp.where(r <= c, 1.0, 0.0).astype(F32)
    triu_new = triu[:NT_PAD, :NT_PAD]
    causal_new = c[:NT_PAD, :NT_PAD] <= r[:NT_PAD, :NT_PAD]
    for s in range(spb):
        base = 3 * n_pages * s
        kps, vps, fps = (pages[base + i * n_pages:base + (i + 1) * n_pages] for i in range(3))
        off = jnp.zeros((C_HEADS, 1), F32)
        cums = []
        for j in range(n_pages):
            cum = _dot(fps[j][...], triu, precision=hi)
            cums.append(cum + off)
            off = off + cum[:, LANES - 1:LANES]
        cum_past = jnp.concatenate(cums, axis=1) * LOG2E
        cum_new = (_dot(fnew[s], triu_new, precision=hi) + off) * LOG2E
        for h in range(C_HEADS):
            q = q_ref[s, h]
            s_past = _dot(q, _head_t(kps, h)) - cum_past[h:h + 1, :]
            s_new = jnp.where(causal_new, _dot_nt(q, knew[s, :, h, :].astype(BF16)) - cum_new[h:h + 1, :], NEG)
            o_ref[s, h] = _softmax_pv(s_past, s_new, _head_t(vps, h), vnew[s, :, h, :].astype(BF16))


def _s_fox(pt, qc_s, cache_ck, cache_cv, cache_lf, k_new, v_new, lf_new, *, layer, ns, n_pages):
    spb = FOX_SEQS_PER_STEP if ns % FOX_SEQS_PER_STEP == 0 else 1
    seq4 = lambda b, pt: (b, 0, 0, 0)
    page = (C_HEADS, HEAD_DIM, LANES)
    page_specs, page_args = [], []
    for s in range(spb):
        page_specs += _seq_page_specs(n_pages, layer, page, spb, s) * 2
        page_specs += _seq_page_specs(n_pages, layer, (C_HEADS, LANES), spb, s)
        page_args += [cache_ck] * n_pages + [cache_cv] * n_pages + [cache_lf] * n_pages
    new = pl.BlockSpec((spb, NT_PAD, C_HEADS, HEAD_DIM), seq4)
    return pl.pallas_call(
        functools.partial(_s_fox_body, n_pages=n_pages, spb=spb),
        grid_spec=pltpu.PrefetchScalarGridSpec(
            num_scalar_prefetch=1, grid=(ns // spb,),
            in_specs=[pl.BlockSpec((spb, C_HEADS, NT_PAD, HEAD_DIM), seq4)] + page_specs
            + [new, new, pl.BlockSpec((spb, C_HEADS, NT_PAD), lambda b, pt: (b, 0, 0))],
            out_specs=pl.BlockSpec((spb, C_HEADS, NT_PAD, HEAD_DIM), seq4)),
        out_shape=jax.ShapeDtypeStruct((ns, C_HEADS, NT_PAD, HEAD_DIM), F32),
        compiler_params=_cparams("arbitrary"),
        name="sample_fox",
    )(pt, qc_s, *page_args, k_new, v_new, lf_new)


def _mixout_body(x_ref, gn_ref, wg_ref, wb_ref, wo_ref, oa_ref, ob_ref, oc_ref, o_ref):
    x = x_ref[...]
    d = x.shape[1]
    xn = _rms(x, gn_ref[...]).astype(BF16)
    merged = jnp.zeros(x.shape, F32)
    for n, br in enumerate((oa_ref, ob_ref, oc_ref)):
        gate = jax.nn.sigmoid(_dot(xn, wg_ref[:, n * d:(n + 1) * d]))
        merged = merged + _dot(br[...], wb_ref[n]) * gate
    o_ref[...] = x + _dot(merged.astype(BF16), wo_ref[...])


def _mixout(x, gn, wg16, wb16, wo16, oa, ob, oc, *, tm):
    r, d = x.shape
    wbr = oa.shape[1]
    row = lambda i: (i, 0)
    const2 = lambda i: (0, 0)
    return pl.pallas_call(
        _mixout_body,
        grid=(r // tm,),
        in_specs=[
            pl.BlockSpec((tm, d), row),
            pl.BlockSpec((1, d), const2),
            pl.BlockSpec((d, N_BRANCH * d), const2),
            pl.BlockSpec((N_BRANCH, wbr, d), lambda i: (0, 0, 0)),
            pl.BlockSpec((d, d), const2),
            pl.BlockSpec((tm, wbr), row),
            pl.BlockSpec((tm, wbr), row),
            pl.BlockSpec((tm, wbr), row),
        ],
        out_specs=pl.BlockSpec((tm, d), row),
        out_shape=jax.ShapeDtypeStruct((r, d), F32),
        compiler_params=_cparams("parallel"),
        name="mix_out",
    )(x, gn, wg16, wb16, wo16, oa, ob, oc)


def _rope_tables(pos):
    rot = HEAD_DIM // ROT_FRAC
    half = rot // 2
    inv = ROPE_THETA ** (-jnp.arange(half, dtype=F32) / half)
    ang = pos.astype(F32)[:, None] * inv[None, :]
    cos, sin = jnp.cos(ang), jnp.sin(ang)
    ones = jnp.ones((pos.shape[0], HEAD_DIM - rot), F32)
    c64 = jnp.concatenate([cos, cos, ones], axis=1)
    s64 = jnp.concatenate([-sin, sin, 0.0 * ones], axis=1)
    return jnp.tile(c64, (1, LANES // HEAD_DIM)), jnp.tile(s64, (1, LANES // HEAD_DIM))


def _layer_weights(l, norm_ffn1, ffn1_wi, ffn1_wo, norm_mix, w_in, b_forget, a_ln_g, a_ln_b, a_ws, a_bs,
                   w_branch, w_out, norm_ffn2, ffn2_wi, ffn2_wo, nt):
    d = w_in.shape[1]
    w = w_in[l]
    widths = (N_BRANCH * d, d, 512, 128, 128, 256, 64, 4, 512, 512, 512, 8)
    cuts = np.concatenate([[0], np.cumsum(widths)])
    (w_gate, w_a, w_bq, w_bk, w_bv, w_iq, w_ik, w_iw, w_cq, w_ck, w_cv, w_cf) = [
        w[:, int(cuts[i]):int(cuts[i + 1])] for i in range(len(widths))]
    slots = []
    for h in range(B_HEADS):
        g = h // (B_HEADS // B_KV_HEADS)
        wh = w_bq[:, h * HEAD_DIM:(h + 1) * HEAD_DIM]
        slots.append(jnp.pad(wh, ((0, 0), (g * HEAD_DIM, (B_KV_HEADS - 1 - g) * HEAD_DIM))))
    cslots = []
    for h in range(C_HEADS):
        wh = w_cq[:, h * HEAD_DIM:(h + 1) * HEAD_DIM]
        cslots.append(jnp.pad(wh, ((0, 0), ((h % 2) * HEAD_DIM, (1 - h % 2) * HEAD_DIM))))
    w_misc = jnp.pad(jnp.concatenate([w_cf, w_iw], axis=1), ((0, 0), (0, LANES - 12)))
    w2 = jnp.concatenate([w_a] + slots + [w_bk, w_bv, w_iq, w_ik, w_ik, w_misc] + cslots + [w_ck, w_cv], axis=1)
    ws = a_ws[l]
    tril = jnp.tril(jnp.ones((LANES, LANES), bool))
    ws_p = jnp.where(tril, ws, 0.0)
    corner = jnp.where(tril[:nt, :nt], ws[:, :nt, :nt], 0.0)
    ws_s = jnp.einsum("ij,gts->gitjs", jnp.eye(LANES // nt, dtype=F32), corner).reshape(A_GROUPS, LANES, LANES)
    bs = a_bs[l]
    ab_p = jnp.repeat(bs.T, LANES, axis=1)
    ab_s = jnp.repeat(jnp.tile(bs[:, :nt].T, (LANES // nt, 1)), LANES, axis=1)
    return dict(
        n1=norm_ffn1[l][None], wi1=ffn1_wi[l].astype(BF16), wo1=ffn1_wo[l].astype(BF16),
        n2=norm_ffn2[l][None], wi2=ffn2_wi[l].astype(BF16), wo2=ffn2_wo[l].astype(BF16),
        nm=norm_mix[l][None], w2=w2.astype(BF16), wg=w_gate.astype(BF16),
        wb=w_branch[l].astype(BF16), wo=w_out[l].astype(BF16),
        lng=a_ln_g[l][None], lnb=a_ln_b[l][None],
        ws_p=ws_p.astype(BF16), ws_s=ws_s.astype(BF16), ab_p=ab_p, ab_s=ab_s,
        bf=jnp.pad(b_forget[l], (0, LANES - C_HEADS))[None],
    )


def _pad_rows(a, n):
    return jnp.pad(a, ((0, 0), (0, n - a.shape[1]), (0, 0)))


def _slot_heads(x, ns, nt, lane_half):
    x = x.reshape(ns, nt, -1, LANES)
    heads = [x[:, :, h, lane_half(h) * HEAD_DIM:(lane_half(h) + 1) * HEAD_DIM] for h in range(x.shape[2])]
    return jnp.pad(jnp.stack(heads, axis=1), ((0, 0), (0, 0), (0, NT_PAD - nt), (0, 0)))


def _new_rows(x, ns, nt):
    x = x.reshape(ns, nt, -1, HEAD_DIM)
    return jnp.pad(x, ((0, 0), (0, NT_PAD - nt), (0, 0), (0, 0)))


def _unpad_heads(o, nt):
    ns, nh = o.shape[:2]
    return o[:, :, :nt].transpose(0, 2, 1, 3).reshape(ns * nt, nh * HEAD_DIM).astype(BF16)


def kernel(x_prompt, x_sample, cache_b_k, cache_b_v, cache_b_kidx, cache_c_k, cache_c_v, cache_c_logf, page_table,
           norm_ffn1, ffn1_wi, ffn1_wo, norm_mix, w_in, b_forget, a_ln_g, a_ln_b, a_ws, a_bs, w_branch, w_out,
           norm_ffn2, ffn2_wi, ffn2_wo, norm_final):
    nb, t, d = x_prompt.shape
    ns, nt, _ = x_sample.shape
    depth = w_in.shape[0]
    n_pages = page_table.shape[1]
    page = cache_b_k.shape[2]
    past = n_pages * page
    assert page == LANES and d == 1024 and t % 512 == 0 and (ns * nt) % TOPK_ROWS == 0 and LANES % nt == 0
    assert nt <= NT_PAD
    k_prompt = min(TOPK_MAX, t // 4)
    k_sample = min(TOPK_MAX, (past + nt) // 4)
    rs_rows = ns * nt
    tm_s = min(256, rs_rows)

    pos_p = jnp.arange(t, dtype=I32)
    pos_s = past + (jnp.arange(rs_rows, dtype=I32) % nt)
    rc_p, rs_p = _rope_tables(pos_p)
    rc_s, rs_s = _rope_tables(pos_s)
    gfin = norm_final[None]

    cbk = cache_b_k.transpose(0, 1, 3, 4, 2)
    cbv = cache_b_v.transpose(0, 1, 3, 4, 2)
    cki = cache_b_kidx.transpose(0, 1, 3, 2)
    cck = cache_c_k.transpose(0, 1, 3, 4, 2)
    ccv = cache_c_v.transpose(0, 1, 3, 4, 2)
    clf = cache_c_logf.transpose(0, 1, 3, 2)

    hp = x_prompt.reshape(nb * t, d)
    hs = x_sample.reshape(rs_rows, d)
    rows_p, rows_s = [], []
    for l in range(depth):
        lw = _layer_weights(l, norm_ffn1, ffn1_wi, ffn1_wo, norm_mix, w_in, b_forget, a_ln_g, a_ln_b, a_ws, a_bs,
                            w_branch, w_out, norm_ffn2, ffn2_wi, ffn2_wo, nt)
        last = l == depth - 1
        hp = _ffn(hp, lw["n1"], lw["wi1"], lw["wo1"], gfin, final_norm=False, tm=512, tf=1408)
        (oa, _, qb, kvb32, kvb16, qi, ki32, ki16, misc, qc, kc32, vc32, kc16, vc16, vbt) = _mixin(
            hp, lw["nm"], lw["w2"], lw["lng"], lw["lnb"], lw["ws_p"], lw["ab_p"], lw["bf"], rc_p, rs_p, tm=256)
        cum_t = _cum_t(misc, nb, t)
        oc = _fox_prompt(qc, kc16, vc16, cum_t, nb=nb, t=t, tq=128, tk=512)
        bias = _idx_topk_prompt(qi, misc, ki16, nb=nb, t=t, k=k_prompt)
        ob = _dsa_prompt(qb, kvb16, vbt, bias, nb=nb, t=t)
        hp = _mixout(hp, lw["nm"], lw["wg"], lw["wb"], lw["wo"], oa, ob, oc, tm=256)
        hp = _ffn(hp, lw["n2"], lw["wi2"], lw["wo2"], gfin, final_norm=last, tm=512, tf=1408)
        rows_p.append(dict(k_b=kvb32[:, :128], v_b=kvb32[:, 128:], k_i=ki32[:, :HEAD_DIM], k_c=kc32, v_c=vc32,
                           logf=misc[:, MISC_LOGF:MISC_LOGF + C_HEADS]))
        hs = _ffn(hs, lw["n1"], lw["wi1"], lw["wo1"], gfin, final_norm=False, tm=tm_s, tf=1408)
        (oa, va, qb, kvb32, kvb16, qi, ki32, ki16, misc, qc, kc32, vc32, kc16, vc16, _) = _mixin(
            hs, lw["nm"], lw["w2"], lw["lng"], lw["lnb"], lw["ws_s"], lw["ab_s"], lw["bf"], rc_s, rs_s, tm=tm_s)
        qi_s = _pad_rows(qi.reshape(ns, nt, IDX_HEADS, HEAD_DIM).transpose(0, 2, 1, 3).reshape(ns * IDX_HEADS, nt, HEAD_DIM),
                         8).reshape(ns, IDX_HEADS * 8, HEAD_DIM)
        wi_s = _pad_rows(misc[:, MISC_WI:MISC_WI + IDX_HEADS].reshape(ns, nt, IDX_HEADS), 8)
        ki_new = _pad_rows(ki16[:, :HEAD_DIM].reshape(ns, nt, HEAD_DIM), page)
        scores = _s_score(page_table, qi_s, wi_s, cki, ki_new, layer=l, ns=ns, n_pages=n_pages)
        bias_s = _s_topk(scores.reshape(rs_rows, -1), past=past, nt=nt, k=k_sample).reshape(ns, nt, -1)
        hg = B_HEADS // B_KV_HEADS
        qb_s = _slot_heads(qb, ns, nt, lambda h: h // hg).reshape(ns, B_KV_HEADS, hg * NT_PAD, HEAD_DIM)
        ob_raw = _s_dsa(page_table, qb_s, _pad_rows(bias_s, NT_PAD), cbk, cbv,
                        _new_rows(kvb32[:, :128], ns, nt), _new_rows(kvb32[:, 128:], ns, nt),
                        layer=l, ns=ns, n_pages=n_pages)
        ob = _unpad_heads(ob_raw.reshape(ns, B_HEADS, NT_PAD, HEAD_DIM), nt)
        qc_s = _slot_heads(qc, ns, nt, lambda h: h % 2)
        lf_new = _pad_rows(misc[:, MISC_LOGF:MISC_LOGF + C_HEADS].reshape(ns, nt, C_HEADS), NT_PAD).transpose(0, 2, 1)
        oc_raw = _s_fox(page_table, qc_s, cck, ccv, clf,
                        _new_rows(kc32, ns, nt), _new_rows(vc32, ns, nt), lf_new, layer=l, ns=ns, n_pages=n_pages)
        oc = _unpad_heads(oc_raw, nt)
        hs = _mixout(hs, lw["nm"], lw["wg"], lw["wb"], lw["wo"], oa, ob, oc, tm=tm_s)
        hs = _ffn(hs, lw["n2"], lw["wi2"], lw["wo2"], gfin, final_norm=last, tm=tm_s, tf=1408)
        rows_s.append(dict(k_b=kvb32[:, :128], v_b=kvb32[:, 128:], k_i=ki32[:, :HEAD_DIM], k_c=kc32, v_c=vc32,
                           logf=misc[:, MISC_LOGF:MISC_LOGF + C_HEADS], v_a=va))

    def stack(rows, key, shape):
        return jnp.stack([r[key] for r in rows]).reshape((depth,) + shape)

    y_prompt = hp.reshape(nb, t, d)
    y_sample = hs.reshape(ns, nt, d)
    outs = [y_prompt, y_sample]
    for rows, lead in ((rows_p, (nb, t)), (rows_s, (ns, nt))):
        outs += [stack(rows, "k_b", lead + (B_KV_HEADS, HEAD_DIM)), stack(rows, "v_b", lead + (B_KV_HEADS, HEAD_DIM)),
                 stack(rows, "k_i", lead + (HEAD_DIM,)), stack(rows, "k_c", lead + (C_HEADS, HEAD_DIM)),
                 stack(rows, "v_c", lead + (C_HEADS, HEAD_DIM)), stack(rows, "logf", lead + (C_HEADS,))]
    outs.append(stack(rows_s, "v_a", (ns, nt, 512)))
    return tuple(outs)
```

```python
import functools

import numpy as np
import jax
import jax.numpy as jnp
from jax import lax
from jax.experimental import pallas as pl
from jax.experimental.pallas import tpu as pltpu

F32 = jnp.float32
BF16 = jnp.bfloat16
I32 = jnp.int32

LANES = 128
HEAD_DIM = 64
N_BRANCH = 3
A_GROUPS = 4
B_HEADS = 8
B_KV_HEADS = 2
IDX_HEADS = 4
C_HEADS = 8
TOPK_MAX = 256
ROT_FRAC = 4
ROPE_THETA = 500000.0
EPS = 1e-6
LOG2E = 1.4426950408889634
QK_SCALE = HEAD_DIM ** -0.5 * LOG2E
IDX_SCALE = HEAD_DIM ** -0.5
NEG = -1e30
INT_MIN = -2 ** 31
VMEM_LIMIT = 56 * 1024 * 1024

_NT = (((1,), (1,)), ((), ()))


def _cparams(*sem):
    return pltpu.CompilerParams(dimension_semantics=sem, vmem_limit_bytes=VMEM_LIMIT)


def _dot(a, b, precision=None):
    return jnp.dot(a, b, preferred_element_type=F32, precision=precision)


def _dot_nt(a, b, precision=None):
    return lax.dot_general(a, b, _NT, preferred_element_type=F32, precision=precision)


def _rms(x, g):
    return x * lax.rsqrt(jnp.mean(x * x, axis=-1, keepdims=True) + EPS) * g


def _ffn_body(x_ref, gn_ref, wig_ref, wiu_ref, wo_ref, gf_ref, o_ref, xn_ref, acc_ref, *, nf, final_norm):
    f = pl.program_id(1)

    @pl.when(f == 0)
    def _():
        xn_ref[...] = _rms(x_ref[...], gn_ref[...]).astype(BF16)
        acc_ref[...] = jnp.zeros_like(acc_ref)

    xn = xn_ref[...]
    g = _dot(xn, wig_ref[...])
    u = _dot(xn, wiu_ref[...])
    h = (jax.nn.silu(g) * u).astype(BF16)
    acc_ref[...] += _dot(h, wo_ref[...])

    @pl.when(f == nf - 1)
    def _():
        y = x_ref[...] + 0.5 * acc_ref[...]
        if final_norm:
            y = _rms(y, gf_ref[...])
        o_ref[...] = y


def _ffn(x, gn, wi16, wo16, gf, *, final_norm, tm, tf):
    r, d = x.shape
    dff = wo16.shape[0]
    nf = dff // tf
    return pl.pallas_call(
        functools.partial(_ffn_body, nf=nf, final_norm=final_norm),
        grid=(r // tm, nf),
        in_specs=[
            pl.BlockSpec((tm, d), lambda i, f: (i, 0)),
            pl.BlockSpec((1, d), lambda i, f: (0, 0)),
            pl.BlockSpec((d, tf), lambda i, f: (0, f)),
            pl.BlockSpec((d, tf), lambda i, f: (0, nf + f)),
            pl.BlockSpec((tf, d), lambda i, f: (f, 0)),
            pl.BlockSpec((1, d), lambda i, f: (0, 0)),
        ],
        out_specs=pl.BlockSpec((tm, d), lambda i, f: (i, 0)),
        out_shape=jax.ShapeDtypeStruct((r, d), F32),
        scratch_shapes=[pltpu.VMEM((tm, d), BF16), pltpu.VMEM((tm, d), F32)],
        compiler_params=_cparams("parallel", "arbitrary"),
        name="ffn",
    )(x, gn, wi16, wi16, wo16, gf)


C_A = 0
C_BQ = 1024
C_BKV = 2048
C_IQ = 2304
C_KI = 2560
C_MISC = 2688
C_CQ = 2816
C_CK = 3840
C_CV = 4352
C_END = 4864
MISC_LOGF = 0
MISC_WI = 8


def _mixin_outputs(r, tm, nb, leaves_t):
    t = r // nb
    npb = t // tm
    row = lambda i: (i, 0)
    col = lambda i: (0, i)
    bt = lambda i: (i // npb, 0, i % npb)
    rows = lambda w, dt: ((tm, w), row, (r, w), dt)
    tr = lambda w, dt: ((None, w, tm), bt, (nb, w, t), dt)
    outs = dict(oa=rows(512, BF16), qb=rows(1024, BF16), kvb16=rows(256, BF16), qi=rows(256, BF16),
                ki16=rows(128, BF16), misc=rows(128, F32), qc=rows(1024, BF16), kc16=rows(512, BF16),
                vbt=((LANES, tm), col, (LANES, r), BF16), vct=((512, tm), col, (512, r), BF16))
    if leaves_t:
        outs.update(kbt32=tr(128, F32), vbt32=tr(128, F32), kit32=tr(HEAD_DIM, F32), lft32=tr(C_HEADS, F32),
                    kct32=tr(512, F32), vct32=tr(512, F32))
    else:
        outs.update(va=rows(512, F32), kvb32=rows(256, F32), ki32=rows(128, F32), kc32=rows(512, F32),
                    vc32=rows(512, F32))
    return outs


def _mixin_body(x_ref, gn_ref, w_ref, lng_ref, lnb_ref, ws_ref, ab_ref, bf_ref, rc_ref, rs_ref, *out_refs,
                tm, names):
    o = dict(zip(names, out_refs))
    leaves_t = "kct32" in o
    chunks = [slice(c * LANES, (c + 1) * LANES) for c in range(tm // LANES)]
    xn = _rms(x_ref[...], gn_ref[...]).astype(BF16)
    rc = rc_ref[...]
    rs = rs_ref[...]
    lane = lax.broadcasted_iota(I32, (tm, LANES), 1)
    first_half = (lane % HEAD_DIM) < (HEAD_DIM // ROT_FRAC // 2)

    def rope(v):
        sw = jnp.where(first_half, pltpu.roll(v, LANES - 8, 1), pltpu.roll(v, 8, 1))
        return v * rc + sw * rs

    def proj(a, b):
        return _dot(xn, w_ref[:, a:b])

    ga = jax.nn.gelu(proj(C_A, C_A + 1024))
    u = ga[:, :512]
    v = ga[:, 512:]
    mu = jnp.mean(v, axis=-1, keepdims=True)
    vc = v - mu
    var = jnp.mean(vc * vc, axis=-1, keepdims=True)
    vn = vc * lax.rsqrt(var + EPS) * lng_ref[...] + lnb_ref[...]
    if not leaves_t:
        o["va"][...] = vn
    vn16 = vn.astype(BF16)
    for rows in chunks:
        for g in range(A_GROUPS):
            cols = slice(g * LANES, (g + 1) * LANES)
            mixed = _dot(ws_ref[g], vn16[rows, cols]) + ab_ref[:, cols]
            o["oa"][rows, cols] = (u[rows, cols] * mixed).astype(BF16)

    for s in range(B_HEADS):
        cols = slice(s * LANES, (s + 1) * LANES)
        o["qb"][:, cols] = (rope(proj(C_BQ + s * LANES, C_BQ + (s + 1) * LANES)) * QK_SCALE).astype(BF16)
    hk = rope(proj(C_BKV, C_BKV + 128))
    hv = proj(C_BKV + 128, C_BKV + 256)
    o["kvb16"][:, :128] = hk.astype(BF16)
    o["kvb16"][:, 128:] = hv.astype(BF16)
    for rows in chunks:
        vt = hv[rows, :].T
        o["vbt"][:, rows] = vt.astype(BF16)
        if leaves_t:
            o["vbt32"][:, rows] = vt
            o["kbt32"][:, rows] = hk[rows, :].T
    for s in range(2):
        cols = slice(s * LANES, (s + 1) * LANES)
        o["qi"][:, cols] = (rope(proj(C_IQ + s * LANES, C_IQ + (s + 1) * LANES)) * IDX_SCALE).astype(BF16)
    hki = rope(proj(C_KI, C_KI + 128))
    o["ki16"][...] = hki.astype(BF16)
    hm = proj(C_MISC, C_MISC + 128)
    lf = jax.nn.log_sigmoid(hm + bf_ref[...])
    misc = jnp.where(lane < MISC_WI, lf, jnp.where(lane < MISC_WI + IDX_HEADS, hm * 0.5, 0.0))
    o["misc"][...] = misc

    for s in range(C_HEADS):
        cols = slice(s * LANES, (s + 1) * LANES)
        o["qc"][:, cols] = (proj(C_CQ + s * LANES, C_CQ + (s + 1) * LANES) * QK_SCALE).astype(BF16)
    hck = proj(C_CK, C_CK + 512)
    o["kc16"][...] = hck.astype(BF16)
    hcv = proj(C_CV, C_CV + 512)
    for rows in chunks:
        for j in range(C_HEADS // 2):
            cols = slice(j * LANES, (j + 1) * LANES)
            vt = hcv[rows, cols].T
            o["vct"][cols, rows] = vt.astype(BF16)
            if leaves_t:
                o["vct32"][cols, rows] = vt
                o["kct32"][cols, rows] = hck[rows, cols].T
    if leaves_t:
        for rows in chunks:
            o["kit32"][:, rows] = hki[rows, :].T[0:HEAD_DIM, :]
            o["lft32"][:, rows] = misc[rows, :].T[MISC_LOGF:MISC_LOGF + C_HEADS, :]
    else:
        o["kvb32"][:, :128] = hk
        o["kvb32"][:, 128:] = hv
        o["ki32"][...] = hki
        o["kc32"][...] = hck
        o["vc32"][...] = hcv


def _mixin(x, gn, w2, lng, lnb, ws16, abias, bfg, rc, rs, *, tm, nb, leaves_t):
    r, d = x.shape
    npos = rc.shape[0] // tm
    row = lambda i: (i, 0)
    const2 = lambda i: (0, 0)
    outs = _mixin_outputs(r, tm, nb, leaves_t)
    res = pl.pallas_call(
        functools.partial(_mixin_body, tm=tm, names=tuple(outs)),
        grid=(r // tm,),
        in_specs=[
            pl.BlockSpec((tm, d), row),
            pl.BlockSpec((1, d), const2),
            pl.BlockSpec((d, C_END), const2),
            pl.BlockSpec((1, 512), const2),
            pl.BlockSpec((1, 512), const2),
            pl.BlockSpec((A_GROUPS, LANES, LANES), lambda i: (0, 0, 0)),
            pl.BlockSpec((LANES, 512), const2),
            pl.BlockSpec((1, LANES), const2),
            pl.BlockSpec((tm, LANES), lambda i: (i % npos, 0)),
            pl.BlockSpec((tm, LANES), lambda i: (i % npos, 0)),
        ],
        out_specs=[pl.BlockSpec(blk, imap) for blk, imap, _, _ in outs.values()],
        out_shape=[jax.ShapeDtypeStruct(shape, dt) for _, _, shape, dt in outs.values()],
        compiler_params=_cparams("parallel"),
        name="mix_in",
    )(x, gn, w2, lng, lnb, ws16, abias, bfg, rc, rs)
    return dict(zip(outs, res))


FB_LANES = 6


def _split3(x):
    hi = x.astype(BF16)
    r1 = x - hi.astype(F32)
    mid = r1.astype(BF16)
    return hi, mid, (r1 - mid.astype(F32)).astype(BF16)


def _cum_body(misc_ref, cum_ref, kb_ref, *, t):
    r = lax.broadcasted_iota(I32, (LANES, LANES), 0)
    c = lax.broadcasted_iota(I32, (LANES, LANES), 1)
    tri = jnp.where(c <= r, 1.0, 0.0).astype(F32)
    place = [jnp.where((r < C_HEADS) & (c == FB_LANES * r + i), 1.0, 0.0).astype(BF16) for i in range(3)]
    lane = lax.broadcasted_iota(I32, (LANES, LANES), 1)
    ones = jnp.where((lane < FB_LANES * C_HEADS) & (lane % FB_LANES >= 3), 1.0, 0.0)
    carry = jnp.zeros((1, LANES), F32)
    for ch in range(t // LANES):
        rows = slice(ch * LANES, (ch + 1) * LANES)
        cum = _dot(tri, misc_ref[rows, :], precision=lax.Precision.HIGHEST) + carry
        carry = cum[LANES - 1:LANES, :]
        cum_ref[rows, :] = cum
        pieces = _split3(cum * (-LOG2E))
        kb_ref[rows, :] = (ones + sum(_dot(pc, pl_) for pc, pl_ in zip(pieces, place))).astype(BF16)


def _cum_logf(misc, nb, t):
    blk = pl.BlockSpec((t, LANES), lambda b: (b, 0))
    return pl.pallas_call(
        functools.partial(_cum_body, t=t),
        grid=(nb,),
        in_specs=[blk],
        out_specs=[blk, blk],
        out_shape=[jax.ShapeDtypeStruct(misc.shape, F32), jax.ShapeDtypeStruct(misc.shape, BF16)],
        compiler_params=_cparams("parallel"),
        name="cum_logf",
    )(misc)


def _fox_body(q_ref, k_ref, kb_ref, vt_ref, cum_ref, o_ref, *, tq, tk):
    i = pl.program_id(1)
    t0 = pl.multiple_of(i * tq, tq)
    nfull = t0 // tk
    kpos = lax.broadcasted_iota(I32, (tk, 2 * tq), 0)
    qpos = t0 + lax.broadcasted_iota(I32, (tk, 2 * tq), 1) % tq
    lane = lax.broadcasted_iota(I32, (2 * tq, LANES), 1)
    first = lax.broadcasted_iota(I32, (2 * tq, LANES), 0) < tq

    def q_side(p):
        qp = jnp.concatenate([q_ref[:, 2 * p * LANES:(2 * p + 1) * LANES],
                              q_ref[:, (2 * p + 1) * LANES:(2 * p + 2) * LANES]], axis=0)
        sel = jnp.zeros((2 * tq, LANES), F32)
        for e, rows in ((0, first), (1, ~first)):
            h = 2 * p + e
            c0 = _split3(cum_ref[pl.ds(t0, 8), h:h + 1][0:1, :] * LOG2E)
            for n in range(3):
                sel = jnp.where(rows & (lane == FB_LANES * h + n), 1.0, sel)
                sel = jnp.where(rows & (lane == FB_LANES * h + 3 + n), c0[n].astype(F32), sel)
        return jnp.concatenate([qp, sel.astype(BF16)], axis=1)

    qq = [q_side(p) for p in range(C_HEADS // 2)]

    def step(j, carry, masked):
        s0 = pl.multiple_of(j * tk, tk)
        kb = kb_ref[pl.ds(s0, tk), :]
        out = []
        for p in range(C_HEADS // 2):
            cols = slice(p * LANES, (p + 1) * LANES)
            m, l, acc = carry[p]
            s = _dot_nt(jnp.concatenate([k_ref[pl.ds(s0, tk), cols], kb], axis=1), qq[p])
            if masked:
                s = jnp.where(s0 + kpos <= qpos, s, NEG)
            m_new = jnp.maximum(m, jnp.max(s, axis=0, keepdims=True))
            alpha = jnp.exp2(m - m_new)
            pr = jnp.exp2(s - m_new)
            l = alpha * l + jnp.sum(pr, axis=0, keepdims=True)
            acc = alpha * acc + _dot(vt_ref[cols, pl.ds(s0, tk)], pr.astype(BF16))
            out.append((m_new, l, acc))
        return tuple(out)

    init = tuple((jnp.full((1, 2 * tq), NEG, F32), jnp.zeros((1, 2 * tq), F32), jnp.zeros((LANES, 2 * tq), F32))
                 for _ in range(C_HEADS // 2))
    carry = lax.fori_loop(0, nfull, functools.partial(step, masked=False), init)
    carry = step(nfull, carry, True)
    first_head = lax.broadcasted_iota(I32, (LANES, tq), 0) < HEAD_DIM
    for p in range(C_HEADS // 2):
        _, l, acc = carry[p]
        out = acc / l
        o_ref[:, p * LANES:(p + 1) * LANES] = jnp.where(first_head, out[:, :tq], out[:, tq:]).T.astype(BF16)


def _fox_prompt(qc16, kc16, kb16, vct16, cum, *, nb, t, tq, tk):
    nq = t // tq
    w = kc16.shape[1]
    return pl.pallas_call(
        functools.partial(_fox_body, tq=tq, tk=tk),
        grid=(nb, nq),
        in_specs=[
            pl.BlockSpec((tq, C_HEADS * LANES), lambda b, i: (b * nq + i, 0)),
            pl.BlockSpec((t, w), lambda b, i: (b, 0)),
            pl.BlockSpec((t, LANES), lambda b, i: (b, 0)),
            pl.BlockSpec((w, t), lambda b, i: (0, b)),
            pl.BlockSpec((t, LANES), lambda b, i: (b, 0)),
        ],
        out_specs=pl.BlockSpec((tq, w), lambda b, i: (b * nq + i, 0)),
        out_shape=jax.ShapeDtypeStruct(kc16.shape, BF16),
        compiler_params=_cparams("parallel", "arbitrary"),
        name="fox_prompt",
    )(qc16, kc16, kb16, vct16, cum)


TOPK_ROWS = 128
TOPK_CH = 512
COUNT_ACC = 32
IDX_BITS = 14
F32_MIN_NORMAL_BITS = 0x00800000


def _pattern_value(key):
    bits = jnp.where(key >= 0, key, key ^ jnp.int32(0x7FFFFFFF))
    bits = jnp.where((bits > 0) & (bits < F32_MIN_NORMAL_BITS), F32_MIN_NORMAL_BITS, bits)
    return pltpu.bitcast(bits, F32)


def _chunk_loop(lo, hi, body, init):
    if isinstance(lo, int) and isinstance(hi, int):
        for c in range(lo, hi):
            init = body(c, init)
        return init
    return lax.fori_loop(lo, hi, body, init)


def _chunk_ds(c, ch):
    return pl.ds(c * ch, ch) if isinstance(c, int) else pl.ds(pl.multiple_of(c * ch, ch), ch)


def _topk_select(s_ref, bias_ref, *, key_axis, nq, nch, ch, k, n_total_ch):
    kf = float(k)
    q_shape = (1, nq) if key_axis == 0 else (nq, 1)
    blk = (ch, nq) if key_axis == 0 else (nq, ch)
    kidx = lax.broadcasted_iota(I32, blk, key_axis)

    def window(c):
        ds = _chunk_ds(c, ch)
        return (ds, slice(None)) if key_axis == 0 else (slice(None), ds)

    def count(pred):
        def body(c, part):
            hit = jnp.where(pred(s_ref[window(c)], c), 1.0, 0.0)
            if key_axis == 0:
                return part + jnp.sum(hit.reshape(ch // COUNT_ACC, COUNT_ACC, nq), axis=0)
            for b in range(ch // LANES):
                part = part + hit[:, b * LANES:(b + 1) * LANES]
            return part
        init = jnp.zeros((COUNT_ACC, nq) if key_axis == 0 else (nq, LANES), F32)
        return jnp.sum(_chunk_loop(0, nch, body, init), axis=key_axis, keepdims=True)

    def vbit(b, key):
        cand = key + (jnp.int32(1) << (jnp.int32(31) - b))
        cv = _pattern_value(cand)
        return jnp.where(count(lambda s, c: s >= cv) >= kf, cand, key)

    thr = _pattern_value(lax.fori_loop(0, 32, vbit, jnp.full(q_shape, INT_MIN, I32)))
    take_all = count(lambda s, c: s > -jnp.inf) <= kf
    thr = jnp.where(take_all, -jnp.inf, thr)
    need = kf - count(lambda s, c: s > thr)
    ties = jnp.where(take_all, 0.0, count(lambda s, c: s == thr))

    def ibit(b, j):
        cand = j + (jnp.int32(1) << (jnp.int32(IDX_BITS - 1) - b))
        return jnp.where(count(lambda s, c: (s == thr) & (c * ch + kidx < cand)) <= need, cand, j)

    jthr = lax.cond(jnp.max(ties - need) > 0.0,
                    lambda: lax.fori_loop(0, IDX_BITS, ibit, jnp.zeros(q_shape, I32)),
                    lambda: jnp.full(q_shape, 1 << IDX_BITS, I32))

    def emit(c, _):
        s = s_ref[window(c)]
        sel = ((s > thr) | ((s == thr) & (c * ch + kidx < jthr))) & (s > -jnp.inf)
        bias_ref[window(c)] = jnp.where(sel, 0.0, NEG).astype(bias_ref.dtype)
        return 0

    _chunk_loop(0, nch, emit, 0)

    def fill(c, _):
        bias_ref[window(c)] = jnp.full(blk, NEG, bias_ref.dtype)
        return 0

    _chunk_loop(nch, n_total_ch, fill, 0)


def _idx_topk_body(qi_ref, misc_ref, ki_ref, bias_ref, s_ref, *, t, k):
    i = pl.program_id(1)
    rows, ch = TOPK_ROWS, TOPK_CH
    q0 = i * rows
    nch = (q0 + rows + ch - 1) // ch
    lo = lax.broadcasted_iota(I32, (rows, LANES), 1) < HEAD_DIM
    qhs = []
    for p in range(IDX_HEADS // 2):
        q2 = qi_ref[:, p * LANES:(p + 1) * LANES]
        zero = jnp.zeros_like(q2)
        qhs += [jnp.where(lo, q2, zero), jnp.where(lo, zero, q2)]
    w_t = misc_ref[...].T
    ws = [w_t[MISC_WI + h:MISC_WI + h + 1, :] for h in range(IDX_HEADS)]
    qpos = q0 + lax.broadcasted_iota(I32, (ch, LANES), 1)
    krow = lax.broadcasted_iota(I32, (ch, LANES), 0)

    def score_chunk(c, _):
        rows_c = _chunk_ds(c, ch)
        score = jnp.zeros((ch, LANES), F32)
        for h in range(IDX_HEADS):
            score = score + ws[h] * jnp.maximum(_dot_nt(ki_ref[rows_c, :], qhs[h]), 0.0)
        s_ref[rows_c, :] = jnp.where(c * ch + krow <= qpos, score, -jnp.inf)
        return 0

    def variant(n):
        def run():
            _chunk_loop(0, n, score_chunk, 0)
            _topk_select(s_ref, bias_ref, key_axis=0, nq=rows, nch=n, ch=ch, k=k, n_total_ch=t // ch)
        return run

    lax.switch(nch - 1, [variant(n) for n in range(1, t // ch + 1)])


def _idx_topk_prompt(qi16, misc, ki16, *, nb, t, k):
    nq = t // TOPK_ROWS
    row = lambda b, i: (b * nq + i, 0)
    return pl.pallas_call(
        functools.partial(_idx_topk_body, t=t, k=k),
        grid=(nb, nq),
        in_specs=[
            pl.BlockSpec((TOPK_ROWS, IDX_HEADS * HEAD_DIM), row),
            pl.BlockSpec((TOPK_ROWS, LANES), row),
            pl.BlockSpec((t, LANES), lambda b, i: (b, 0)),
        ],
        out_specs=pl.BlockSpec((None, t, TOPK_ROWS), lambda b, i: (b * nq + i, 0, 0)),
        out_shape=jax.ShapeDtypeStruct((nb * nq, t, TOPK_ROWS), BF16),
        scratch_shapes=[pltpu.VMEM((t, TOPK_ROWS), F32)],
        compiler_params=_cparams("parallel", "arbitrary"),
        name="idx_topk_prompt",
    )(qi16, misc, ki16)


def _dsa_body(q_ref, k_ref, vt_ref, bias_ref, o_ref):
    tq, ch = TOPK_ROWS, TOPK_CH
    nch = (pl.program_id(1) * tq + tq + ch - 1) // ch
    lane = lax.broadcasted_iota(I32, (tq, LANES), 1)
    lo = lane < HEAD_DIM

    q_all = jnp.concatenate([q_ref[:, h * LANES:(h + 1) * LANES] for h in range(B_HEADS)], axis=0)

    def step(c, carry):
        m, l, acc = carry
        off = pl.multiple_of(c * ch, ch)
        bias = bias_ref[pl.ds(off, ch), :].astype(F32)
        s = _dot_nt(k_ref[pl.ds(off, ch), :], q_all) + jnp.concatenate([bias] * B_HEADS, axis=1)
        m_new = jnp.maximum(m, jnp.max(s, axis=0, keepdims=True))
        alpha = jnp.exp2(m - m_new)
        pr = jnp.exp2(s - m_new)
        l = alpha * l + jnp.sum(pr, axis=0, keepdims=True)
        acc = alpha * acc + _dot(vt_ref[:, pl.ds(off, ch)], pr.astype(BF16))
        return m_new, l, acc

    cols = B_HEADS * tq
    init = (jnp.full((1, cols), 2 * NEG, F32), jnp.zeros((1, cols), F32), jnp.zeros((LANES, cols), F32))
    _, l, acc = lax.fori_loop(0, nch, step, init)
    out = acc / l
    res = [out[:, h * tq:(h + 1) * tq].T for h in range(B_HEADS)]
    for p in range(B_HEADS // 2):
        g = (2 * p) // (B_HEADS // B_KV_HEADS)
        a, b = res[2 * p], res[2 * p + 1]
        if g == 0:
            b = pltpu.roll(b, HEAD_DIM, 1)
        else:
            a = pltpu.roll(a, HEAD_DIM, 1)
        o_ref[:, p * LANES:(p + 1) * LANES] = jnp.where(lo, a, b).astype(BF16)


def _dsa_prompt(qb16, kvb16, vbt16, bias_t, *, nb, t):
    tq = TOPK_ROWS
    nq = t // tq
    row = lambda b, i: (b * nq + i, 0)
    return pl.pallas_call(
        _dsa_body,
        grid=(nb, nq),
        in_specs=[
            pl.BlockSpec((tq, B_HEADS * LANES), row),
            pl.BlockSpec((t, LANES), lambda b, i: (b, 0)),
            pl.BlockSpec((LANES, t), lambda b, i: (0, b)),
            pl.BlockSpec((None, t, tq), lambda b, i: (b * nq + i, 0, 0)),
        ],
        out_specs=pl.BlockSpec((tq, 512), row),
        out_shape=jax.ShapeDtypeStruct((nb * t, 512), BF16),
        compiler_params=_cparams("parallel", "arbitrary"),
        name="dsa_prompt",
    )(qb16, kvb16, vbt16, bias_t)


def _page_specs(n_pages, layer, block):
    def spec(j):
        return pl.BlockSpec((None, None) + block, lambda b, pt: (layer, pt[b, j]) + (0,) * len(block))
    return [spec(j) for j in range(n_pages)]


def _s_score_body(pt_ref, q_ref, w_ref, *refs, n_pages):
    pages, new_ref, o_ref = refs[:n_pages], refs[n_pages], refs[n_pages + 1]
    q = q_ref[...]
    w = w_ref[...]
    for j in range(n_pages + 1):
        s = _dot(q, pages[j][...].astype(BF16)) if j < n_pages else _dot_nt(q, new_ref[...])
        s = jnp.maximum(s, 0.0)
        score = jnp.zeros((8, LANES), F32)
        for h in range(IDX_HEADS):
            score = score + w[:, h:h + 1] * s[h * 8:(h + 1) * 8, :]
        o_ref[:, j * LANES:(j + 1) * LANES] = score[0:4, :]


def _s_score(pt, qi_s, wi_s, cache_kidx, ki_new, *, layer, ns, n_pages):
    lw = (n_pages + 1) * LANES
    return pl.pallas_call(
        functools.partial(_s_score_body, n_pages=n_pages),
        grid_spec=pltpu.PrefetchScalarGridSpec(
            num_scalar_prefetch=1, grid=(ns,),
            in_specs=[pl.BlockSpec((None, 32, HEAD_DIM), lambda b, pt: (b, 0, 0)),
                      pl.BlockSpec((None, 8, IDX_HEADS), lambda b, pt: (b, 0, 0))]
            + _page_specs(n_pages, layer, (HEAD_DIM, LANES))
            + [pl.BlockSpec((None, LANES, HEAD_DIM), lambda b, pt: (b, 0, 0))],
            out_specs=pl.BlockSpec((None, 4, lw), lambda b, pt: (b, 0, 0))),
        out_shape=jax.ShapeDtypeStruct((ns, 4, lw), F32),
        compiler_params=_cparams("arbitrary"),
        name="sample_idx_score",
    )(pt, qi_s, wi_s, *([cache_kidx] * n_pages), ki_new)


def _s_topk_body(s_ref, bias_ref, m_ref, *, past, nt, k, lw):
    rows = TOPK_ROWS
    col = lax.broadcasted_iota(I32, (rows, LANES), 1)
    tq = lax.broadcasted_iota(I32, (rows, LANES), 0) % nt
    nch = lw // LANES
    for c in range(nch):
        cols = slice(c * LANES, (c + 1) * LANES)
        s = s_ref[:, cols]
        if (c + 1) * LANES > past:
            s = jnp.where(c * LANES + col - past <= tq, s, -jnp.inf)
        m_ref[:, cols] = s
    _topk_select(m_ref, bias_ref, key_axis=1, nq=rows, nch=nch, ch=LANES, k=k, n_total_ch=nch)


def _s_topk(scores, *, past, nt, k):
    r, lw = scores.shape
    return pl.pallas_call(
        functools.partial(_s_topk_body, past=past, nt=nt, k=k, lw=lw),
        grid=(r // TOPK_ROWS,),
        in_specs=[pl.BlockSpec((TOPK_ROWS, lw), lambda i: (i, 0))],
        out_specs=pl.BlockSpec((TOPK_ROWS, lw), lambda i: (i, 0)),
        out_shape=jax.ShapeDtypeStruct((r, lw), F32),
        scratch_shapes=[pltpu.VMEM((TOPK_ROWS, lw), F32)],
        compiler_params=_cparams("parallel"),
        name="sample_topk",
    )(scores)


NT_PAD = 8


def _softmax_pv(s_past, s_new, vt_past, v_new):
    m = jnp.maximum(jnp.max(s_past, axis=-1, keepdims=True), jnp.max(s_new, axis=-1, keepdims=True))
    p_past = jnp.exp2(s_past - m)
    p_new = jnp.exp2(s_new - m)
    l = jnp.sum(p_past, axis=-1, keepdims=True) + jnp.sum(p_new, axis=-1, keepdims=True)
    return (_dot_nt(p_past.astype(BF16), vt_past) + _dot(p_new.astype(BF16), v_new)) / l


def _head_t(page_refs, h):
    return jnp.concatenate([r[h] for r in page_refs], axis=1).astype(BF16)


def _seq_page_specs(n_pages, layer, block, spb, s):
    def spec(j):
        return pl.BlockSpec((None, None) + block, lambda b, pt: (layer, pt[b * spb + s, j]) + (0,) * len(block))
    return [spec(j) for j in range(n_pages)]


DSA_SEQS_PER_STEP = 4
FOX_SEQS_PER_STEP = 1


def _s_dsa_body(pt_ref, q_ref, bias_ref, *refs, n_pages, spb):
    pages = refs[:2 * n_pages * spb]
    knew, vnew, o_ref = refs[2 * n_pages * spb:]
    past = n_pages * LANES
    hg = B_HEADS // B_KV_HEADS
    for s in range(spb):
        kps = pages[2 * n_pages * s:2 * n_pages * s + n_pages]
        vps = pages[2 * n_pages * s + n_pages:2 * n_pages * (s + 1)]
        bias = bias_ref[s]
        b_past = jnp.concatenate([bias[:, :past]] * hg, axis=0)
        b_new = jnp.concatenate([bias[:, past:past + NT_PAD]] * hg, axis=0)
        for g in range(B_KV_HEADS):
            q = q_ref[s, g]
            s_past = _dot(q, _head_t(kps, g)) + b_past
            s_new = _dot_nt(q, knew[s, :, g, :].astype(BF16)) + b_new
            o_ref[s, g] = _softmax_pv(s_past, s_new, _head_t(vps, g), vnew[s, :, g, :].astype(BF16))


def _s_dsa(pt, qb_s, bias_s, cache_bk, cache_bv, k_new, v_new, *, layer, ns, n_pages):
    lw = (n_pages + 1) * LANES
    hg = B_HEADS // B_KV_HEADS
    spb = DSA_SEQS_PER_STEP if ns % DSA_SEQS_PER_STEP == 0 else 1
    seq4 = lambda b, pt: (b, 0, 0, 0)
    page = (B_KV_HEADS, HEAD_DIM, LANES)
    page_specs, page_args = [], []
    for s in range(spb):
        page_specs += _seq_page_specs(n_pages, layer, page, spb, s) * 2
        page_args += [cache_bk] * n_pages + [cache_bv] * n_pages
    new = pl.BlockSpec((spb, NT_PAD, B_KV_HEADS, HEAD_DIM), seq4)
    return pl.pallas_call(
        functools.partial(_s_dsa_body, n_pages=n_pages, spb=spb),
        grid_spec=pltpu.PrefetchScalarGridSpec(
            num_scalar_prefetch=1, grid=(ns // spb,),
            in_specs=[pl.BlockSpec((spb, B_KV_HEADS, hg * NT_PAD, HEAD_DIM), seq4),
                      pl.BlockSpec((spb, NT_PAD, lw), lambda b, pt: (b, 0, 0))]
            + page_specs + [new, new],
            out_specs=pl.BlockSpec((spb, B_KV_HEADS, hg * NT_PAD, HEAD_DIM), seq4)),
        out_shape=jax.ShapeDtypeStruct((ns, B_KV_HEADS, hg * NT_PAD, HEAD_DIM), F32),
        compiler_params=_cparams("arbitrary"),
        name="sample_dsa",
    )(pt, qb_s, bias_s, *page_args, k_new, v_new)


def _s_fox_body(pt_ref, q_ref, *refs, n_pages, spb):
    pages = refs[:3 * n_pages * spb]
    knew, vnew, fnew, o_ref = refs[3 * n_pages * spb:]
    hi = lax.Precision.HIGHEST
    r = lax.broadcasted_iota(I32, (LANES, LANES), 0)
    c = lax.broadcasted_iota(I32, (LANES, LANES), 1)
    triu = jnp.where(r <= c, 1.0, 0.0).astype(F32)
    triu_new = triu[:NT_PAD, :NT_PAD]
    causal_new = c[:NT_PAD, :NT_PAD] <= r[:NT_PAD, :NT_PAD]
    for s in range(spb):
        base = 3 * n_pages * s
        kps, vps, fps = (pages[base + i * n_pages:base + (i + 1) * n_pages] for i in range(3))
        off = jnp.zeros((C_HEADS, 1), F32)
        cums = []
        for j in range(n_pages):
            cum = _dot(fps[j][...], triu, precision=hi)
            cums.append(cum + off)
            off = off + cum[:, LANES - 1:LANES]
        cum_past = jnp.concatenate(cums, axis=1) * LOG2E
        cum_new = (_dot(fnew[s], triu_new, precision=hi) + off) * LOG2E
        for h in range(C_HEADS):
            q = q_ref[s, h]
            s_past = _dot(q, _head_t(kps, h)) - cum_past[h:h + 1, :]
            s_new = jnp.where(causal_new, _dot_nt(q, knew[s, :, h, :].astype(BF16)) - cum_new[h:h + 1, :], NEG)
            o_ref[s, h] = _softmax_pv(s_past, s_new, _head_t(vps, h), vnew[s, :, h, :].astype(BF16))


def _s_fox(pt, qc_s, cache_ck, cache_cv, cache_lf, k_new, v_new, lf_new, *, layer, ns, n_pages):
    spb = FOX_SEQS_PER_STEP if ns % FOX_SEQS_PER_STEP == 0 else 1
    seq4 = lambda b, pt: (b, 0, 0, 0)
    page = (C_HEADS, HEAD_DIM, LANES)
    page_specs, page_args = [], []
    for s in range(spb):
        page_specs += _seq_page_specs(n_pages, layer, page, spb, s) * 2
        page_specs += _seq_page_specs(n_pages, layer, (C_HEADS, LANES), spb, s)
        page_args += [cache_ck] * n_pages + [cache_cv] * n_pages + [cache_lf] * n_pages
    new = pl.BlockSpec((spb, NT_PAD, C_HEADS, HEAD_DIM), seq4)
    return pl.pallas_call(
        functools.partial(_s_fox_body, n_pages=n_pages, spb=spb),
        grid_spec=pltpu.PrefetchScalarGridSpec(
            num_scalar_prefetch=1, grid=(ns // spb,),
            in_specs=[pl.BlockSpec((spb, C_HEADS, NT_PAD, HEAD_DIM), seq4)] + page_specs
            + [new, new, pl.BlockSpec((spb, C_HEADS, NT_PAD), lambda b, pt: (b, 0, 0))],
            out_specs=pl.BlockSpec((spb, C_HEADS, NT_PAD, HEAD_DIM), seq4)),
        out_shape=jax.ShapeDtypeStruct((ns, C_HEADS, NT_PAD, HEAD_DIM), F32),
        compiler_params=_cparams("arbitrary"),
        name="sample_fox",
    )(pt, qc_s, *page_args, k_new, v_new, lf_new)


def _mixout_body(x_ref, gn_ref, wg_ref, wb_ref, wo_ref, oa_ref, ob_ref, oc_ref, o_ref):
    x = x_ref[...]
    d = x.shape[1]
    xn = _rms(x, gn_ref[...]).astype(BF16)
    merged = jnp.zeros(x.shape, F32)
    for n, br in enumerate((oa_ref, ob_ref, oc_ref)):
        gate = jax.nn.sigmoid(_dot(xn, wg_ref[:, n * d:(n + 1) * d]))
        merged = merged + _dot(br[...], wb_ref[n]) * gate
    o_ref[...] = x + _dot(merged.astype(BF16), wo_ref[...])


def _mixout(x, gn, wg16, wb16, wo16, oa, ob, oc, *, tm):
    r, d = x.shape
    wbr = oa.shape[1]
    row = lambda i: (i, 0)
    const2 = lambda i: (0, 0)
    return pl.pallas_call(
        _mixout_body,
        grid=(r // tm,),
        in_specs=[
            pl.BlockSpec((tm, d), row),
            pl.BlockSpec((1, d), const2),
            pl.BlockSpec((d, N_BRANCH * d), const2),
            pl.BlockSpec((N_BRANCH, wbr, d), lambda i: (0, 0, 0)),
            pl.BlockSpec((d, d), const2),
            pl.BlockSpec((tm, wbr), row),
            pl.BlockSpec((tm, wbr), row),
            pl.BlockSpec((tm, wbr), row),
        ],
        out_specs=pl.BlockSpec((tm, d), row),
        out_shape=jax.ShapeDtypeStruct((r, d), F32),
        compiler_params=_cparams("parallel"),
        name="mix_out",
    )(x, gn, wg16, wb16, wo16, oa, ob, oc)


def _rope_tables(pos):
    rot = HEAD_DIM // ROT_FRAC
    half = rot // 2
    inv = ROPE_THETA ** (-jnp.arange(half, dtype=F32) / half)
    ang = pos.astype(F32)[:, None] * inv[None, :]
    cos, sin = jnp.cos(ang), jnp.sin(ang)
    ones = jnp.ones((pos.shape[0], HEAD_DIM - rot), F32)
    c64 = jnp.concatenate([cos, cos, ones], axis=1)
    s64 = jnp.concatenate([-sin, sin, 0.0 * ones], axis=1)
    return jnp.tile(c64, (1, LANES // HEAD_DIM)), jnp.tile(s64, (1, LANES // HEAD_DIM))


def _layer_weights(l, norm_ffn1, ffn1_wi, ffn1_wo, norm_mix, w_in, b_forget, a_ln_g, a_ln_b, a_ws, a_bs,
                   w_branch, w_out, norm_ffn2, ffn2_wi, ffn2_wo, nt):
    d = w_in.shape[1]
    w = w_in[l]
    widths = (N_BRANCH * d, d, 512, 128, 128, 256, 64, 4, 512, 512, 512, 8)
    cuts = np.concatenate([[0], np.cumsum(widths)])
    (w_gate, w_a, w_bq, w_bk, w_bv, w_iq, w_ik, w_iw, w_cq, w_ck, w_cv, w_cf) = [
        w[:, int(cuts[i]):int(cuts[i + 1])] for i in range(len(widths))]
    slots = []
    for h in range(B_HEADS):
        g = h // (B_HEADS // B_KV_HEADS)
        wh = w_bq[:, h * HEAD_DIM:(h + 1) * HEAD_DIM]
        slots.append(jnp.pad(wh, ((0, 0), (g * HEAD_DIM, (B_KV_HEADS - 1 - g) * HEAD_DIM))))
    cslots = []
    for h in range(C_HEADS):
        wh = w_cq[:, h * HEAD_DIM:(h + 1) * HEAD_DIM]
        cslots.append(jnp.pad(wh, ((0, 0), ((h % 2) * HEAD_DIM, (1 - h % 2) * HEAD_DIM))))
    w_misc = jnp.pad(jnp.concatenate([w_cf, w_iw], axis=1), ((0, 0), (0, LANES - 12)))
    w2 = jnp.concatenate([w_a] + slots + [w_bk, w_bv, w_iq, w_ik, w_ik, w_misc] + cslots + [w_ck, w_cv], axis=1)
    ws = a_ws[l]
    tril = jnp.tril(jnp.ones((LANES, LANES), bool))
    ws_p = jnp.where(tril, ws, 0.0)
    corner = jnp.where(tril[:nt, :nt], ws[:, :nt, :nt], 0.0)
    ws_s = jnp.einsum("ij,gts->gitjs", jnp.eye(LANES // nt, dtype=F32), corner).reshape(A_GROUPS, LANES, LANES)
    bs = a_bs[l]
    ab_p = jnp.repeat(bs.T, LANES, axis=1)
    ab_s = jnp.repeat(jnp.tile(bs[:, :nt].T, (LANES // nt, 1)), LANES, axis=1)
    return dict(
        n1=norm_ffn1[l][None], wi1=ffn1_wi[l].astype(BF16), wo1=ffn1_wo[l].astype(BF16),
        n2=norm_ffn2[l][None], wi2=ffn2_wi[l].astype(BF16), wo2=ffn2_wo[l].astype(BF16),
        nm=norm_mix[l][None], w2=w2.astype(BF16), wg=w_gate.astype(BF16),
        wb=w_branch[l].astype(BF16), wo=w_out[l].astype(BF16),
        lng=a_ln_g[l][None], lnb=a_ln_b[l][None],
        ws_p=ws_p.astype(BF16), ws_s=ws_s.astype(BF16), ab_p=ab_p, ab_s=ab_s,
        bf=jnp.pad(b_forget[l], (0, LANES - C_HEADS))[None],
    )


def _pad_rows(a, n):
    return jnp.pad(a, ((0, 0), (0, n - a.shape[1]), (0, 0)))


def _slot_heads(x, ns, nt, lane_half):
    x = x.reshape(ns, nt, -1, LANES)
    heads = [x[:, :, h, lane_half(h) * HEAD_DIM:(lane_half(h) + 1) * HEAD_DIM] for h in range(x.shape[2])]
    return jnp.pad(jnp.stack(heads, axis=1), ((0, 0), (0, 0), (0, NT_PAD - nt), (0, 0)))


def _new_rows(x, ns, nt):
    x = x.reshape(ns, nt, -1, HEAD_DIM)
    return jnp.pad(x, ((0, 0), (0, NT_PAD - nt), (0, 0), (0, 0)))


def _unpad_heads(o, nt):
    ns, nh = o.shape[:2]
    return o[:, :, :nt].transpose(0, 2, 1, 3).reshape(ns * nt, nh * HEAD_DIM).astype(BF16)


def kernel(x_prompt, x_sample, cache_b_k, cache_b_v, cache_b_kidx, cache_c_k, cache_c_v, cache_c_logf, page_table,
           norm_ffn1, ffn1_wi, ffn1_wo, norm_mix, w_in, b_forget, a_ln_g, a_ln_b, a_ws, a_bs, w_branch, w_out,
           norm_ffn2, ffn2_wi, ffn2_wo, norm_final):
    nb, t, d = x_prompt.shape
    ns, nt, _ = x_sample.shape
    depth = w_in.shape[0]
    n_pages = page_table.shape[1]
    page = cache_b_k.shape[2]
    past = n_pages * page
    assert page == LANES and d == 1024 and t % 512 == 0 and (ns * nt) % TOPK_ROWS == 0 and LANES % nt == 0
    assert nt <= NT_PAD
    k_prompt = min(TOPK_MAX, t // 4)
    k_sample = min(TOPK_MAX, (past + nt) // 4)
    rs_rows = ns * nt
    tm_s = min(256, rs_rows)

    pos_p = jnp.arange(t, dtype=I32)
    pos_s = past + (jnp.arange(rs_rows, dtype=I32) % nt)
    rc_p, rs_p = _rope_tables(pos_p)
    rc_s, rs_s = _rope_tables(pos_s)
    gfin = norm_final[None]

    cbk = cache_b_k.transpose(0, 1, 3, 4, 2)
    cbv = cache_b_v.transpose(0, 1, 3, 4, 2)
    cki = cache_b_kidx.transpose(0, 1, 3, 2)
    cck = cache_c_k.transpose(0, 1, 3, 4, 2)
    ccv = cache_c_v.transpose(0, 1, 3, 4, 2)
    clf = cache_c_logf.transpose(0, 1, 3, 2)

    hp = x_prompt.reshape(nb * t, d)
    hs = x_sample.reshape(rs_rows, d)
    rows_p, rows_s = [], []
    for l in range(depth):
        lw = _layer_weights(l, norm_ffn1, ffn1_wi, ffn1_wo, norm_mix, w_in, b_forget, a_ln_g, a_ln_b, a_ws, a_bs,
                            w_branch, w_out, norm_ffn2, ffn2_wi, ffn2_wo, nt)
        last = l == depth - 1
        hp = _ffn(hp, lw["n1"], lw["wi1"], lw["wo1"], gfin, final_norm=False, tm=512, tf=1408)
        mp = _mixin(hp, lw["nm"], lw["w2"], lw["lng"], lw["lnb"], lw["ws_p"], lw["ab_p"], lw["bf"], rc_p, rs_p,
                    tm=256, nb=nb, leaves_t=True)
        cum, kb = _cum_logf(mp["misc"], nb, t)
        oc = _fox_prompt(mp["qc"], mp["kc16"], kb, mp["vct"], cum, nb=nb, t=t, tq=128, tk=512)
        bias = _idx_topk_prompt(mp["qi"], mp["misc"], mp["ki16"], nb=nb, t=t, k=k_prompt)
        ob = _dsa_prompt(mp["qb"], mp["kvb16"], mp["vbt"], bias, nb=nb, t=t)
        hp = _mixout(hp, lw["nm"], lw["wg"], lw["wb"], lw["wo"], mp["oa"], ob, oc, tm=256)
        hp = _ffn(hp, lw["n2"], lw["wi2"], lw["wo2"], gfin, final_norm=last, tm=512, tf=1408)
        rows_p.append(mp)
        hs = _ffn(hs, lw["n1"], lw["wi1"], lw["wo1"], gfin, final_norm=False, tm=tm_s, tf=1408)
        ms = _mixin(hs, lw["nm"], lw["w2"], lw["lng"], lw["lnb"], lw["ws_s"], lw["ab_s"], lw["bf"], rc_s, rs_s,
                    tm=tm_s, nb=1, leaves_t=False)
        oa, va, qb, kvb32, qi, ki32, ki16, misc, qc, kc32, vc32 = (
            ms[n] for n in ("oa", "va", "qb", "kvb32", "qi", "ki32", "ki16", "misc", "qc", "kc32", "vc32"))
        qi_s = _pad_rows(qi.reshape(ns, nt, IDX_HEADS, HEAD_DIM).transpose(0, 2, 1, 3).reshape(ns * IDX_HEADS, nt, HEAD_DIM),
                         8).reshape(ns, IDX_HEADS * 8, HEAD_DIM)
        wi_s = _pad_rows(misc[:, MISC_WI:MISC_WI + IDX_HEADS].reshape(ns, nt, IDX_HEADS), 8)
        ki_new = _pad_rows(ki16[:, :HEAD_DIM].reshape(ns, nt, HEAD_DIM), page)
        scores = _s_score(page_table, qi_s, wi_s, cki, ki_new, layer=l, ns=ns, n_pages=n_pages)
        bias_s = _s_topk(scores.reshape(rs_rows, -1), past=past, nt=nt, k=k_sample).reshape(ns, nt, -1)
        hg = B_HEADS // B_KV_HEADS
        qb_s = _slot_heads(qb, ns, nt, lambda h: h // hg).reshape(ns, B_KV_HEADS, hg * NT_PAD, HEAD_DIM)
        ob_raw = _s_dsa(page_table, qb_s, _pad_rows(bias_s, NT_PAD), cbk, cbv,
                        _new_rows(kvb32[:, :128], ns, nt), _new_rows(kvb32[:, 128:], ns, nt),
                        layer=l, ns=ns, n_pages=n_pages)
        ob = _unpad_heads(ob_raw.reshape(ns, B_HEADS, NT_PAD, HEAD_DIM), nt)
        qc_s = _slot_heads(qc, ns, nt, lambda h: h % 2)
        lf_new = _pad_rows(misc[:, MISC_LOGF:MISC_LOGF + C_HEADS].reshape(ns, nt, C_HEADS), NT_PAD).transpose(0, 2, 1)
        oc_raw = _s_fox(page_table, qc_s, cck, ccv, clf,
                        _new_rows(kc32, ns, nt), _new_rows(vc32, ns, nt), lf_new, layer=l, ns=ns, n_pages=n_pages)
        oc = _unpad_heads(oc_raw, nt)
        hs = _mixout(hs, lw["nm"], lw["wg"], lw["wb"], lw["wo"], oa, ob, oc, tm=tm_s)
        hs = _ffn(hs, lw["n2"], lw["wi2"], lw["wo2"], gfin, final_norm=last, tm=tm_s, tf=1408)
        rows_s.append(dict(k_b=kvb32[:, :128], v_b=kvb32[:, 128:], k_i=ki32[:, :HEAD_DIM], k_c=kc32, v_c=vc32,
                           logf=misc[:, MISC_LOGF:MISC_LOGF + C_HEADS], v_a=va))

    def stack(rows, key, shape):
        return jnp.stack([r[key] for r in rows]).reshape((depth,) + shape)

    def stack_t(key, heads):
        a = jnp.stack([r[key] for r in rows_p])
        if heads is None:
            return a.transpose(0, 1, 3, 2)
        return a.reshape(depth, nb, heads, -1, t).transpose(0, 1, 4, 2, 3)

    y_prompt = hp.reshape(nb, t, d)
    y_sample = hs.reshape(ns, nt, d)
    lead = (ns, nt)
    outs = [y_prompt, y_sample,
            stack_t("kbt32", B_KV_HEADS), stack_t("vbt32", B_KV_HEADS), stack_t("kit32", None),
            stack_t("kct32", C_HEADS), stack_t("vct32", C_HEADS), stack_t("lft32", None),
            stack(rows_s, "k_b", lead + (B_KV_HEADS, HEAD_DIM)), stack(rows_s, "v_b", lead + (B_KV_HEADS, HEAD_DIM)),
            stack(rows_s, "k_i", lead + (HEAD_DIM,)), stack(rows_s, "k_c", lead + (C_HEADS, HEAD_DIM)),
            stack(rows_s, "v_c", lead + (C_HEADS, HEAD_DIM)), stack(rows_s, "logf", lead + (C_HEADS,)),
            stack(rows_s, "v_a", (ns, nt, 512))]
    return tuple(outs)
```

```python
import functools

import numpy as np
import jax
import jax.numpy as jnp
from jax import lax
from jax.experimental import pallas as pl
from jax.experimental.pallas import tpu as pltpu

F32 = jnp.float32
BF16 = jnp.bfloat16
I32 = jnp.int32

LANES = 128
HEAD_DIM = 64
N_BRANCH = 3
A_GROUPS = 4
B_HEADS = 8
B_KV_HEADS = 2
IDX_HEADS = 4
C_HEADS = 8
TOPK_MAX = 256
ROT_FRAC = 4
ROPE_THETA = 500000.0
EPS = 1e-6
LOG2E = 1.4426950408889634
QK_SCALE = HEAD_DIM ** -0.5 * LOG2E
IDX_SCALE = HEAD_DIM ** -0.5
NEG = -1e30
INT_MIN = -2 ** 31
VMEM_LIMIT = 56 * 1024 * 1024

_NT = (((1,), (1,)), ((), ()))


def _cparams(*sem):
    return pltpu.CompilerParams(dimension_semantics=sem, vmem_limit_bytes=VMEM_LIMIT)


def _dot(a, b, precision=None):
    return jnp.dot(a, b, preferred_element_type=F32, precision=precision)


def _dot_nt(a, b, precision=None):
    return lax.dot_general(a, b, _NT, preferred_element_type=F32, precision=precision)


def _rms(x, g):
    return x * lax.rsqrt(jnp.mean(x * x, axis=-1, keepdims=True) + EPS) * g


def _ffn_body(x_ref, gn_ref, wig_ref, wiu_ref, wo_ref, gf_ref, o_ref, xn_ref, acc_ref, *, nf, final_norm):
    f = pl.program_id(1)

    @pl.when(f == 0)
    def _():
        xn_ref[...] = _rms(x_ref[...], gn_ref[...]).astype(BF16)
        acc_ref[...] = jnp.zeros_like(acc_ref)

    xn = xn_ref[...]
    g = _dot(xn, wig_ref[...])
    u = _dot(xn, wiu_ref[...])
    h = (jax.nn.silu(g) * u).astype(BF16)
    acc_ref[...] += _dot(h, wo_ref[...])

    @pl.when(f == nf - 1)
    def _():
        y = x_ref[...] + 0.5 * acc_ref[...]
        if final_norm:
            y = _rms(y, gf_ref[...])
        o_ref[...] = y


def _ffn(x, gn, wi16, wo16, gf, *, final_norm, tm, tf):
    r, d = x.shape
    dff = wo16.shape[0]
    nf = dff // tf
    return pl.pallas_call(
        functools.partial(_ffn_body, nf=nf, final_norm=final_norm),
        grid=(r // tm, nf),
        in_specs=[
            pl.BlockSpec((tm, d), lambda i, f: (i, 0)),
            pl.BlockSpec((1, d), lambda i, f: (0, 0)),
            pl.BlockSpec((d, tf), lambda i, f: (0, f)),
            pl.BlockSpec((d, tf), lambda i, f: (0, nf + f)),
            pl.BlockSpec((tf, d), lambda i, f: (f, 0)),
            pl.BlockSpec((1, d), lambda i, f: (0, 0)),
        ],
        out_specs=pl.BlockSpec((tm, d), lambda i, f: (i, 0)),
        out_shape=jax.ShapeDtypeStruct((r, d), F32),
        scratch_shapes=[pltpu.VMEM((tm, d), BF16), pltpu.VMEM((tm, d), F32)],
        compiler_params=_cparams("parallel", "arbitrary"),
        name="ffn",
    )(x, gn, wi16, wi16, wo16, gf)


C_A = 0
C_BQ = 1024
C_BKV = 2048
C_IQ = 2304
C_KI = 2560
C_MISC = 2688
C_CQ = 2816
C_CK = 3840
C_CV = 4352
C_END = 4864
MISC_LOGF = 0
MISC_WI = 8


def _mixin_outputs(r, tm, nb, leaves_t):
    t = r // nb
    npb = t // tm
    row = lambda i: (i, 0)
    col = lambda i: (0, i)
    bt = lambda i: (i // npb, 0, i % npb)
    rows = lambda w, dt: ((tm, w), row, (r, w), dt)
    tr = lambda w, dt: ((None, w, tm), bt, (nb, w, t), dt)
    outs = dict(oa=rows(512, BF16), qb=rows(1024, BF16), kvb16=rows(256, BF16), qi=rows(256, BF16),
                ki16=rows(128, BF16), misc=rows(128, F32), qc=rows(1024, BF16), kc16=rows(512, BF16),
                vbt=((LANES, tm), col, (LANES, r), BF16), vct=((512, tm), col, (512, r), BF16))
    if leaves_t:
        outs.update(kbt32=tr(128, F32), vbt32=tr(128, F32), kit32=tr(HEAD_DIM, F32), lft32=tr(C_HEADS, F32),
                    kct32=tr(512, F32), vct32=tr(512, F32))
    else:
        outs.update(va=rows(512, F32), kvb32=rows(256, F32), ki32=rows(128, F32), kc32=rows(512, F32),
                    vc32=rows(512, F32))
    return outs


def _mixin_body(x_ref, gn_ref, w_ref, lng_ref, lnb_ref, ws_ref, ab_ref, bf_ref, rc_ref, rs_ref, *out_refs,
                tm, names):
    o = dict(zip(names, out_refs))
    leaves_t = "kct32" in o
    chunks = [slice(c * LANES, (c + 1) * LANES) for c in range(tm // LANES)]
    xn = _rms(x_ref[...], gn_ref[...]).astype(BF16)
    rc = rc_ref[...]
    rs = rs_ref[...]
    lane = lax.broadcasted_iota(I32, (tm, LANES), 1)
    first_half = (lane % HEAD_DIM) < (HEAD_DIM // ROT_FRAC // 2)

    def rope(v):
        sw = jnp.where(first_half, pltpu.roll(v, LANES - 8, 1), pltpu.roll(v, 8, 1))
        return v * rc + sw * rs

    def proj(a, b):
        return _dot(xn, w_ref[:, a:b])

    ga = jax.nn.gelu(proj(C_A, C_A + 1024))
    u = ga[:, :512]
    v = ga[:, 512:]
    mu = jnp.mean(v, axis=-1, keepdims=True)
    vc = v - mu
    var = jnp.mean(vc * vc, axis=-1, keepdims=True)
    vn = vc * lax.rsqrt(var + EPS) * lng_ref[...] + lnb_ref[...]
    if not leaves_t:
        o["va"][...] = vn
    vn16 = vn.astype(BF16)
    for rows in chunks:
        for g in range(A_GROUPS):
            cols = slice(g * LANES, (g + 1) * LANES)
            mixed = _dot(ws_ref[g], vn16[rows, cols]) + ab_ref[:, cols]
            o["oa"][rows, cols] = (u[rows, cols] * mixed).astype(BF16)

    for s in range(B_HEADS):
        cols = slice(s * LANES, (s + 1) * LANES)
        o["qb"][:, cols] = (rope(proj(C_BQ + s * LANES, C_BQ + (s + 1) * LANES)) * QK_SCALE).astype(BF16)
    hk = rope(proj(C_BKV, C_BKV + 128))
    hv = proj(C_BKV + 128, C_BKV + 256)
    o["kvb16"][:, :128] = hk.astype(BF16)
    o["kvb16"][:, 128:] = hv.astype(BF16)
    for rows in chunks:
        vt = hv[rows, :].T
        o["vbt"][:, rows] = vt.astype(BF16)
        if leaves_t:
            o["vbt32"][:, rows] = vt
            o["kbt32"][:, rows] = hk[rows, :].T
    for s in range(2):
        cols = slice(s * LANES, (s + 1) * LANES)
        o["qi"][:, cols] = (rope(proj(C_IQ + s * LANES, C_IQ + (s + 1) * LANES)) * IDX_SCALE).astype(BF16)
    hki = rope(proj(C_KI, C_KI + 128))
    o["ki16"][...] = hki.astype(BF16)
    hm = proj(C_MISC, C_MISC + 128)
    lf = jax.nn.log_sigmoid(hm + bf_ref[...])
    misc = jnp.where(lane < MISC_WI, lf, jnp.where(lane < MISC_WI + IDX_HEADS, hm * 0.5, 0.0))
    o["misc"][...] = misc

    for s in range(C_HEADS):
        cols = slice(s * LANES, (s + 1) * LANES)
        o["qc"][:, cols] = (proj(C_CQ + s * LANES, C_CQ + (s + 1) * LANES) * QK_SCALE).astype(BF16)
    hck = proj(C_CK, C_CK + 512)
    o["kc16"][...] = hck.astype(BF16)
    hcv = proj(C_CV, C_CV + 512)
    for rows in chunks:
        for j in range(C_HEADS // 2):
            cols = slice(j * LANES, (j + 1) * LANES)
            vt = hcv[rows, cols].T
            o["vct"][cols, rows] = vt.astype(BF16)
            if leaves_t:
                o["vct32"][cols, rows] = vt
                o["kct32"][cols, rows] = hck[rows, cols].T
    if leaves_t:
        for rows in chunks:
            o["kit32"][:, rows] = hki[rows, :].T[0:HEAD_DIM, :]
            o["lft32"][:, rows] = misc[rows, :].T[MISC_LOGF:MISC_LOGF + C_HEADS, :]
    else:
        o["kvb32"][:, :128] = hk
        o["kvb32"][:, 128:] = hv
        o["ki32"][...] = hki
        o["kc32"][...] = hck
        o["vc32"][...] = hcv


def _mixin(x, gn, w2, lng, lnb, ws16, abias, bfg, rc, rs, *, tm, nb, leaves_t):
    r, d = x.shape
    npos = rc.shape[0] // tm
    row = lambda i: (i, 0)
    const2 = lambda i: (0, 0)
    outs = _mixin_outputs(r, tm, nb, leaves_t)
    res = pl.pallas_call(
        functools.partial(_mixin_body, tm=tm, names=tuple(outs)),
        grid=(r // tm,),
        in_specs=[
            pl.BlockSpec((tm, d), row),
            pl.BlockSpec((1, d), const2),
            pl.BlockSpec((d, C_END), const2),
            pl.BlockSpec((1, 512), const2),
            pl.BlockSpec((1, 512), const2),
            pl.BlockSpec((A_GROUPS, LANES, LANES), lambda i: (0, 0, 0)),
            pl.BlockSpec((LANES, 512), const2),
            pl.BlockSpec((1, LANES), const2),
            pl.BlockSpec((tm, LANES), lambda i: (i % npos, 0)),
            pl.BlockSpec((tm, LANES), lambda i: (i % npos, 0)),
        ],
        out_specs=[pl.BlockSpec(blk, imap) for blk, imap, _, _ in outs.values()],
        out_shape=[jax.ShapeDtypeStruct(shape, dt) for _, _, shape, dt in outs.values()],
        compiler_params=_cparams("parallel"),
        name="mix_in",
    )(x, gn, w2, lng, lnb, ws16, abias, bfg, rc, rs)
    return dict(zip(outs, res))


FB_LANES = 6


def _split3(x):
    hi = x.astype(BF16)
    r1 = x - hi.astype(F32)
    mid = r1.astype(BF16)
    return hi, mid, (r1 - mid.astype(F32)).astype(BF16)


def _cum_body(misc_ref, cum_ref, kb_ref, *, t):
    r = lax.broadcasted_iota(I32, (LANES, LANES), 0)
    c = lax.broadcasted_iota(I32, (LANES, LANES), 1)
    tri = jnp.where(c <= r, 1.0, 0.0).astype(F32)
    place = [jnp.where((r < C_HEADS) & (c == FB_LANES * r + i), 1.0, 0.0).astype(BF16) for i in range(3)]
    lane = lax.broadcasted_iota(I32, (LANES, LANES), 1)
    ones = jnp.where((lane < FB_LANES * C_HEADS) & (lane % FB_LANES >= 3), 1.0, 0.0)
    carry = jnp.zeros((1, LANES), F32)
    for ch in range(t // LANES):
        rows = slice(ch * LANES, (ch + 1) * LANES)
        cum = _dot(tri, misc_ref[rows, :], precision=lax.Precision.HIGHEST) + carry
        carry = cum[LANES - 1:LANES, :]
        cum_ref[rows, :] = cum
        pieces = _split3(cum * (-LOG2E))
        kb_ref[rows, :] = (ones + sum(_dot(pc, pl_) for pc, pl_ in zip(pieces, place))).astype(BF16)


def _cum_logf(misc, nb, t):
    blk = pl.BlockSpec((t, LANES), lambda b: (b, 0))
    return pl.pallas_call(
        functools.partial(_cum_body, t=t),
        grid=(nb,),
        in_specs=[blk],
        out_specs=[blk, blk],
        out_shape=[jax.ShapeDtypeStruct(misc.shape, F32), jax.ShapeDtypeStruct(misc.shape, BF16)],
        compiler_params=_cparams("parallel"),
        name="cum_logf",
    )(misc)


def _fox_body(q_ref, k_ref, kb_ref, vt_ref, cum_ref, o_ref, *, tq, tk):
    i = pl.program_id(1)
    t0 = pl.multiple_of(i * tq, tq)
    nfull = t0 // tk
    kpos = lax.broadcasted_iota(I32, (tk, 2 * tq), 0)
    qpos = t0 + lax.broadcasted_iota(I32, (tk, 2 * tq), 1) % tq
    lane = lax.broadcasted_iota(I32, (2 * tq, LANES), 1)
    first = lax.broadcasted_iota(I32, (2 * tq, LANES), 0) < tq

    def q_side(p):
        qp = jnp.concatenate([q_ref[:, 2 * p * LANES:(2 * p + 1) * LANES],
                              q_ref[:, (2 * p + 1) * LANES:(2 * p + 2) * LANES]], axis=0)
        sel = jnp.zeros((2 * tq, LANES), F32)
        for e, rows in ((0, first), (1, ~first)):
            h = 2 * p + e
            c0 = _split3(cum_ref[pl.ds(t0, 8), h:h + 1][0:1, :] * LOG2E)
            for n in range(3):
                sel = jnp.where(rows & (lane == FB_LANES * h + n), 1.0, sel)
                sel = jnp.where(rows & (lane == FB_LANES * h + 3 + n), c0[n].astype(F32), sel)
        return jnp.concatenate([qp, sel.astype(BF16)], axis=1)

    qq = [q_side(p) for p in range(C_HEADS // 2)]

    def step(j, carry, masked):
        s0 = pl.multiple_of(j * tk, tk)
        kb = kb_ref[pl.ds(s0, tk), :]
        out = []
        for p in range(C_HEADS // 2):
            cols = slice(p * LANES, (p + 1) * LANES)
            m, l, acc = carry[p]
            s = _dot_nt(jnp.concatenate([k_ref[pl.ds(s0, tk), cols], kb], axis=1), qq[p])
            if masked:
                s = jnp.where(s0 + kpos <= qpos, s, NEG)
            m_new = jnp.maximum(m, jnp.max(s, axis=0, keepdims=True))
            alpha = jnp.exp2(m - m_new)
            pr = jnp.exp2(s - m_new)
            l = alpha * l + jnp.sum(pr, axis=0, keepdims=True)
            acc = alpha * acc + _dot(vt_ref[cols, pl.ds(s0, tk)], pr.astype(BF16))
            out.append((m_new, l, acc))
        return tuple(out)

    init = tuple((jnp.full((1, 2 * tq), NEG, F32), jnp.zeros((1, 2 * tq), F32), jnp.zeros((LANES, 2 * tq), F32))
                 for _ in range(C_HEADS // 2))
    carry = lax.fori_loop(0, nfull, functools.partial(step, masked=False), init)
    carry = step(nfull, carry, True)
    first_head = lax.broadcasted_iota(I32, (LANES, tq), 0) < HEAD_DIM
    for p in range(C_HEADS // 2):
        _, l, acc = carry[p]
        out = acc / l
        o_ref[:, p * LANES:(p + 1) * LANES] = jnp.where(first_head, out[:, :tq], out[:, tq:]).T.astype(BF16)


def _fox_prompt(qc16, kc16, kb16, vct16, cum, *, nb, t, tq, tk):
    nq = t // tq
    w = kc16.shape[1]
    return pl.pallas_call(
        functools.partial(_fox_body, tq=tq, tk=tk),
        grid=(nb, nq),
        in_specs=[
            pl.BlockSpec((tq, C_HEADS * LANES), lambda b, i: (b * nq + i, 0)),
            pl.BlockSpec((t, w), lambda b, i: (b, 0)),
            pl.BlockSpec((t, LANES), lambda b, i: (b, 0)),
            pl.BlockSpec((w, t), lambda b, i: (0, b)),
            pl.BlockSpec((t, LANES), lambda b, i: (b, 0)),
        ],
        out_specs=pl.BlockSpec((tq, w), lambda b, i: (b * nq + i, 0)),
        out_shape=jax.ShapeDtypeStruct(kc16.shape, BF16),
        compiler_params=_cparams("parallel", "arbitrary"),
        name="fox_prompt",
    )(qc16, kc16, kb16, vct16, cum)


TOPK_ROWS = 128
TOPK_CH = 512
COUNT_ACC = 32
IDX_BITS = 14
F32_MIN_NORMAL_BITS = 0x00800000


def _pattern_value(key):
    bits = jnp.where(key >= 0, key, key ^ jnp.int32(0x7FFFFFFF))
    bits = jnp.where((bits > 0) & (bits < F32_MIN_NORMAL_BITS), F32_MIN_NORMAL_BITS, bits)
    return pltpu.bitcast(bits, F32)


def _chunk_loop(lo, hi, body, init):
    if isinstance(lo, int) and isinstance(hi, int):
        for c in range(lo, hi):
            init = body(c, init)
        return init
    return lax.fori_loop(lo, hi, body, init)


def _chunk_ds(c, ch):
    return pl.ds(c * ch, ch) if isinstance(c, int) else pl.ds(pl.multiple_of(c * ch, ch), ch)


def _topk_select(s_ref, bias_ref, *, key_axis, nq, nch, ch, k, n_total_ch):
    kf = float(k)
    q_shape = (1, nq) if key_axis == 0 else (nq, 1)
    blk = (ch, nq) if key_axis == 0 else (nq, ch)
    kidx = lax.broadcasted_iota(I32, blk, key_axis)

    def window(c):
        ds = _chunk_ds(c, ch)
        return (ds, slice(None)) if key_axis == 0 else (slice(None), ds)

    def count(pred):
        def body(c, part):
            hit = jnp.where(pred(s_ref[window(c)], c), 1.0, 0.0)
            if key_axis == 0:
                return part + jnp.sum(hit.reshape(ch // COUNT_ACC, COUNT_ACC, nq), axis=0)
            for b in range(ch // LANES):
                part = part + hit[:, b * LANES:(b + 1) * LANES]
            return part
        init = jnp.zeros((COUNT_ACC, nq) if key_axis == 0 else (nq, LANES), F32)
        return jnp.sum(_chunk_loop(0, nch, body, init), axis=key_axis, keepdims=True)

    def vbit(b, key):
        cand = key + (jnp.int32(1) << (jnp.int32(31) - b))
        cv = _pattern_value(cand)
        return jnp.where(count(lambda s, c: s >= cv) >= kf, cand, key)

    thr = _pattern_value(lax.fori_loop(0, 32, vbit, jnp.full(q_shape, INT_MIN, I32)))
    take_all = count(lambda s, c: s > -jnp.inf) <= kf
    thr = jnp.where(take_all, -jnp.inf, thr)
    need = kf - count(lambda s, c: s > thr)
    ties = jnp.where(take_all, 0.0, count(lambda s, c: s == thr))

    def ibit(b, j):
        cand = j + (jnp.int32(1) << (jnp.int32(IDX_BITS - 1) - b))
        return jnp.where(count(lambda s, c: (s == thr) & (c * ch + kidx < cand)) <= need, cand, j)

    jthr = lax.cond(jnp.max(ties - need) > 0.0,
                    lambda: lax.fori_loop(0, IDX_BITS, ibit, jnp.zeros(q_shape, I32)),
                    lambda: jnp.full(q_shape, 1 << IDX_BITS, I32))

    def emit(c, _):
        s = s_ref[window(c)]
        sel = ((s > thr) | ((s == thr) & (c * ch + kidx < jthr))) & (s > -jnp.inf)
        bias_ref[window(c)] = jnp.where(sel, 0.0, NEG).astype(bias_ref.dtype)
        return 0

    _chunk_loop(0, nch, emit, 0)

    def fill(c, _):
        bias_ref[window(c)] = jnp.full(blk, NEG, bias_ref.dtype)
        return 0

    _chunk_loop(nch, n_total_ch, fill, 0)


def _idx_topk_body(qi_ref, misc_ref, ki_ref, bias_ref, s_ref, *, t, k):
    i = pl.program_id(1)
    rows, ch = TOPK_ROWS, TOPK_CH
    q0 = i * rows
    nch = (q0 + rows + ch - 1) // ch
    lo = lax.broadcasted_iota(I32, (rows, LANES), 1) < HEAD_DIM
    qhs = []
    for p in range(IDX_HEADS // 2):
        q2 = qi_ref[:, p * LANES:(p + 1) * LANES]
        zero = jnp.zeros_like(q2)
        qhs += [jnp.where(lo, q2, zero), jnp.where(lo, zero, q2)]
    w_t = misc_ref[...].T
    ws = [w_t[MISC_WI + h:MISC_WI + h + 1, :] for h in range(IDX_HEADS)]
    qpos = q0 + lax.broadcasted_iota(I32, (ch, LANES), 1)
    krow = lax.broadcasted_iota(I32, (ch, LANES), 0)

    def score_chunk(c, _):
        rows_c = _chunk_ds(c, ch)
        score = jnp.zeros((ch, LANES), F32)
        for h in range(IDX_HEADS):
            score = score + ws[h] * jnp.maximum(_dot_nt(ki_ref[rows_c, :], qhs[h]), 0.0)
        s_ref[rows_c, :] = jnp.where(c * ch + krow <= qpos, score, -jnp.inf)
        return 0

    def variant(n):
        def run():
            _chunk_loop(0, n, score_chunk, 0)
            _topk_select(s_ref, bias_ref, key_axis=0, nq=rows, nch=n, ch=ch, k=k, n_total_ch=t // ch)
        return run

    lax.switch(nch - 1, [variant(n) for n in range(1, t // ch + 1)])


def _idx_topk_prompt(qi16, misc, ki16, *, nb, t, k):
    nq = t // TOPK_ROWS
    row = lambda b, i: (b * nq + i, 0)
    return pl.pallas_call(
        functools.partial(_idx_topk_body, t=t, k=k),
        grid=(nb, nq),
        in_specs=[
            pl.BlockSpec((TOPK_ROWS, IDX_HEADS * HEAD_DIM), row),
            pl.BlockSpec((TOPK_ROWS, LANES), row),
            pl.BlockSpec((t, LANES), lambda b, i: (b, 0)),
        ],
        out_specs=pl.BlockSpec((None, t, TOPK_ROWS), lambda b, i: (b * nq + i, 0, 0)),
        out_shape=jax.ShapeDtypeStruct((nb * nq, t, TOPK_ROWS), BF16),
        scratch_shapes=[pltpu.VMEM((t, TOPK_ROWS), F32)],
        compiler_params=_cparams("parallel", "arbitrary"),
        name="idx_topk_prompt",
    )(qi16, misc, ki16)


def _dsa_body(q_ref, k_ref, vt_ref, bias_ref, o_ref):
    tq, ch = TOPK_ROWS, TOPK_CH
    nch = (pl.program_id(1) * tq + tq + ch - 1) // ch
    lane = lax.broadcasted_iota(I32, (tq, LANES), 1)
    lo = lane < HEAD_DIM

    q_all = jnp.concatenate([q_ref[:, h * LANES:(h + 1) * LANES] for h in range(B_HEADS)], axis=0)

    def step(c, carry):
        m, l, acc = carry
        off = pl.multiple_of(c * ch, ch)
        bias = bias_ref[pl.ds(off, ch), :].astype(F32)
        s = _dot_nt(k_ref[pl.ds(off, ch), :], q_all) + jnp.concatenate([bias] * B_HEADS, axis=1)
        m_new = jnp.maximum(m, jnp.max(s, axis=0, keepdims=True))
        alpha = jnp.exp2(m - m_new)
        pr = jnp.exp2(s - m_new)
        l = alpha * l + jnp.sum(pr, axis=0, keepdims=True)
        acc = alpha * acc + _dot(vt_ref[:, pl.ds(off, ch)], pr.astype(BF16))
        return m_new, l, acc

    cols = B_HEADS * tq
    init = (jnp.full((1, cols), 2 * NEG, F32), jnp.zeros((1, cols), F32), jnp.zeros((LANES, cols), F32))
    _, l, acc = lax.fori_loop(0, nch, step, init)
    out = acc / l
    res = [out[:, h * tq:(h + 1) * tq].T for h in range(B_HEADS)]
    for p in range(B_HEADS // 2):
        g = (2 * p) // (B_HEADS // B_KV_HEADS)
        a, b = res[2 * p], res[2 * p + 1]
        if g == 0:
            b = pltpu.roll(b, HEAD_DIM, 1)
        else:
            a = pltpu.roll(a, HEAD_DIM, 1)
        o_ref[:, p * LANES:(p + 1) * LANES] = jnp.where(lo, a, b).astype(BF16)


def _dsa_prompt(qb16, kvb16, vbt16, bias_t, *, nb, t):
    tq = TOPK_ROWS
    nq = t // tq
    row = lambda b, i: (b * nq + i, 0)
    return pl.pallas_call(
        _dsa_body,
        grid=(nb, nq),
        in_specs=[
            pl.BlockSpec((tq, B_HEADS * LANES), row),
            pl.BlockSpec((t, LANES), lambda b, i: (b, 0)),
            pl.BlockSpec((LANES, t), lambda b, i: (0, b)),
            pl.BlockSpec((None, t, tq), lambda b, i: (b * nq + i, 0, 0)),
        ],
        out_specs=pl.BlockSpec((tq, 512), row),
        out_shape=jax.ShapeDtypeStruct((nb * t, 512), BF16),
        compiler_params=_cparams("parallel", "arbitrary"),
        name="dsa_prompt",
    )(qb16, kvb16, vbt16, bias_t)


def _page_specs(n_pages, layer, block):
    def spec(j):
        return pl.BlockSpec((None, None) + block, lambda b, pt: (layer, pt[b, j]) + (0,) * len(block))
    return [spec(j) for j in range(n_pages)]


def _s_score_body(pt_ref, q_ref, w_ref, *refs, n_pages):
    pages, new_ref, o_ref = refs[:n_pages], refs[n_pages], refs[n_pages + 1]
    q = q_ref[...]
    w = w_ref[...]
    for j in range(n_pages + 1):
        s = _dot(q, pages[j][...].astype(BF16)) if j < n_pages else _dot_nt(q, new_ref[...])
        s = jnp.maximum(s, 0.0)
        score = jnp.zeros((8, LANES), F32)
        for h in range(IDX_HEADS):
            score = score + w[:, h:h + 1] * s[h * 8:(h + 1) * 8, :]
        o_ref[:, j * LANES:(j + 1) * LANES] = score[0:4, :]


def _s_score(pt, qi_s, wi_s, cache_kidx, ki_new, *, layer, ns, n_pages):
    lw = (n_pages + 1) * LANES
    return pl.pallas_call(
        functools.partial(_s_score_body, n_pages=n_pages),
        grid_spec=pltpu.PrefetchScalarGridSpec(
            num_scalar_prefetch=1, grid=(ns,),
            in_specs=[pl.BlockSpec((None, 32, HEAD_DIM), lambda b, pt: (b, 0, 0)),
                      pl.BlockSpec((None, 8, IDX_HEADS), lambda b, pt: (b, 0, 0))]
            + _page_specs(n_pages, layer, (HEAD_DIM, LANES))
            + [pl.BlockSpec((None, LANES, HEAD_DIM), lambda b, pt: (b, 0, 0))],
            out_specs=pl.BlockSpec((None, 4, lw), lambda b, pt: (b, 0, 0))),
        out_shape=jax.ShapeDtypeStruct((ns, 4, lw), F32),
        compiler_params=_cparams("arbitrary"),
        name="sample_idx_score",
    )(pt, qi_s, wi_s, *([cache_kidx] * n_pages), ki_new)


def _s_topk_body(s_ref, bias_ref, m_ref, *, past, nt, k, lw):
    rows = TOPK_ROWS
    col = lax.broadcasted_iota(I32, (rows, LANES), 1)
    tq = lax.broadcasted_iota(I32, (rows, LANES), 0) % nt
    nch = lw // LANES
    for c in range(nch):
        cols = slice(c * LANES, (c + 1) * LANES)
        s = s_ref[:, cols]
        if (c + 1) * LANES > past:
            s = jnp.where(c * LANES + col - past <= tq, s, -jnp.inf)
        m_ref[:, cols] = s
    _topk_select(m_ref, bias_ref, key_axis=1, nq=rows, nch=nch, ch=LANES, k=k, n_total_ch=nch)


def _s_topk(scores, *, past, nt, k):
    r, lw = scores.shape
    return pl.pallas_call(
        functools.partial(_s_topk_body, past=past, nt=nt, k=k, lw=lw),
        grid=(r // TOPK_ROWS,),
        in_specs=[pl.BlockSpec((TOPK_ROWS, lw), lambda i: (i, 0))],
        out_specs=pl.BlockSpec((TOPK_ROWS, lw), lambda i: (i, 0)),
        out_shape=jax.ShapeDtypeStruct((r, lw), F32),
        scratch_shapes=[pltpu.VMEM((TOPK_ROWS, lw), F32)],
        compiler_params=_cparams("parallel"),
        name="sample_topk",
    )(scores)


NT_PAD = 8


def _softmax_pv(s_past, s_new, vt_past, v_new):
    m = jnp.maximum(jnp.max(s_past, axis=-1, keepdims=True), jnp.max(s_new, axis=-1, keepdims=True))
    p_past = jnp.exp2(s_past - m)
    p_new = jnp.exp2(s_new - m)
    l = jnp.sum(p_past, axis=-1, keepdims=True) + jnp.sum(p_new, axis=-1, keepdims=True)
    return (_dot_nt(p_past.astype(BF16), vt_past) + _dot(p_new.astype(BF16), v_new)) / l


def _head_t(page_refs, h):
    return jnp.concatenate([r[h] for r in page_refs], axis=1).astype(BF16)


def _seq_page_specs(n_pages, layer, block, spb, s):
    def spec(j):
        return pl.BlockSpec((None, None) + block, lambda b, pt: (layer, pt[b * spb + s, j]) + (0,) * len(block))
    return [spec(j) for j in range(n_pages)]


DSA_SEQS_PER_STEP = 4
FOX_SEQS_PER_STEP = 1


def _s_dsa_body(pt_ref, q_ref, bias_ref, *refs, n_pages, spb):
    pages = refs[:2 * n_pages * spb]
    knew, vnew, o_ref = refs[2 * n_pages * spb:]
    past = n_pages * LANES
    hg = B_HEADS // B_KV_HEADS
    for s in range(spb):
        kps = pages[2 * n_pages * s:2 * n_pages * s + n_pages]
        vps = pages[2 * n_pages * s + n_pages:2 * n_pages * (s + 1)]
        bias = bias_ref[s]
        b_past = jnp.concatenate([bias[:, :past]] * hg, axis=0)
        b_new = jnp.concatenate([bias[:, past:past + NT_PAD]] * hg, axis=0)
        for g in range(B_KV_HEADS):
            q = q_ref[s, g]
            s_past = _dot(q, _head_t(kps, g)) + b_past
            s_new = _dot_nt(q, knew[s, :, g, :].astype(BF16)) + b_new
            o_ref[s, g] = _softmax_pv(s_past, s_new, _head_t(vps, g), vnew[s, :, g, :].astype(BF16))


def _s_dsa(pt, qb_s, bias_s, cache_bk, cache_bv, k_new, v_new, *, layer, ns, n_pages):
    lw = (n_pages + 1) * LANES
    hg = B_HEADS // B_KV_HEADS
    spb = DSA_SEQS_PER_STEP if ns % DSA_SEQS_PER_STEP == 0 else 1
    seq4 = lambda b, pt: (b, 0, 0, 0)
    page = (B_KV_HEADS, HEAD_DIM, LANES)
    page_specs, page_args = [], []
    for s in range(spb):
        page_specs += _seq_page_specs(n_pages, layer, page, spb, s) * 2
        page_args += [cache_bk] * n_pages + [cache_bv] * n_pages
    new = pl.BlockSpec((spb, NT_PAD, B_KV_HEADS, HEAD_DIM), seq4)
    return pl.pallas_call(
        functools.partial(_s_dsa_body, n_pages=n_pages, spb=spb),
        grid_spec=pltpu.PrefetchScalarGridSpec(
            num_scalar_prefetch=1, grid=(ns // spb,),
            in_specs=[pl.BlockSpec((spb, B_KV_HEADS, hg * NT_PAD, HEAD_DIM), seq4),
                      pl.BlockSpec((spb, NT_PAD, lw), lambda b, pt: (b, 0, 0))]
            + page_specs + [new, new],
            out_specs=pl.BlockSpec((spb, B_KV_HEADS, hg * NT_PAD, HEAD_DIM), seq4)),
        out_shape=jax.ShapeDtypeStruct((ns, B_KV_HEADS, hg * NT_PAD, HEAD_DIM), F32),
        compiler_params=_cparams("arbitrary"),
        name="sample_dsa",
    )(pt, qb_s, bias_s, *page_args, k_new, v_new)


def _s_fox_body(pt_ref, q_ref, *refs, n_pages, spb):
    pages = refs[:3 * n_pages * spb]
    knew, vnew, fnew, o_ref = refs[3 * n_pages * spb:]
    hi = lax.Precision.HIGHEST
    r = lax.broadcasted_iota(I32, (LANES, LANES), 0)
    c = lax.broadcasted_iota(I32, (LANES, LANES), 1)
    triu = jnp.where(r <= c, 1.0, 0.0).astype(F32)
    triu_new = triu[:NT_PAD, :NT_PAD]
    causal_new = c[:NT_PAD, :NT_PAD] <= r[:NT_PAD, :NT_PAD]
    for s in range(spb):
        base = 3 * n_pages * s
        kps, vps, fps = (pages[base + i * n_pages:base + (i + 1) * n_pages] for i in range(3))
        off = jnp.zeros((C_HEADS, 1), F32)
        cums = []
        for j in range(n_pages):
            cum = _dot(fps[j][...], triu, precision=hi)
            cums.append(cum + off)
            off = off + cum[:, LANES - 1:LANES]
        cum_past = jnp.concatenate(cums, axis=1) * LOG2E
        cum_new = (_dot(fnew[s], triu_new, precision=hi) + off) * LOG2E
        for h in range(C_HEADS):
            q = q_ref[s, h]
            s_past = _dot(q, _head_t(kps, h)) - cum_past[h:h + 1, :]
            s_new = jnp.where(causal_new, _dot_nt(q, knew[s, :, h, :].astype(BF16)) - cum_new[h:h + 1, :], NEG)
            o_ref[s, h] = _softmax_pv(s_past, s_new, _head_t(vps, h), vnew[s, :, h, :].astype(BF16))


def _s_fox(pt, qc_s, cache_ck, cache_cv, cache_lf, k_new, v_new, lf_new, *, layer, ns, n_pages):
    spb = FOX_SEQS_PER_STEP if ns % FOX_SEQS_PER_STEP == 0 else 1
    seq4 = lambda b, pt: (b, 0, 0, 0)
    page = (C_HEADS, HEAD_DIM, LANES)
    page_specs, page_args = [], []
    for s in range(spb):
        page_specs += _seq_page_specs(n_pages, layer, page, spb, s) * 2
        page_specs += _seq_page_specs(n_pages, layer, (C_HEADS, LANES), spb, s)
        page_args += [cache_ck] * n_pages + [cache_cv] * n_pages + [cache_lf] * n_pages
    new = pl.BlockSpec((spb, NT_PAD, C_HEADS, HEAD_DIM), seq4)
    return pl.pallas_call(
        functools.partial(_s_fox_body, n_pages=n_pages, spb=spb),
        grid_spec=pltpu.PrefetchScalarGridSpec(
            num_scalar_prefetch=1, grid=(ns // spb,),
            in_specs=[pl.BlockSpec((spb, C_HEADS, NT_PAD, HEAD_DIM), seq4)] + page_specs
            + [new, new, pl.BlockSpec((spb, C_HEADS, NT_PAD), lambda b, pt: (b, 0, 0))],
            out_specs=pl.BlockSpec((spb, C_HEADS, NT_PAD, HEAD_DIM), seq4)),
        out_shape=jax.ShapeDtypeStruct((ns, C_HEADS, NT_PAD, HEAD_DIM), F32),
        compiler_params=_cparams("arbitrary"),
        name="sample_fox",
    )(pt, qc_s, *page_args, k_new, v_new, lf_new)


def _mixout_body(x_ref, gn_ref, wg_ref, wb_ref, wo_ref, oa_ref, ob_ref, oc_ref, o_ref):
    x = x_ref[...]
    d = x.shape[1]
    xn = _rms(x, gn_ref[...]).astype(BF16)
    merged = jnp.zeros(x.shape, F32)
    for n, br in enumerate((oa_ref, ob_ref, oc_ref)):
        gate = jax.nn.sigmoid(_dot(xn, wg_ref[:, n * d:(n + 1) * d]))
        merged = merged + _dot(br[...], wb_ref[n]) * gate
    o_ref[...] = x + _dot(merged.astype(BF16), wo_ref[...])


def _mixout(x, gn, wg16, wb16, wo16, oa, ob, oc, *, tm):
    r, d = x.shape
    wbr = oa.shape[1]
    row = lambda i: (i, 0)
    const2 = lambda i: (0, 0)
    return pl.pallas_call(
        _mixout_body,
        grid=(r // tm,),
        in_specs=[
            pl.BlockSpec((tm, d), row),
            pl.BlockSpec((1, d), const2),
            pl.BlockSpec((d, N_BRANCH * d), const2),
            pl.BlockSpec((N_BRANCH, wbr, d), lambda i: (0, 0, 0)),
            pl.BlockSpec((d, d), const2),
            pl.BlockSpec((tm, wbr), row),
            pl.BlockSpec((tm, wbr), row),
            pl.BlockSpec((tm, wbr), row),
        ],
        out_specs=pl.BlockSpec((tm, d), row),
        out_shape=jax.ShapeDtypeStruct((r, d), F32),
        compiler_params=_cparams("parallel"),
        name="mix_out",
    )(x, gn, wg16, wb16, wo16, oa, ob, oc)


def _rope_tables(pos):
    rot = HEAD_DIM // ROT_FRAC
    half = rot // 2
    inv = ROPE_THETA ** (-jnp.arange(half, dtype=F32) / half)
    ang = pos.astype(F32)[:, None] * inv[None, :]
    cos, sin = jnp.cos(ang), jnp.sin(ang)
    ones = jnp.ones((pos.shape[0], HEAD_DIM - rot), F32)
    c64 = jnp.concatenate([cos, cos, ones], axis=1)
    s64 = jnp.concatenate([-sin, sin, 0.0 * ones], axis=1)
    return jnp.tile(c64, (1, LANES // HEAD_DIM)), jnp.tile(s64, (1, LANES // HEAD_DIM))


def _layer_weights(l, norm_ffn1, ffn1_wi, ffn1_wo, norm_mix, w_in, b_forget, a_ln_g, a_ln_b, a_ws, a_bs,
                   w_branch, w_out, norm_ffn2, ffn2_wi, ffn2_wo, nt):
    d = w_in.shape[1]
    wt = w_in.transpose(2, 0, 1)[:, l, :].astype(BF16)
    w = lax.dot_general(jnp.eye(d, dtype=BF16), wt, _NT, preferred_element_type=BF16)
    widths = (N_BRANCH * d, d, 512, 128, 128, 256, 64, 4, 512, 512, 512, 8)
    cuts = np.concatenate([[0], np.cumsum(widths)])
    (w_gate, w_a, w_bq, w_bk, w_bv, w_iq, w_ik, w_iw, w_cq, w_ck, w_cv, w_cf) = [
        w[:, int(cuts[i]):int(cuts[i + 1])] for i in range(len(widths))]
    slots = []
    for h in range(B_HEADS):
        g = h // (B_HEADS // B_KV_HEADS)
        wh = w_bq[:, h * HEAD_DIM:(h + 1) * HEAD_DIM]
        slots.append(jnp.pad(wh, ((0, 0), (g * HEAD_DIM, (B_KV_HEADS - 1 - g) * HEAD_DIM))))
    cslots = []
    for h in range(C_HEADS):
        wh = w_cq[:, h * HEAD_DIM:(h + 1) * HEAD_DIM]
        cslots.append(jnp.pad(wh, ((0, 0), ((h % 2) * HEAD_DIM, (1 - h % 2) * HEAD_DIM))))
    w_misc = jnp.pad(jnp.concatenate([w_cf, w_iw], axis=1), ((0, 0), (0, LANES - 12)))
    w2 = jnp.concatenate([w_a] + slots + [w_bk, w_bv, w_iq, w_ik, w_ik, w_misc] + cslots + [w_ck, w_cv], axis=1)
    ws = a_ws[l]
    tril = jnp.tril(jnp.ones((LANES, LANES), bool))
    ws_p = jnp.where(tril, ws, 0.0)
    corner = jnp.where(tril[:nt, :nt], ws[:, :nt, :nt], 0.0)
    ws_s = jnp.einsum("ij,gts->gitjs", jnp.eye(LANES // nt, dtype=F32), corner).reshape(A_GROUPS, LANES, LANES)
    bs = a_bs[l]
    ab_p = jnp.repeat(bs.T, LANES, axis=1)
    ab_s = jnp.repeat(jnp.tile(bs[:, :nt].T, (LANES // nt, 1)), LANES, axis=1)
    return dict(
        n1=norm_ffn1[l][None], wi1=ffn1_wi[l].astype(BF16), wo1=ffn1_wo[l].astype(BF16),
        n2=norm_ffn2[l][None], wi2=ffn2_wi[l].astype(BF16), wo2=ffn2_wo[l].astype(BF16),
        nm=norm_mix[l][None], w2=w2.astype(BF16), wg=w_gate.astype(BF16),
        wb=w_branch[l].astype(BF16), wo=w_out[l].astype(BF16),
        lng=a_ln_g[l][None], lnb=a_ln_b[l][None],
        ws_p=ws_p.astype(BF16), ws_s=ws_s.astype(BF16), ab_p=ab_p, ab_s=ab_s,
        bf=jnp.pad(b_forget[l], (0, LANES - C_HEADS))[None],
    )


def _pad_rows(a, n):
    return jnp.pad(a, ((0, 0), (0, n - a.shape[1]), (0, 0)))


def _slot_heads(x, ns, nt, lane_half):
    x = x.reshape(ns, nt, -1, LANES)
    heads = [x[:, :, h, lane_half(h) * HEAD_DIM:(lane_half(h) + 1) * HEAD_DIM] for h in range(x.shape[2])]
    return jnp.pad(jnp.stack(heads, axis=1), ((0, 0), (0, 0), (0, NT_PAD - nt), (0, 0)))


def _new_rows(x, ns, nt):
    x = x.reshape(ns, nt, -1, HEAD_DIM)
    return jnp.pad(x, ((0, 0), (0, NT_PAD - nt), (0, 0), (0, 0)))


def _unpad_heads(o, nt):
    ns, nh = o.shape[:2]
    return o[:, :, :nt].transpose(0, 2, 1, 3).reshape(ns * nt, nh * HEAD_DIM).astype(BF16)


def kernel(x_prompt, x_sample, cache_b_k, cache_b_v, cache_b_kidx, cache_c_k, cache_c_v, cache_c_logf, page_table,
           norm_ffn1, ffn1_wi, ffn1_wo, norm_mix, w_in, b_forget, a_ln_g, a_ln_b, a_ws, a_bs, w_branch, w_out,
           norm_ffn2, ffn2_wi, ffn2_wo, norm_final):
    nb, t, d = x_prompt.shape
    ns, nt, _ = x_sample.shape
    depth = w_in.shape[0]
    n_pages = page_table.shape[1]
    page = cache_b_k.shape[2]
    past = n_pages * page
    assert page == LANES and d == 1024 and t % 512 == 0 and (ns * nt) % TOPK_ROWS == 0 and LANES % nt == 0
    assert nt <= NT_PAD
    k_prompt = min(TOPK_MAX, t // 4)
    k_sample = min(TOPK_MAX, (past + nt) // 4)
    rs_rows = ns * nt
    tm_s = min(256, rs_rows)

    pos_p = jnp.arange(t, dtype=I32)
    pos_s = past + (jnp.arange(rs_rows, dtype=I32) % nt)
    rc_p, rs_p = _rope_tables(pos_p)
    rc_s, rs_s = _rope_tables(pos_s)
    gfin = norm_final[None]

    cbk = cache_b_k.transpose(0, 1, 3, 4, 2)
    cbv = cache_b_v.transpose(0, 1, 3, 4, 2)
    cki = cache_b_kidx.transpose(0, 1, 3, 2)
    cck = cache_c_k.transpose(0, 1, 3, 4, 2)
    ccv = cache_c_v.transpose(0, 1, 3, 4, 2)
    clf = cache_c_logf.transpose(0, 1, 3, 2)

    hp = x_prompt.reshape(nb * t, d)
    hs = x_sample.reshape(rs_rows, d)
    rows_p, rows_s = [], []
    for l in range(depth):
        lw = _layer_weights(l, norm_ffn1, ffn1_wi, ffn1_wo, norm_mix, w_in, b_forget, a_ln_g, a_ln_b, a_ws, a_bs,
                            w_branch, w_out, norm_ffn2, ffn2_wi, ffn2_wo, nt)
        last = l == depth - 1
        hp = _ffn(hp, lw["n1"], lw["wi1"], lw["wo1"], gfin, final_norm=False, tm=512, tf=1408)
        mp = _mixin(hp, lw["nm"], lw["w2"], lw["lng"], lw["lnb"], lw["ws_p"], lw["ab_p"], lw["bf"], rc_p, rs_p,
                    tm=256, nb=nb, leaves_t=True)
        cum, kb = _cum_logf(mp["misc"], nb, t)
        oc = _fox_prompt(mp["qc"], mp["kc16"], kb, mp["vct"], cum, nb=nb, t=t, tq=128, tk=512)
        bias = _idx_topk_prompt(mp["qi"], mp["misc"], mp["ki16"], nb=nb, t=t, k=k_prompt)
        ob = _dsa_prompt(mp["qb"], mp["kvb16"], mp["vbt"], bias, nb=nb, t=t)
        hp = _mixout(hp, lw["nm"], lw["wg"], lw["wb"], lw["wo"], mp["oa"], ob, oc, tm=256)
        hp = _ffn(hp, lw["n2"], lw["wi2"], lw["wo2"], gfin, final_norm=last, tm=512, tf=1408)
        rows_p.append(mp)
        hs = _ffn(hs, lw["n1"], lw["wi1"], lw["wo1"], gfin, final_norm=False, tm=tm_s, tf=1408)
        ms = _mixin(hs, lw["nm"], lw["w2"], lw["lng"], lw["lnb"], lw["ws_s"], lw["ab_s"], lw["bf"], rc_s, rs_s,
                    tm=tm_s, nb=1, leaves_t=False)
        oa, va, qb, kvb32, qi, ki32, ki16, misc, qc, kc32, vc32 = (
            ms[n] for n in ("oa", "va", "qb", "kvb32", "qi", "ki32", "ki16", "misc", "qc", "kc32", "vc32"))
        qi_s = _pad_rows(qi.reshape(ns, nt, IDX_HEADS, HEAD_DIM).transpose(0, 2, 1, 3).reshape(ns * IDX_HEADS, nt, HEAD_DIM),
                         8).reshape(ns, IDX_HEADS * 8, HEAD_DIM)
        wi_s = _pad_rows(misc[:, MISC_WI:MISC_WI + IDX_HEADS].reshape(ns, nt, IDX_HEADS), 8)
        ki_new = _pad_rows(ki16[:, :HEAD_DIM].reshape(ns, nt, HEAD_DIM), page)
        scores = _s_score(page_table, qi_s, wi_s, cki, ki_new, layer=l, ns=ns, n_pages=n_pages)
        bias_s = _s_topk(scores.reshape(rs_rows, -1), past=past, nt=nt, k=k_sample).reshape(ns, nt, -1)
        hg = B_HEADS // B_KV_HEADS
        qb_s = _slot_heads(qb, ns, nt, lambda h: h // hg).reshape(ns, B_KV_HEADS, hg * NT_PAD, HEAD_DIM)
        ob_raw = _s_dsa(page_table, qb_s, _pad_rows(bias_s, NT_PAD), cbk, cbv,
                        _new_rows(kvb32[:, :128], ns, nt), _new_rows(kvb32[:, 128:], ns, nt),
                        layer=l, ns=ns, n_pages=n_pages)
        ob = _unpad_heads(ob_raw.reshape(ns, B_HEADS, NT_PAD, HEAD_DIM), nt)
        qc_s = _slot_heads(qc, ns, nt, lambda h: h % 2)
        lf_new = _pad_rows(misc[:, MISC_LOGF:MISC_LOGF + C_HEADS].reshape(ns, nt, C_HEADS), NT_PAD).transpose(0, 2, 1)
        oc_raw = _s_fox(page_table, qc_s, cck, ccv, clf,
                        _new_rows(kc32, ns, nt), _new_rows(vc32, ns, nt), lf_new, layer=l, ns=ns, n_pages=n_pages)
        oc = _unpad_heads(oc_raw, nt)
        hs = _mixout(hs, lw["nm"], lw["wg"], lw["wb"], lw["wo"], oa, ob, oc, tm=tm_s)
        hs = _ffn(hs, lw["n2"], lw["wi2"], lw["wo2"], gfin, final_norm=last, tm=tm_s, tf=1408)
        rows_s.append(dict(k_b=kvb32[:, :128], v_b=kvb32[:, 128:], k_i=ki32[:, :HEAD_DIM], k_c=kc32, v_c=vc32,
                           logf=misc[:, MISC_LOGF:MISC_LOGF + C_HEADS], v_a=va))

    def stack(rows, key, shape):
        return jnp.stack([r[key] for r in rows]).reshape((depth,) + shape)

    def stack_t(key, heads):
        a = jnp.stack([r[key] for r in rows_p])
        if heads is None:
            return a.transpose(0, 1, 3, 2)
        return a.reshape(depth, nb, heads, -1, t).transpose(0, 1, 4, 2, 3)

    y_prompt = hp.reshape(nb, t, d)
    y_sample = hs.reshape(ns, nt, d)
    lead = (ns, nt)
    outs = [y_prompt, y_sample,
            stack_t("kbt32", B_KV_HEADS), stack_t("vbt32", B_KV_HEADS), stack_t("kit32", None),
            stack_t("kct32", C_HEADS), stack_t("vct32", C_HEADS), stack_t("lft32", None),
            stack(rows_s, "k_b", lead + (B_KV_HEADS, HEAD_DIM)), stack(rows_s, "v_b", lead + (B_KV_HEADS, HEAD_DIM)),
            stack(rows_s, "k_i", lead + (HEAD_DIM,)), stack(rows_s, "k_c", lead + (C_HEADS, HEAD_DIM)),
            stack(rows_s, "v_c", lead + (C_HEADS, HEAD_DIM)), stack(rows_s, "logf", lead + (C_HEADS,)),
            stack(rows_s, "v_a", (ns, nt, 512))]
    return tuple(outs)
```

```python
import functools

import numpy as np
import jax
import jax.numpy as jnp
from jax import lax
from jax.experimental import pallas as pl
from jax.experimental.pallas import tpu as pltpu

F32 = jnp.float32
BF16 = jnp.bfloat16
I32 = jnp.int32

LANES = 128
HEAD_DIM = 64
N_BRANCH = 3
A_GROUPS = 4
B_HEADS = 8
B_KV_HEADS = 2
IDX_HEADS = 4
C_HEADS = 8
TOPK_MAX = 256
ROT_FRAC = 4
ROPE_THETA = 500000.0
EPS = 1e-6
LOG2E = 1.4426950408889634
QK_SCALE = HEAD_DIM ** -0.5 * LOG2E
IDX_SCALE = HEAD_DIM ** -0.5
NEG = -1e30
INT_MIN = -2 ** 31
VMEM_LIMIT = 56 * 1024 * 1024

_NT = (((1,), (1,)), ((), ()))


def _cparams(*sem):
    return pltpu.CompilerParams(dimension_semantics=sem, vmem_limit_bytes=VMEM_LIMIT)


def _dot(a, b, precision=None):
    return jnp.dot(a, b, preferred_element_type=F32, precision=precision)


def _dot_nt(a, b, precision=None):
    return lax.dot_general(a, b, _NT, preferred_element_type=F32, precision=precision)


def _rms(x, g):
    return x * lax.rsqrt(jnp.mean(x * x, axis=-1, keepdims=True) + EPS) * g


def _ffn_body(x_ref, gn_ref, wig_ref, wiu_ref, wo_ref, gf_ref, o_ref, xn_ref, acc_ref, *, nf, final_norm):
    f = pl.program_id(1)

    @pl.when(f == 0)
    def _():
        xn_ref[...] = _rms(x_ref[...], gn_ref[...]).astype(BF16)
        acc_ref[...] = jnp.zeros_like(acc_ref)

    xn = xn_ref[...]
    g = _dot(xn, wig_ref[...])
    u = _dot(xn, wiu_ref[...])
    h = (jax.nn.silu(g) * u).astype(BF16)
    acc_ref[...] += _dot(h, wo_ref[...])

    @pl.when(f == nf - 1)
    def _():
        y = x_ref[...] + 0.5 * acc_ref[...]
        if final_norm:
            y = _rms(y, gf_ref[...])
        o_ref[...] = y


def _ffn(x, gn, wi16, wo16, gf, *, final_norm, tm, tf):
    r, d = x.shape
    dff = wo16.shape[0]
    nf = dff // tf
    return pl.pallas_call(
        functools.partial(_ffn_body, nf=nf, final_norm=final_norm),
        grid=(r // tm, nf),
        in_specs=[
            pl.BlockSpec((tm, d), lambda i, f: (i, 0)),
            pl.BlockSpec((1, d), lambda i, f: (0, 0)),
            pl.BlockSpec((d, tf), lambda i, f: (0, f)),
            pl.BlockSpec((d, tf), lambda i, f: (0, nf + f)),
            pl.BlockSpec((tf, d), lambda i, f: (f, 0)),
            pl.BlockSpec((1, d), lambda i, f: (0, 0)),
        ],
        out_specs=pl.BlockSpec((tm, d), lambda i, f: (i, 0)),
        out_shape=jax.ShapeDtypeStruct((r, d), F32),
        scratch_shapes=[pltpu.VMEM((tm, d), BF16), pltpu.VMEM((tm, d), F32)],
        compiler_params=_cparams("parallel", "arbitrary"),
        name="ffn",
    )(x, gn, wi16, wi16, wo16, gf)


C_A = 0
C_BQ = 1024
C_BKV = 2048
C_IQ = 2304
C_KI = 2560
C_MISC = 2688
C_CQ = 2816
C_CK = 3840
C_CV = 4352
C_END = 4864
MISC_LOGF = 0
MISC_WI = 8


def _mixin_outputs(r, tm, nb, leaves_t):
    t = r // nb
    npb = t // tm
    row = lambda i: (i, 0)
    col = lambda i: (0, i)
    bt = lambda i: (i // npb, 0, i % npb)
    rows = lambda w, dt: ((tm, w), row, (r, w), dt)
    tr = lambda w, dt: ((None, w, tm), bt, (nb, w, t), dt)
    outs = dict(oa=rows(512, BF16), qb=rows(1024, BF16), kvb16=rows(256, BF16), qi=rows(256, BF16),
                ki16=rows(128, BF16), misc=rows(128, F32), qc=rows(1024, BF16), kc16=rows(512, BF16),
                vc16=rows(512, BF16), vbt=((LANES, tm), col, (LANES, r), BF16))
    if leaves_t:
        outs.update(kbt32=tr(128, F32), vbt32=tr(128, F32), kit32=tr(HEAD_DIM, F32), lft32=tr(C_HEADS, F32),
                    kct32=tr(512, F32), vct32=tr(512, F32))
    else:
        outs.update(va=rows(512, F32), kvb32=rows(256, F32), ki32=rows(128, F32), kc32=rows(512, F32),
                    vc32=rows(512, F32))
    return outs


def _mixin_body(x_ref, gn_ref, w_ref, lng_ref, lnb_ref, ws_ref, ab_ref, bf_ref, rc_ref, rs_ref, *out_refs,
                tm, names):
    o = dict(zip(names, out_refs))
    leaves_t = "kct32" in o
    chunks = [slice(c * LANES, (c + 1) * LANES) for c in range(tm // LANES)]
    xn = _rms(x_ref[...], gn_ref[...]).astype(BF16)
    rc = rc_ref[...]
    rs = rs_ref[...]
    lane = lax.broadcasted_iota(I32, (tm, LANES), 1)
    first_half = (lane % HEAD_DIM) < (HEAD_DIM // ROT_FRAC // 2)

    def rope(v):
        sw = jnp.where(first_half, pltpu.roll(v, LANES - 8, 1), pltpu.roll(v, 8, 1))
        return v * rc + sw * rs

    def proj(a, b):
        return _dot(xn, w_ref[:, a:b])

    ga = jax.nn.gelu(proj(C_A, C_A + 1024))
    u = ga[:, :512]
    v = ga[:, 512:]
    mu = jnp.mean(v, axis=-1, keepdims=True)
    vc = v - mu
    var = jnp.mean(vc * vc, axis=-1, keepdims=True)
    vn = vc * lax.rsqrt(var + EPS) * lng_ref[...] + lnb_ref[...]
    if not leaves_t:
        o["va"][...] = vn
    vn16 = vn.astype(BF16)
    for rows in chunks:
        for g in range(A_GROUPS):
            cols = slice(g * LANES, (g + 1) * LANES)
            mixed = _dot(ws_ref[g], vn16[rows, cols]) + ab_ref[:, cols]
            o["oa"][rows, cols] = (u[rows, cols] * mixed).astype(BF16)

    for s in range(B_HEADS):
        cols = slice(s * LANES, (s + 1) * LANES)
        o["qb"][:, cols] = (rope(proj(C_BQ + s * LANES, C_BQ + (s + 1) * LANES)) * QK_SCALE).astype(BF16)
    hk = rope(proj(C_BKV, C_BKV + 128))
    hv = proj(C_BKV + 128, C_BKV + 256)
    o["kvb16"][:, :128] = hk.astype(BF16)
    o["kvb16"][:, 128:] = hv.astype(BF16)
    for rows in chunks:
        vt = hv[rows, :].T
        o["vbt"][:, rows] = vt.astype(BF16)
        if leaves_t:
            o["vbt32"][:, rows] = vt
            o["kbt32"][:, rows] = hk[rows, :].T
    for s in range(2):
        cols = slice(s * LANES, (s + 1) * LANES)
        o["qi"][:, cols] = (rope(proj(C_IQ + s * LANES, C_IQ + (s + 1) * LANES)) * IDX_SCALE).astype(BF16)
    hki = rope(proj(C_KI, C_KI + 128))
    o["ki16"][...] = hki.astype(BF16)
    hm = proj(C_MISC, C_MISC + 128)
    lf = jax.nn.log_sigmoid(hm + bf_ref[...])
    misc = jnp.where(lane < MISC_WI, lf, jnp.where(lane < MISC_WI + IDX_HEADS, hm * 0.5, 0.0))
    o["misc"][...] = misc

    for s in range(C_HEADS):
        cols = slice(s * LANES, (s + 1) * LANES)
        o["qc"][:, cols] = (proj(C_CQ + s * LANES, C_CQ + (s + 1) * LANES) * QK_SCALE).astype(BF16)
    hck = proj(C_CK, C_CK + 512)
    o["kc16"][...] = hck.astype(BF16)
    hcv = proj(C_CV, C_CV + 512)
    o["vc16"][...] = hcv.astype(BF16)
    if leaves_t:
        for rows in chunks:
            for j in range(C_HEADS // 2):
                cols = slice(j * LANES, (j + 1) * LANES)
                o["vct32"][cols, rows] = hcv[rows, cols].T
                o["kct32"][cols, rows] = hck[rows, cols].T
            o["kit32"][:, rows] = hki[rows, :].T[0:HEAD_DIM, :]
            o["lft32"][:, rows] = misc[rows, :].T[MISC_LOGF:MISC_LOGF + C_HEADS, :]
    else:
        o["kvb32"][:, :128] = hk
        o["kvb32"][:, 128:] = hv
        o["ki32"][...] = hki
        o["kc32"][...] = hck
        o["vc32"][...] = hcv


def _mixin(x, gn, w2, lng, lnb, ws16, abias, bfg, rc, rs, *, tm, nb, leaves_t):
    r, d = x.shape
    npos = rc.shape[0] // tm
    row = lambda i: (i, 0)
    const2 = lambda i: (0, 0)
    outs = _mixin_outputs(r, tm, nb, leaves_t)
    res = pl.pallas_call(
        functools.partial(_mixin_body, tm=tm, names=tuple(outs)),
        grid=(r // tm,),
        in_specs=[
            pl.BlockSpec((tm, d), row),
            pl.BlockSpec((1, d), const2),
            pl.BlockSpec((d, C_END), const2),
            pl.BlockSpec((1, 512), const2),
            pl.BlockSpec((1, 512), const2),
            pl.BlockSpec((A_GROUPS, LANES, LANES), lambda i: (0, 0, 0)),
            pl.BlockSpec((LANES, 512), const2),
            pl.BlockSpec((1, LANES), const2),
            pl.BlockSpec((tm, LANES), lambda i: (i % npos, 0)),
            pl.BlockSpec((tm, LANES), lambda i: (i % npos, 0)),
        ],
        out_specs=[pl.BlockSpec(blk, imap) for blk, imap, _, _ in outs.values()],
        out_shape=[jax.ShapeDtypeStruct(shape, dt) for _, _, shape, dt in outs.values()],
        compiler_params=_cparams("parallel"),
        name="mix_in",
    )(x, gn, w2, lng, lnb, ws16, abias, bfg, rc, rs)
    return dict(zip(outs, res))


def _cum_body(misc_ref, o_ref, *, t):
    r = lax.broadcasted_iota(I32, (LANES, LANES), 0)
    c = lax.broadcasted_iota(I32, (LANES, LANES), 1)
    tri = jnp.where(c <= r, 1.0, 0.0).astype(F32)
    carry = jnp.zeros((1, LANES), F32)
    for ch in range(t // LANES):
        xs = misc_ref[ch * LANES:(ch + 1) * LANES, :]
        cum = _dot(tri, xs, precision=lax.Precision.HIGHEST) + carry
        carry = cum[LANES - 1:LANES, :]
        o_ref[0, :, ch * LANES:(ch + 1) * LANES] = cum.T[0:C_HEADS, :]


def _cum_t(misc, nb, t):
    return pl.pallas_call(
        functools.partial(_cum_body, t=t),
        grid=(nb,),
        in_specs=[pl.BlockSpec((t, LANES), lambda b: (b, 0))],
        out_specs=pl.BlockSpec((1, C_HEADS, t), lambda b: (b, 0, 0)),
        out_shape=jax.ShapeDtypeStruct((nb, C_HEADS, t), F32),
        compiler_params=_cparams("parallel"),
        name="cum_logf",
    )(misc)


def _fox_body(q_ref, k_ref, v_ref, cum_ref, o_ref, *, tq, tk):
    i = pl.program_id(1)
    lo = lax.broadcasted_iota(I32, (tq, LANES), 1) < HEAD_DIM
    t0 = pl.multiple_of(i * tq, tq)
    nfull = t0 // tk
    qpos = t0 + lax.broadcasted_iota(I32, (2 * tq, tk), 0) % tq
    col = lax.broadcasted_iota(I32, (2 * tq, tk), 1)
    c0 = [cum_ref[0, h:h + 1, pl.ds(t0, LANES)][:, 0:1] for h in range(C_HEADS)]

    def step(j, carry, masked):
        s0 = pl.multiple_of(j * tk, tk)
        out = []
        for p in range(C_HEADS // 2):
            cols = slice(p * LANES, (p + 1) * LANES)
            m, l, acc = carry[p]
            qp = jnp.concatenate([q_ref[:, 2 * p * LANES:(2 * p + 1) * LANES],
                                  q_ref[:, (2 * p + 1) * LANES:(2 * p + 2) * LANES]], axis=0)
            s = _dot_nt(qp, k_ref[pl.ds(s0, tk), cols])
            ba = (c0[2 * p] - cum_ref[0, 2 * p:2 * p + 1, pl.ds(s0, tk)]) * LOG2E
            bb = (c0[2 * p + 1] - cum_ref[0, 2 * p + 1:2 * p + 2, pl.ds(s0, tk)]) * LOG2E
            s = jnp.concatenate([s[:tq] + ba, s[tq:] + bb], axis=0)
            if masked:
                s = jnp.where(s0 + col <= qpos, s, NEG)
            m_new = jnp.maximum(m, jnp.max(s, axis=-1, keepdims=True))
            alpha = jnp.exp2(m - m_new)
            pr = jnp.exp2(s - m_new)
            l = alpha * l + jnp.sum(pr, axis=-1, keepdims=True)
            acc = alpha * acc + _dot(pr.astype(BF16), v_ref[pl.ds(s0, tk), cols])
            out.append((m_new, l, acc))
        return tuple(out)

    init = tuple((jnp.full((2 * tq, 1), NEG, F32), jnp.zeros((2 * tq, 1), F32), jnp.zeros((2 * tq, LANES), F32))
                 for _ in range(C_HEADS // 2))
    carry = lax.fori_loop(0, nfull, functools.partial(step, masked=False), init)
    carry = step(nfull, carry, True)
    for p in range(C_HEADS // 2):
        _, l, acc = carry[p]
        out = acc / l
        o_ref[:, p * LANES:(p + 1) * LANES] = jnp.where(lo, out[:tq], out[tq:]).astype(BF16)


def _fox_prompt(qc16, kc16, vc16, cum_t, *, nb, t, tq, tk):
    nq = t // tq
    w = kc16.shape[1]
    return pl.pallas_call(
        functools.partial(_fox_body, tq=tq, tk=tk),
        grid=(nb, nq),
        in_specs=[
            pl.BlockSpec((tq, C_HEADS * LANES), lambda b, i: (b * nq + i, 0)),
            pl.BlockSpec((t, w), lambda b, i: (b, 0)),
            pl.BlockSpec((t, w), lambda b, i: (b, 0)),
            pl.BlockSpec((1, C_HEADS, t), lambda b, i: (b, 0, 0)),
        ],
        out_specs=pl.BlockSpec((tq, w), lambda b, i: (b * nq + i, 0)),
        out_shape=jax.ShapeDtypeStruct(kc16.shape, BF16),
        compiler_params=_cparams("parallel", "arbitrary"),
        name="fox_prompt",
    )(qc16, kc16, vc16, cum_t)


TOPK_ROWS = 128
TOPK_CH = 512
COUNT_ACC = 32
IDX_BITS = 14
F32_MIN_NORMAL_BITS = 0x00800000


def _pattern_value(key):
    bits = jnp.where(key >= 0, key, key ^ jnp.int32(0x7FFFFFFF))
    bits = jnp.where((bits > 0) & (bits < F32_MIN_NORMAL_BITS), F32_MIN_NORMAL_BITS, bits)
    return pltpu.bitcast(bits, F32)


def _chunk_loop(lo, hi, body, init):
    if isinstance(lo, int) and isinstance(hi, int):
        for c in range(lo, hi):
            init = body(c, init)
        return init
    return lax.fori_loop(lo, hi, body, init)


def _chunk_ds(c, ch):
    return pl.ds(c * ch, ch) if isinstance(c, int) else pl.ds(pl.multiple_of(c * ch, ch), ch)


def _topk_select(s_ref, bias_ref, *, key_axis, nq, nch, ch, k, n_total_ch):
    kf = float(k)
    q_shape = (1, nq) if key_axis == 0 else (nq, 1)
    blk = (ch, nq) if key_axis == 0 else (nq, ch)
    kidx = lax.broadcasted_iota(I32, blk, key_axis)

    def window(c):
        ds = _chunk_ds(c, ch)
        return (ds, slice(None)) if key_axis == 0 else (slice(None), ds)

    def count(pred):
        def body(c, part):
            hit = jnp.where(pred(s_ref[window(c)], c), 1.0, 0.0)
            if key_axis == 0:
                return part + jnp.sum(hit.reshape(ch // COUNT_ACC, COUNT_ACC, nq), axis=0)
            for b in range(ch // LANES):
                part = part + hit[:, b * LANES:(b + 1) * LANES]
            return part
        init = jnp.zeros((COUNT_ACC, nq) if key_axis == 0 else (nq, LANES), F32)
        return jnp.sum(_chunk_loop(0, nch, body, init), axis=key_axis, keepdims=True)

    def vbit(b, key):
        cand = key + (jnp.int32(1) << (jnp.int32(31) - b))
        cv = _pattern_value(cand)
        return jnp.where(count(lambda s, c: s >= cv) >= kf, cand, key)

    thr = _pattern_value(lax.fori_loop(0, 32, vbit, jnp.full(q_shape, INT_MIN, I32)))
    take_all = count(lambda s, c: s > -jnp.inf) <= kf
    thr = jnp.where(take_all, -jnp.inf, thr)
    need = kf - count(lambda s, c: s > thr)
    ties = jnp.where(take_all, 0.0, count(lambda s, c: s == thr))

    def ibit(b, j):
        cand = j + (jnp.int32(1) << (jnp.int32(IDX_BITS - 1) - b))
        return jnp.where(count(lambda s, c: (s == thr) & (c * ch + kidx < cand)) <= need, cand, j)

    jthr = lax.cond(jnp.max(ties - need) > 0.0,
                    lambda: lax.fori_loop(0, IDX_BITS, ibit, jnp.zeros(q_shape, I32)),
                    lambda: jnp.full(q_shape, 1 << IDX_BITS, I32))

    def emit(c, _):
        s = s_ref[window(c)]
        sel = ((s > thr) | ((s == thr) & (c * ch + kidx < jthr))) & (s > -jnp.inf)
        bias_ref[window(c)] = jnp.where(sel, 0.0, NEG).astype(bias_ref.dtype)
        return 0

    _chunk_loop(0, nch, emit, 0)

    def fill(c, _):
        bias_ref[window(c)] = jnp.full(blk, NEG, bias_ref.dtype)
        return 0

    _chunk_loop(nch, n_total_ch, fill, 0)


def _idx_topk_body(qi_ref, misc_ref, ki_ref, bias_ref, s_ref, *, t, k):
    i = pl.program_id(1)
    rows, ch = TOPK_ROWS, TOPK_CH
    q0 = i * rows
    nch = (q0 + rows + ch - 1) // ch
    lo = lax.broadcasted_iota(I32, (rows, LANES), 1) < HEAD_DIM
    qhs = []
    for p in range(IDX_HEADS // 2):
        q2 = qi_ref[:, p * LANES:(p + 1) * LANES]
        zero = jnp.zeros_like(q2)
        qhs += [jnp.where(lo, q2, zero), jnp.where(lo, zero, q2)]
    w_t = misc_ref[...].T
    ws = [w_t[MISC_WI + h:MISC_WI + h + 1, :] for h in range(IDX_HEADS)]
    qpos = q0 + lax.broadcasted_iota(I32, (ch, LANES), 1)
    krow = lax.broadcasted_iota(I32, (ch, LANES), 0)

    def score_chunk(c, _):
        rows_c = _chunk_ds(c, ch)
        score = jnp.zeros((ch, LANES), F32)
        for h in range(IDX_HEADS):
            score = score + ws[h] * jnp.maximum(_dot_nt(ki_ref[rows_c, :], qhs[h]), 0.0)
        s_ref[rows_c, :] = jnp.where(c * ch + krow <= qpos, score, -jnp.inf)
        return 0

    def variant(n):
        def run():
            _chunk_loop(0, n, score_chunk, 0)
            _topk_select(s_ref, bias_ref, key_axis=0, nq=rows, nch=n, ch=ch, k=k, n_total_ch=t // ch)
        return run

    lax.switch(nch - 1, [variant(n) for n in range(1, t // ch + 1)])


def _idx_topk_prompt(qi16, misc, ki16, *, nb, t, k):
    nq = t // TOPK_ROWS
    row = lambda b, i: (b * nq + i, 0)
    return pl.pallas_call(
        functools.partial(_idx_topk_body, t=t, k=k),
        grid=(nb, nq),
        in_specs=[
            pl.BlockSpec((TOPK_ROWS, IDX_HEADS * HEAD_DIM), row),
            pl.BlockSpec((TOPK_ROWS, LANES), row),
            pl.BlockSpec((t, LANES), lambda b, i: (b, 0)),
        ],
        out_specs=pl.BlockSpec((None, t, TOPK_ROWS), lambda b, i: (b * nq + i, 0, 0)),
        out_shape=jax.ShapeDtypeStruct((nb * nq, t, TOPK_ROWS), BF16),
        scratch_shapes=[pltpu.VMEM((t, TOPK_ROWS), F32)],
        compiler_params=_cparams("parallel", "arbitrary"),
        name="idx_topk_prompt",
    )(qi16, misc, ki16)


def _dsa_body(q_ref, k_ref, vt_ref, bias_ref, o_ref):
    tq, ch = TOPK_ROWS, TOPK_CH
    nch = (pl.program_id(1) * tq + tq + ch - 1) // ch
    lane = lax.broadcasted_iota(I32, (tq, LANES), 1)
    lo = lane < HEAD_DIM

    q_all = jnp.concatenate([q_ref[:, h * LANES:(h + 1) * LANES] for h in range(B_HEADS)], axis=0)

    def step(c, carry):
        m, l, acc = carry
        off = pl.multiple_of(c * ch, ch)
        bias = bias_ref[pl.ds(off, ch), :].astype(F32)
        s = _dot_nt(k_ref[pl.ds(off, ch), :], q_all) + jnp.concatenate([bias] * B_HEADS, axis=1)
        m_new = jnp.maximum(m, jnp.max(s, axis=0, keepdims=True))
        alpha = jnp.exp2(m - m_new)
        pr = jnp.exp2(s - m_new)
        l = alpha * l + jnp.sum(pr, axis=0, keepdims=True)
        acc = alpha * acc + _dot(vt_ref[:, pl.ds(off, ch)], pr.astype(BF16))
        return m_new, l, acc

    cols = B_HEADS * tq
    init = (jnp.full((1, cols), 2 * NEG, F32), jnp.zeros((1, cols), F32), jnp.zeros((LANES, cols), F32))
    _, l, acc = lax.fori_loop(0, nch, step, init)
    out = acc / l
    res = [out[:, h * tq:(h + 1) * tq].T for h in range(B_HEADS)]
    for p in range(B_HEADS // 2):
        g = (2 * p) // (B_HEADS // B_KV_HEADS)
        a, b = res[2 * p], res[2 * p + 1]
        if g == 0:
            b = pltpu.roll(b, HEAD_DIM, 1)
        else:
            a = pltpu.roll(a, HEAD_DIM, 1)
        o_ref[:, p * LANES:(p + 1) * LANES] = jnp.where(lo, a, b).astype(BF16)


def _dsa_prompt(qb16, kvb16, vbt16, bias_t, *, nb, t):
    tq = TOPK_ROWS
    nq = t // tq
    row = lambda b, i: (b * nq + i, 0)
    return pl.pallas_call(
        _dsa_body,
        grid=(nb, nq),
        in_specs=[
            pl.BlockSpec((tq, B_HEADS * LANES), row),
            pl.BlockSpec((t, LANES), lambda b, i: (b, 0)),
            pl.BlockSpec((LANES, t), lambda b, i: (0, b)),
            pl.BlockSpec((None, t, tq), lambda b, i: (b * nq + i, 0, 0)),
        ],
        out_specs=pl.BlockSpec((tq, 512), row),
        out_shape=jax.ShapeDtypeStruct((nb * t, 512), BF16),
        compiler_params=_cparams("parallel", "arbitrary"),
        name="dsa_prompt",
    )(qb16, kvb16, vbt16, bias_t)


def _page_specs(n_pages, layer, block):
    def spec(j):
        return pl.BlockSpec((None, None) + block, lambda b, pt: (layer, pt[b, j]) + (0,) * len(block))
    return [spec(j) for j in range(n_pages)]


def _s_score_body(pt_ref, q_ref, w_ref, *refs, n_pages):
    pages, new_ref, o_ref = refs[:n_pages], refs[n_pages], refs[n_pages + 1]
    q = q_ref[...]
    w = w_ref[...]
    for j in range(n_pages + 1):
        s = _dot(q, pages[j][...].astype(BF16)) if j < n_pages else _dot_nt(q, new_ref[...])
        s = jnp.maximum(s, 0.0)
        score = jnp.zeros((8, LANES), F32)
        for h in range(IDX_HEADS):
            score = score + w[:, h:h + 1] * s[h * 8:(h + 1) * 8, :]
        o_ref[:, j * LANES:(j + 1) * LANES] = score[0:4, :]


def _s_score(pt, qi_s, wi_s, cache_kidx, ki_new, *, layer, ns, n_pages):
    lw = (n_pages + 1) * LANES
    return pl.pallas_call(
        functools.partial(_s_score_body, n_pages=n_pages),
        grid_spec=pltpu.PrefetchScalarGridSpec(
            num_scalar_prefetch=1, grid=(ns,),
            in_specs=[pl.BlockSpec((None, 32, HEAD_DIM), lambda b, pt: (b, 0, 0)),
                      pl.BlockSpec((None, 8, IDX_HEADS), lambda b, pt: (b, 0, 0))]
            + _page_specs(n_pages, layer, (HEAD_DIM, LANES))
            + [pl.BlockSpec((None, LANES, HEAD_DIM), lambda b, pt: (b, 0, 0))],
            out_specs=pl.BlockSpec((None, 4, lw), lambda b, pt: (b, 0, 0))),
        out_shape=jax.ShapeDtypeStruct((ns, 4, lw), F32),
        compiler_params=_cparams("arbitrary"),
        name="sample_idx_score",
    )(pt, qi_s, wi_s, *([cache_kidx] * n_pages), ki_new)


def _s_topk_body(s_ref, bias_ref, m_ref, *, past, nt, k, lw):
    rows = TOPK_ROWS
    col = lax.broadcasted_iota(I32, (rows, LANES), 1)
    tq = lax.broadcasted_iota(I32, (rows, LANES), 0) % nt
    nch = lw // LANES
    for c in range(nch):
        cols = slice(c * LANES, (c + 1) * LANES)
        s = s_ref[:, cols]
        if (c + 1) * LANES > past:
            s = jnp.where(c * LANES + col - past <= tq, s, -jnp.inf)
        m_ref[:, cols] = s
    _topk_select(m_ref, bias_ref, key_axis=1, nq=rows, nch=nch, ch=LANES, k=k, n_total_ch=nch)


def _s_topk(scores, *, past, nt, k):
    r, lw = scores.shape
    return pl.pallas_call(
        functools.partial(_s_topk_body, past=past, nt=nt, k=k, lw=lw),
        grid=(r // TOPK_ROWS,),
        in_specs=[pl.BlockSpec((TOPK_ROWS, lw), lambda i: (i, 0))],
        out_specs=pl.BlockSpec((TOPK_ROWS, lw), lambda i: (i, 0)),
        out_shape=jax.ShapeDtypeStruct((r, lw), F32),
        scratch_shapes=[pltpu.VMEM((TOPK_ROWS, lw), F32)],
        compiler_params=_cparams("parallel"),
        name="sample_topk",
    )(scores)


NT_PAD = 8


def _softmax_pv(s_past, s_new, vt_past, v_new):
    m = jnp.maximum(jnp.max(s_past, axis=-1, keepdims=True), jnp.max(s_new, axis=-1, keepdims=True))
    p_past = jnp.exp2(s_past - m)
    p_new = jnp.exp2(s_new - m)
    l = jnp.sum(p_past, axis=-1, keepdims=True) + jnp.sum(p_new, axis=-1, keepdims=True)
    return (_dot_nt(p_past.astype(BF16), vt_past) + _dot(p_new.astype(BF16), v_new)) / l


def _head_t(page_refs, h):
    return jnp.concatenate([r[h] for r in page_refs], axis=1).astype(BF16)


def _seq_page_specs(n_pages, layer, block, spb, s):
    def spec(j):
        return pl.BlockSpec((None, None) + block, lambda b, pt: (layer, pt[b * spb + s, j]) + (0,) * len(block))
    return [spec(j) for j in range(n_pages)]


DSA_SEQS_PER_STEP = 4
FOX_SEQS_PER_STEP = 1


def _s_dsa_body(pt_ref, q_ref, bias_ref, *refs, n_pages, spb):
    pages = refs[:2 * n_pages * spb]
    knew, vnew, o_ref = refs[2 * n_pages * spb:]
    past = n_pages * LANES
    hg = B_HEADS // B_KV_HEADS
    for s in range(spb):
        kps = pages[2 * n_pages * s:2 * n_pages * s + n_pages]
        vps = pages[2 * n_pages * s + n_pages:2 * n_pages * (s + 1)]
        bias = bias_ref[s]
        b_past = jnp.concatenate([bias[:, :past]] * hg, axis=0)
        b_new = jnp.concatenate([bias[:, past:past + NT_PAD]] * hg, axis=0)
        for g in range(B_KV_HEADS):
            q = q_ref[s, g]
            s_past = _dot(q, _head_t(kps, g)) + b_past
            s_new = _dot_nt(q, knew[s, :, g, :].astype(BF16)) + b_new
            o_ref[s, g] = _softmax_pv(s_past, s_new, _head_t(vps, g), vnew[s, :, g, :].astype(BF16))


def _s_dsa(pt, qb_s, bias_s, cache_bk, cache_bv, k_new, v_new, *, layer, ns, n_pages):
    lw = (n_pages + 1) * LANES
    hg = B_HEADS // B_KV_HEADS
    spb = DSA_SEQS_PER_STEP if ns % DSA_SEQS_PER_STEP == 0 else 1
    seq4 = lambda b, pt: (b, 0, 0, 0)
    page = (B_KV_HEADS, HEAD_DIM, LANES)
    page_specs, page_args = [], []
    for s in range(spb):
        page_specs += _seq_page_specs(n_pages, layer, page, spb, s) * 2
        page_args += [cache_bk] * n_pages + [cache_bv] * n_pages
    new = pl.BlockSpec((spb, NT_PAD, B_KV_HEADS, HEAD_DIM), seq4)
    return pl.pallas_call(
        functools.partial(_s_dsa_body, n_pages=n_pages, spb=spb),
        grid_spec=pltpu.PrefetchScalarGridSpec(
            num_scalar_prefetch=1, grid=(ns // spb,),
            in_specs=[pl.BlockSpec((spb, B_KV_HEADS, hg * NT_PAD, HEAD_DIM), seq4),
                      pl.BlockSpec((spb, NT_PAD, lw), lambda b, pt: (b, 0, 0))]
            + page_specs + [new, new],
            out_specs=pl.BlockSpec((spb, B_KV_HEADS, hg * NT_PAD, HEAD_DIM), seq4)),
        out_shape=jax.ShapeDtypeStruct((ns, B_KV_HEADS, hg * NT_PAD, HEAD_DIM), F32),
        compiler_params=_cparams("arbitrary"),
        name="sample_dsa",
    )(pt, qb_s, bias_s, *page_args, k_new, v_new)


def _s_fox_body(pt_ref, q_ref, *refs, n_pages, spb):
    pages = refs[:3 * n_pages * spb]
    knew, vnew, fnew, o_ref = refs[3 * n_pages * spb:]
    hi = lax.Precision.HIGHEST
    r = lax.broadcasted_iota(I32, (LANES, LANES), 0)
    c = lax.broadcasted_iota(I32, (LANES, LANES), 1)
    triu = jnp.where(r <= c, 1.0, 0.0).astype(F32)
    triu_new = triu[:NT_PAD, :NT_PAD]
    causal_new = c[:NT_PAD, :NT_PAD] <= r[:NT_PAD, :NT_PAD]
    for s in range(spb):
        base = 3 * n_pages * s
        kps, vps, fps = (pages[base + i * n_pages:base + (i + 1) * n_pages] for i in range(3))
        off = jnp.zeros((C_HEADS, 1), F32)
        cums = []
        for j in range(n_pages):
            cum = _dot(fps[j][...], triu, precision=hi)
            cums.append(cum + off)
            off = off + cum[:, LANES - 1:LANES]
        cum_past = jnp.concatenate(cums, axis=1) * LOG2E
        cum_new = (_dot(fnew[s], triu_new, precision=hi) + off) * LOG2E
        for h in range(C_HEADS):
            q = q_ref[s, h]
            s_past = _dot(q, _head_t(kps, h)) - cum_past[h:h + 1, :]
            s_new = jnp.where(causal_new, _dot_nt(q, knew[s, :, h, :].astype(BF16)) - cum_new[h:h + 1, :], NEG)
            o_ref[s, h] = _softmax_pv(s_past, s_new, _head_t(vps, h), vnew[s, :, h, :].astype(BF16))


def _s_fox(pt, qc_s, cache_ck, cache_cv, cache_lf, k_new, v_new, lf_new, *, layer, ns, n_pages):
    spb = FOX_SEQS_PER_STEP if ns % FOX_SEQS_PER_STEP == 0 else 1
    seq4 = lambda b, pt: (b, 0, 0, 0)
    page = (C_HEADS, HEAD_DIM, LANES)
    page_specs, page_args = [], []
    for s in range(spb):
        page_specs += _seq_page_specs(n_pages, layer, page, spb, s) * 2
        page_specs += _seq_page_specs(n_pages, layer, (C_HEADS, LANES), spb, s)
        page_args += [cache_ck] * n_pages + [cache_cv] * n_pages + [cache_lf] * n_pages
    new = pl.BlockSpec((spb, NT_PAD, C_HEADS, HEAD_DIM), seq4)
    return pl.pallas_call(
        functools.partial(_s_fox_body, n_pages=n_pages, spb=spb),
        grid_spec=pltpu.PrefetchScalarGridSpec(
            num_scalar_prefetch=1, grid=(ns // spb,),
            in_specs=[pl.BlockSpec((spb, C_HEADS, NT_PAD, HEAD_DIM), seq4)] + page_specs
            + [new, new, pl.BlockSpec((spb, C_HEADS, NT_PAD), lambda b, pt: (b, 0, 0))],
            out_specs=pl.BlockSpec((spb, C_HEADS, NT_PAD, HEAD_DIM), seq4)),
        out_shape=jax.ShapeDtypeStruct((ns, C_HEADS, NT_PAD, HEAD_DIM), F32),
        compiler_params=_cparams("arbitrary"),
        name="sample_fox",
    )(pt, qc_s, *page_args, k_new, v_new, lf_new)


def _mixout_body(x_ref, gn_ref, wg_ref, wb_ref, wo_ref, oa_ref, ob_ref, oc_ref, o_ref):
    x = x_ref[...]
    d = x.shape[1]
    xn = _rms(x, gn_ref[...]).astype(BF16)
    merged = jnp.zeros(x.shape, F32)
    for n, br in enumerate((oa_ref, ob_ref, oc_ref)):
        gate = jax.nn.sigmoid(_dot(xn, wg_ref[:, n * d:(n + 1) * d]))
        merged = merged + _dot(br[...], wb_ref[n]) * gate
    o_ref[...] = x + _dot(merged.astype(BF16), wo_ref[...])


def _mixout(x, gn, wg16, wb16, wo16, oa, ob, oc, *, tm):
    r, d = x.shape
    wbr = oa.shape[1]
    row = lambda i: (i, 0)
    const2 = lambda i: (0, 0)
    return pl.pallas_call(
        _mixout_body,
        grid=(r // tm,),
        in_specs=[
            pl.BlockSpec((tm, d), row),
            pl.BlockSpec((1, d), const2),
            pl.BlockSpec((d, N_BRANCH * d), const2),
            pl.BlockSpec((N_BRANCH, wbr, d), lambda i: (0, 0, 0)),
            pl.BlockSpec((d, d), const2),
            pl.BlockSpec((tm, wbr), row),
            pl.BlockSpec((tm, wbr), row),
            pl.BlockSpec((tm, wbr), row),
        ],
        out_specs=pl.BlockSpec((tm, d), row),
        out_shape=jax.ShapeDtypeStruct((r, d), F32),
        compiler_params=_cparams("parallel"),
        name="mix_out",
    )(x, gn, wg16, wb16, wo16, oa, ob, oc)


def _rope_tables(pos):
    rot = HEAD_DIM // ROT_FRAC
    half = rot // 2
    inv = ROPE_THETA ** (-jnp.arange(half, dtype=F32) / half)
    ang = pos.astype(F32)[:, None] * inv[None, :]
    cos, sin = jnp.cos(ang), jnp.sin(ang)
    ones = jnp.ones((pos.shape[0], HEAD_DIM - rot), F32)
    c64 = jnp.concatenate([cos, cos, ones], axis=1)
    s64 = jnp.concatenate([-sin, sin, 0.0 * ones], axis=1)
    return jnp.tile(c64, (1, LANES // HEAD_DIM)), jnp.tile(s64, (1, LANES // HEAD_DIM))


def _layer_weights(l, norm_ffn1, ffn1_wi, ffn1_wo, norm_mix, w_in, b_forget, a_ln_g, a_ln_b, a_ws, a_bs,
                   w_branch, w_out, norm_ffn2, ffn2_wi, ffn2_wo, nt):
    d = w_in.shape[1]
    wt = w_in.transpose(2, 0, 1)[:, l, :].astype(BF16)
    w = lax.dot_general(jnp.eye(d, dtype=BF16), wt, _NT, preferred_element_type=BF16)
    widths = (N_BRANCH * d, d, 512, 128, 128, 256, 64, 4, 512, 512, 512, 8)
    cuts = np.concatenate([[0], np.cumsum(widths)])
    (w_gate, w_a, w_bq, w_bk, w_bv, w_iq, w_ik, w_iw, w_cq, w_ck, w_cv, w_cf) = [
        w[:, int(cuts[i]):int(cuts[i + 1])] for i in range(len(widths))]
    slots = []
    for h in range(B_HEADS):
        g = h // (B_HEADS // B_KV_HEADS)
        wh = w_bq[:, h * HEAD_DIM:(h + 1) * HEAD_DIM]
        slots.append(jnp.pad(wh, ((0, 0), (g * HEAD_DIM, (B_KV_HEADS - 1 - g) * HEAD_DIM))))
    cslots = []
    for h in range(C_HEADS):
        wh = w_cq[:, h * HEAD_DIM:(h + 1) * HEAD_DIM]
        cslots.append(jnp.pad(wh, ((0, 0), ((h % 2) * HEAD_DIM, (1 - h % 2) * HEAD_DIM))))
    w_misc = jnp.pad(jnp.concatenate([w_cf, w_iw], axis=1), ((0, 0), (0, LANES - 12)))
    w2 = jnp.concatenate([w_a] + slots + [w_bk, w_bv, w_iq, w_ik, w_ik, w_misc] + cslots + [w_ck, w_cv], axis=1)
    ws = a_ws[l]
    tril = jnp.tril(jnp.ones((LANES, LANES), bool))
    ws_p = jnp.where(tril, ws, 0.0)
    corner = jnp.where(tril[:nt, :nt], ws[:, :nt, :nt], 0.0)
    ws_s = jnp.einsum("ij,gts->gitjs", jnp.eye(LANES // nt, dtype=F32), corner).reshape(A_GROUPS, LANES, LANES)
    bs = a_bs[l]
    ab_p = jnp.repeat(bs.T, LANES, axis=1)
    ab_s = jnp.repeat(jnp.tile(bs[:, :nt].T, (LANES // nt, 1)), LANES, axis=1)
    return dict(
        n1=norm_ffn1[l][None], wi1=ffn1_wi[l].astype(BF16), wo1=ffn1_wo[l].astype(BF16),
        n2=norm_ffn2[l][None], wi2=ffn2_wi[l].astype(BF16), wo2=ffn2_wo[l].astype(BF16),
        nm=norm_mix[l][None], w2=w2.astype(BF16), wg=w_gate.astype(BF16),
        wb=w_branch[l].astype(BF16), wo=w_out[l].astype(BF16),
        lng=a_ln_g[l][None], lnb=a_ln_b[l][None],
        ws_p=ws_p.astype(BF16), ws_s=ws_s.astype(BF16), ab_p=ab_p, ab_s=ab_s,
        bf=jnp.pad(b_forget[l], (0, LANES - C_HEADS))[None],
    )


def _pad_rows(a, n):
    return jnp.pad(a, ((0, 0), (0, n - a.shape[1]), (0, 0)))


def _slot_heads(x, ns, nt, lane_half):
    x = x.reshape(ns, nt, -1, LANES)
    heads = [x[:, :, h, lane_half(h) * HEAD_DIM:(lane_half(h) + 1) * HEAD_DIM] for h in range(x.shape[2])]
    return jnp.pad(jnp.stack(heads, axis=1), ((0, 0), (0, 0), (0, NT_PAD - nt), (0, 0)))


def _new_rows(x, ns, nt):
    x = x.reshape(ns, nt, -1, HEAD_DIM)
    return jnp.pad(x, ((0, 0), (0, NT_PAD - nt), (0, 0), (0, 0)))


def _unpad_heads(o, nt):
    ns, nh = o.shape[:2]
    return o[:, :, :nt].transpose(0, 2, 1, 3).reshape(ns * nt, nh * HEAD_DIM).astype(BF16)


def kernel(x_prompt, x_sample, cache_b_k, cache_b_v, cache_b_kidx, cache_c_k, cache_c_v, cache_c_logf, page_table,
           norm_ffn1, ffn1_wi, ffn1_wo, norm_mix, w_in, b_forget, a_ln_g, a_ln_b, a_ws, a_bs, w_branch, w_out,
           norm_ffn2, ffn2_wi, ffn2_wo, norm_final):
    nb, t, d = x_prompt.shape
    ns, nt, _ = x_sample.shape
    depth = w_in.shape[0]
    n_pages = page_table.shape[1]
    page = cache_b_k.shape[2]
    past = n_pages * page
    assert page == LANES and d == 1024 and t % 512 == 0 and (ns * nt) % TOPK_ROWS == 0 and LANES % nt == 0
    assert nt <= NT_PAD
    k_prompt = min(TOPK_MAX, t // 4)
    k_sample = min(TOPK_MAX, (past + nt) // 4)
    rs_rows = ns * nt
    tm_s = min(256, rs_rows)

    pos_p = jnp.arange(t, dtype=I32)
    pos_s = past + (jnp.arange(rs_rows, dtype=I32) % nt)
    rc_p, rs_p = _rope_tables(pos_p)
    rc_s, rs_s = _rope_tables(pos_s)
    gfin = norm_final[None]

    cbk = cache_b_k.transpose(0, 1, 3, 4, 2)
    cbv = cache_b_v.transpose(0, 1, 3, 4, 2)
    cki = cache_b_kidx.transpose(0, 1, 3, 2)
    cck = cache_c_k.transpose(0, 1, 3, 4, 2)
    ccv = cache_c_v.transpose(0, 1, 3, 4, 2)
    clf = cache_c_logf.transpose(0, 1, 3, 2)

    hp = x_prompt.reshape(nb * t, d)
    hs = x_sample.reshape(rs_rows, d)
    rows_p, rows_s = [], []
    for l in range(depth):
        lw = _layer_weights(l, norm_ffn1, ffn1_wi, ffn1_wo, norm_mix, w_in, b_forget, a_ln_g, a_ln_b, a_ws, a_bs,
                            w_branch, w_out, norm_ffn2, ffn2_wi, ffn2_wo, nt)
        last = l == depth - 1
        hp = _ffn(hp, lw["n1"], lw["wi1"], lw["wo1"], gfin, final_norm=False, tm=512, tf=1408)
        mp = _mixin(hp, lw["nm"], lw["w2"], lw["lng"], lw["lnb"], lw["ws_p"], lw["ab_p"], lw["bf"], rc_p, rs_p,
                    tm=256, nb=nb, leaves_t=True)
        oc = _fox_prompt(mp["qc"], mp["kc16"], mp["vc16"], _cum_t(mp["misc"], nb, t), nb=nb, t=t, tq=128, tk=512)
        bias = _idx_topk_prompt(mp["qi"], mp["misc"], mp["ki16"], nb=nb, t=t, k=k_prompt)
        ob = _dsa_prompt(mp["qb"], mp["kvb16"], mp["vbt"], bias, nb=nb, t=t)
        hp = _mixout(hp, lw["nm"], lw["wg"], lw["wb"], lw["wo"], mp["oa"], ob, oc, tm=256)
        hp = _ffn(hp, lw["n2"], lw["wi2"], lw["wo2"], gfin, final_norm=last, tm=512, tf=1408)
        rows_p.append(mp)
        hs = _ffn(hs, lw["n1"], lw["wi1"], lw["wo1"], gfin, final_norm=False, tm=tm_s, tf=1408)
        ms = _mixin(hs, lw["nm"], lw["w2"], lw["lng"], lw["lnb"], lw["ws_s"], lw["ab_s"], lw["bf"], rc_s, rs_s,
                    tm=tm_s, nb=1, leaves_t=False)
        oa, va, qb, kvb32, qi, ki32, ki16, misc, qc, kc32, vc32 = (
            ms[n] for n in ("oa", "va", "qb", "kvb32", "qi", "ki32", "ki16", "misc", "qc", "kc32", "vc32"))
        qi_s = _pad_rows(qi.reshape(ns, nt, IDX_HEADS, HEAD_DIM).transpose(0, 2, 1, 3).reshape(ns * IDX_HEADS, nt, HEAD_DIM),
                         8).reshape(ns, IDX_HEADS * 8, HEAD_DIM)
        wi_s = _pad_rows(misc[:, MISC_WI:MISC_WI + IDX_HEADS].reshape(ns, nt, IDX_HEADS), 8)
        ki_new = _pad_rows(ki16[:, :HEAD_DIM].reshape(ns, nt, HEAD_DIM), page)
        scores = _s_score(page_table, qi_s, wi_s, cki, ki_new, layer=l, ns=ns, n_pages=n_pages)
        bias_s = _s_topk(scores.reshape(rs_rows, -1), past=past, nt=nt, k=k_sample).reshape(ns, nt, -1)
        hg = B_HEADS // B_KV_HEADS
        qb_s = _slot_heads(qb, ns, nt, lambda h: h // hg).reshape(ns, B_KV_HEADS, hg * NT_PAD, HEAD_DIM)
        ob_raw = _s_dsa(page_table, qb_s, _pad_rows(bias_s, NT_PAD), cbk, cbv,
                        _new_rows(kvb32[:, :128], ns, nt), _new_rows(kvb32[:, 128:], ns, nt),
                        layer=l, ns=ns, n_pages=n_pages)
        ob = _unpad_heads(ob_raw.reshape(ns, B_HEADS, NT_PAD, HEAD_DIM), nt)
        qc_s = _slot_heads(qc, ns, nt, lambda h: h % 2)
        lf_new = _pad_rows(misc[:, MISC_LOGF:MISC_LOGF + C_HEADS].reshape(ns, nt, C_HEADS), NT_PAD).transpose(0, 2, 1)
        oc_raw = _s_fox(page_table, qc_s, cck, ccv, clf,
                        _new_rows(kc32, ns, nt), _new_rows(vc32, ns, nt), lf_new, layer=l, ns=ns, n_pages=n_pages)
        oc = _unpad_heads(oc_raw, nt)
        hs = _mixout(hs, lw["nm"], lw["wg"], lw["wb"], lw["wo"], oa, ob, oc, tm=tm_s)
        hs = _ffn(hs, lw["n2"], lw["wi2"], lw["wo2"], gfin, final_norm=last, tm=tm_s, tf=1408)
        rows_s.append(dict(k_b=kvb32[:, :128], v_b=kvb32[:, 128:], k_i=ki32[:, :HEAD_DIM], k_c=kc32, v_c=vc32,
                           logf=misc[:, MISC_LOGF:MISC_LOGF + C_HEADS], v_a=va))

    def stack(rows, key, shape):
        return jnp.stack([r[key] for r in rows]).reshape((depth,) + shape)

    def stack_t(key, heads):
        a = jnp.stack([r[key] for r in rows_p])
        if heads is None:
            return a.transpose(0, 1, 3, 2)
        return a.reshape(depth, nb, heads, -1, t).transpose(0, 1, 4, 2, 3)

    y_prompt = hp.reshape(nb, t, d)
    y_sample = hs.reshape(ns, nt, d)
    lead = (ns, nt)
    outs = [y_prompt, y_sample,
            stack_t("kbt32", B_KV_HEADS), stack_t("vbt32", B_KV_HEADS), stack_t("kit32", None),
            stack_t("kct32", C_HEADS), stack_t("vct32", C_HEADS), stack_t("lft32", None),
            stack(rows_s, "k_b", lead + (B_KV_HEADS, HEAD_DIM)), stack(rows_s, "v_b", lead + (B_KV_HEADS, HEAD_DIM)),
            stack(rows_s, "k_i", lead + (HEAD_DIM,)), stack(rows_s, "k_c", lead + (C_HEADS, HEAD_DIM)),
            stack(rows_s, "v_c", lead + (C_HEADS, HEAD_DIM)), stack(rows_s, "logf", lead + (C_HEADS,)),
            stack(rows_s, "v_a", (ns, nt, 512))]
    return tuple(outs)
```
